```python
import math
import jax, jax.numpy as jnp
from jax import lax
import numpy as np


D_MODEL = 1024
BATCH = 8
SEQ = 2048
DEPTH = 2

CHUNK = 64
Q_BLOCK = 128
N_EVEN = (DEPTH + 1) // 2
N_ODD = DEPTH // 2

MLA_HEADS = 8
MLA_Q_LORA = 384
MLA_KV_LORA = 256
MLA_NOPE = 64
MLA_ROPE = 32
MLA_V = 64
SB_HEADS = 8
SB_HEAD_DIM = 64
SB_W = SB_HEADS * SB_HEAD_DIM
AB_WIDTH = MLA_HEADS * MLA_V + SB_W
AB_IN = MLA_Q_LORA + MLA_KV_LORA + MLA_ROPE + 3 * SB_W + AB_WIDTH
DIFF_HEADS = 8
DIFF_HEAD_DIM = 64
DIFF_QK = DIFF_HEADS * 2 * DIFF_HEAD_DIM
DIFF_WIDTH = DIFF_HEADS * 2 * DIFF_HEAD_DIM
DIFF_IN = 2 * DIFF_QK + 2 * DIFF_WIDTH

REL_BUCKETS = 32
REL_MAX_DIST = 128
ROPE_THETA = 10000.0
NORM_EPS = 1e-6
SUBLN_EPS = 1e-5
NEG_INF = -1e30

kernel_name = 'hybrid_mla_stickbreak_diffattn_encoder'


def rms_norm(x, g, eps=NORM_EPS):
    xf = x.astype(jnp.float32)
    y = xf * lax.rsqrt(jnp.mean(xf * xf, axis=-1, keepdims=True) + eps)
    return (y * g.astype(jnp.float32)).astype(x.dtype)


def rope(x, pos):
    half = x.shape[-1] // 2
    inv_freq = ROPE_THETA ** (-jnp.arange(half, dtype=jnp.float32) / half)
    ang = pos.astype(jnp.float32)[..., None, None] * inv_freq
    cos, sin = jnp.cos(ang), jnp.sin(ang)
    xf = x.astype(jnp.float32)
    x1, x2 = xf[..., :half], xf[..., half:]
    return jnp.concatenate([x1 * cos - x2 * sin, x1 * sin + x2 * cos], axis=-1).astype(x.dtype)


def t5_bucket(rel):
    nb = REL_BUCKETS // 2
    max_exact = nb // 2
    ret = jnp.where(rel > 0, nb, 0)
    n = jnp.abs(rel)
    nf = jnp.maximum(n, 1).astype(jnp.float32)
    large = max_exact + (jnp.log(nf / max_exact) / math.log(REL_MAX_DIST / max_exact) * (nb - max_exact)).astype(jnp.int32)
    large = jnp.minimum(large, nb - 1)
    return ret + jnp.where(n < max_exact, n, large)


def chunk_mask(qs, ke):
    tq = qs + jnp.arange(Q_BLOCK)
    tk = jnp.arange(ke)
    return (tk // CHUNK)[None, :] <= (tq // CHUNK)[:, None]


def sweep(block_fn, seq):
    return jnp.concatenate([block_fn(i * Q_BLOCK, (i + 1) * Q_BLOCK) for i in range(seq // Q_BLOCK)], axis=1)


def mla_attention(q, k, v):
    scale = (MLA_NOPE + MLA_ROPE) ** -0.5

    def block(qs, ke):
        s = jnp.einsum('bqhd,bkhd->bhqk', q[:, qs:ke], k[:, :ke]).astype(jnp.float32) * scale
        s = jnp.where(chunk_mask(qs, ke), s, NEG_INF)
        p = jax.nn.softmax(s, axis=-1).astype(v.dtype)
        return jnp.einsum('bhqk,bkhd->bqhd', p, v[:, :ke])

    return sweep(block, q.shape[1])


def stick_breaking_attention(q, k, v):
    scale = SB_HEAD_DIM ** -0.5

    def block(qs, ke):
        z = jnp.einsum('bqhd,bkhd->bhqk', q[:, qs:ke], k[:, :ke]).astype(jnp.float32) * scale
        tq = qs + jnp.arange(Q_BLOCK)
        tk = jnp.arange(ke)
        m = tk[None, :] < tq[:, None]
        log_1m = jnp.where(m, jax.nn.log_sigmoid(-z), 0.0)
        csum = jnp.cumsum(log_1m, axis=-1)
        log_w = jax.nn.log_sigmoid(z) + csum[..., -1:] - csum
        w = jnp.where(m, jnp.exp(log_w), 0.0).astype(v.dtype)
        return jnp.einsum('bhqk,bkhd->bqhd', w, v[:, :ke])

    return sweep(block, q.shape[1])


def layer_ab(h, pos, w_in, q_norm_g, kv_norm_g, w_uq, w_ukv, w_out):
    B, S, _ = h.shape
    proj = h @ w_in
    cuts = np.cumsum([MLA_Q_LORA, MLA_KV_LORA, MLA_ROPE, SB_W, SB_W, SB_W]).tolist()
    cq, ckv, kr, q_sb, k_sb, v_sb, z = jnp.split(proj, cuts, axis=-1)
    q = (rms_norm(cq, q_norm_g) @ w_uq).reshape(B, S, MLA_HEADS, MLA_NOPE + MLA_ROPE)
    q = jnp.concatenate([q[..., :MLA_NOPE], rope(q[..., MLA_NOPE:], pos)], axis=-1)
    kv = (rms_norm(ckv, kv_norm_g) @ w_ukv).reshape(B, S, MLA_HEADS, MLA_NOPE + MLA_V)
    k_rope = jnp.broadcast_to(rope(kr[:, :, None, :], pos), (B, S, MLA_HEADS, MLA_ROPE))
    k = jnp.concatenate([kv[..., :MLA_NOPE], k_rope], axis=-1)
    o_mla = mla_attention(q, k, kv[..., MLA_NOPE:]).reshape(B, S, MLA_HEADS * MLA_V)
    shp = (B, S, SB_HEADS, SB_HEAD_DIM)
    o_sb = stick_breaking_attention(q_sb.reshape(shp), k_sb.reshape(shp), v_sb.reshape(shp)).reshape(B, S, SB_W)
    y = jnp.concatenate([o_mla, o_sb], axis=-1) * jax.nn.silu(z)
    return y @ w_out


def layer_diff(h, rel_table, w_in, lq1, lk1, lq2, lk2, subln_g, w_out, lam_init):
    B, S, _ = h.shape
    proj = h @ w_in
    q, k, v, z = jnp.split(proj, [DIFF_QK, 2 * DIFF_QK, 2 * DIFF_QK + DIFF_WIDTH], axis=-1)
    q = q.reshape(B, S, DIFF_HEADS, 2, DIFF_HEAD_DIM)
    k = k.reshape(B, S, DIFF_HEADS, 2, DIFF_HEAD_DIM)
    v = v.reshape(B, S, DIFF_HEADS, 2 * DIFF_HEAD_DIM)
    f32 = jnp.float32
    lam = (jnp.exp(jnp.sum(lq1.astype(f32) * lk1.astype(f32)))
           - jnp.exp(jnp.sum(lq2.astype(f32) * lk2.astype(f32))) + lam_init)
    scale = DIFF_HEAD_DIM ** -0.5
    table = rel_table.astype(f32)

    def block(qs, ke):
        tq = qs + jnp.arange(Q_BLOCK)
        tk = jnp.arange(ke)
        bias = table[t5_bucket(tk[None, :] - tq[:, None])].transpose(2, 0, 1)
        s = jnp.einsum('bqhcd,bkhcd->cbhqk', q[:, qs:ke], k[:, :ke]).astype(f32) * scale + bias
        s = jnp.where(chunk_mask(qs, ke), s, NEG_INF)
        p = jax.nn.softmax(s, axis=-1)
        a = (p[0] - lam * p[1]).astype(v.dtype)
        return jnp.einsum('bhqk,bkhd->bqhd', a, v[:, :ke])

    o = sweep(block, S)
    o = rms_norm(o, subln_g, SUBLN_EPS) * (1.0 - lam_init)
    y = o.reshape(B, S, DIFF_WIDTH) * jax.nn.silu(z)
    return y @ w_out


def setup_inputs(seed: int = 0) -> dict:
    key = jax.random.key(seed)
    ks = jax.random.split(key, 24)
    nrm = lambda k, shape, s: jax.random.normal(k, shape, jnp.float32) * s
    D = D_MODEL
    return {
        'x': nrm(ks[0], (BATCH, SEQ, D), 1.0),
        'c': nrm(ks[1], (BATCH, D), 1.0),
        'pos_offset': (jax.random.randint(ks[2], (BATCH,), 0, 64) * CHUNK).astype(jnp.int32),
        'rel_bias_table': nrm(ks[3], (REL_BUCKETS, DIFF_HEADS), 0.5),
        'ada_w': nrm(ks[4], (DEPTH, D, 3 * D), D ** -0.5),
        'ada_b': nrm(ks[5], (DEPTH, 3 * D), 0.01),
        'norm_g': 1.0 + nrm(ks[6], (DEPTH, D), 0.01),
        'final_g': 1.0 + nrm(ks[7], (D,), 0.01),
        'ab_w_in': nrm(ks[8], (N_EVEN, D, AB_IN), D ** -0.5),
        'ab_q_norm_g': 1.0 + nrm(ks[9], (N_EVEN, MLA_Q_LORA), 0.01),
        'ab_kv_norm_g': 1.0 + nrm(ks[10], (N_EVEN, MLA_KV_LORA), 0.01),
        'ab_w_uq': nrm(ks[11], (N_EVEN, MLA_Q_LORA, MLA_HEADS * (MLA_NOPE + MLA_ROPE)), MLA_Q_LORA ** -0.5),
        'ab_w_ukv': nrm(ks[12], (N_EVEN, MLA_KV_LORA, MLA_HEADS * (MLA_NOPE + MLA_V)), MLA_KV_LORA ** -0.5),
        'ab_w_out': nrm(ks[13], (N_EVEN, AB_WIDTH, D), AB_WIDTH ** -0.5),
        'dif_w_in': nrm(ks[14], (N_ODD, D, DIFF_IN), D ** -0.5),
        'dif_lam_q1': nrm(ks[15], (N_ODD, DIFF_HEAD_DIM), 0.1),
        'dif_lam_k1': nrm(ks[16], (N_ODD, DIFF_HEAD_DIM), 0.1),
        'dif_lam_q2': nrm(ks[17], (N_ODD, DIFF_HEAD_DIM), 0.1),
        'dif_lam_k2': nrm(ks[18], (N_ODD, DIFF_HEAD_DIM), 0.1),
        'dif_subln_g': 1.0 + nrm(ks[19], (N_ODD, 2 * DIFF_HEAD_DIM), 0.01),
        'dif_w_out': nrm(ks[20], (N_ODD, DIFF_WIDTH, D), DIFF_WIDTH ** -0.5),
    }


def reference(x, c, pos_offset, rel_bias_table, ada_w, ada_b, norm_g, final_g,
              ab_w_in, ab_q_norm_g, ab_kv_norm_g, ab_w_uq, ab_w_ukv, ab_w_out,
              dif_w_in, dif_lam_q1, dif_lam_k1, dif_lam_q2, dif_lam_k2, dif_subln_g, dif_w_out):
    B, S, _ = x.shape
    pos = pos_offset[:, None] + jnp.arange(S, dtype=jnp.int32)[None, :]
    c_act = jax.nn.silu(c)
    for i in range(DEPTH):
        mod = c_act @ ada_w[i] + ada_b[i]
        shift, scale, gate = jnp.split(mod, 3, axis=-1)
        h = rms_norm(x, norm_g[i]) * (1.0 + scale[:, None, :]) + shift[:, None, :]
        j = i // 2
        if i % 2 == 0:
            out = layer_ab(h, pos, ab_w_in[j], ab_q_norm_g[j], ab_kv_norm_g[j],
                           ab_w_uq[j], ab_w_ukv[j], ab_w_out[j])
        else:
            lam_init = 0.8 - 0.6 * math.exp(-0.3 * i)
            out = layer_diff(h, rel_bias_table, dif_w_in[j], dif_lam_q1[j], dif_lam_k1[j],
                             dif_lam_q2[j], dif_lam_k2[j], dif_subln_g[j], dif_w_out[j], lam_init)
        x = x + gate[:, None, :] * out
    return rms_norm(x, final_g)
```

```python
import functools
import math

import numpy as np
import jax
import jax.numpy as jnp
from jax import lax
from jax.experimental import pallas as pl
from jax.experimental.pallas import tpu as pltpu

D_MODEL = 1024
DEPTH = 2
CHUNK = 64

MLA_HEADS = 8
MLA_Q_LORA = 384
MLA_KV_LORA = 256
MLA_NOPE = 64
MLA_ROPE = 32
MLA_V = 64
SB_HEADS = 8
SB_HEAD_DIM = 64
SB_W = SB_HEADS * SB_HEAD_DIM
DIFF_HEADS = 8
DIFF_HEAD_DIM = 64
DIFF_W = DIFF_HEADS * 2 * DIFF_HEAD_DIM

REL_BUCKETS = 32
REL_MAX_DIST = 128
ROPE_THETA = 10000.0
NORM_EPS = 1e-6
SUBLN_EPS = 1e-5
NEG_INF = -1e30

LANES = 128
V7X_VMEM_BYTES = 64 * 1024 * 1024
VMEM_LIMIT = V7X_VMEM_BYTES * 7 // 8

TM = 512
TQ = 256
TK = 256

F32 = jnp.float32
BF16 = jnp.bfloat16


def _silu(z):
    return z * (1.0 / (1.0 + jnp.exp(-z)))


def _dot(a, b):
    return jnp.dot(a, b, preferred_element_type=F32)


def _dot_nt(a, b):
    return lax.dot_general(a, b, (((1,), (1,)), ((), ())), preferred_element_type=F32)


def _rep(x, n):
    return x if n == 1 else jnp.concatenate([x] * n, axis=1)


def _params(n_axes=1):
    return pltpu.CompilerParams(dimension_semantics=("arbitrary",) * n_axes, vmem_limit_bytes=VMEM_LIMIT)


def _mod_kernel(c_ref, w_ref, b_ref, o_ref):
    ca = _silu(c_ref[...]).astype(BF16)
    o_ref[0] = _dot(ca, w_ref[0].astype(BF16)) + b_ref[0]


def _modulation(c, ada_w, ada_b):
    B, D = c.shape
    out = pl.pallas_call(
        _mod_kernel,
        name="ada_mod",
        grid=(DEPTH, 3),
        in_specs=[
            pl.BlockSpec((B, D), lambda i, j: (0, 0)),
            pl.BlockSpec((1, D, D), lambda i, j: (i, 0, j)),
            pl.BlockSpec((1, 1, D), lambda i, j: (i * 3 + j, 0, 0)),
        ],
        out_specs=pl.BlockSpec((1, B, D), lambda i, j: (i * 3 + j, 0, 0)),
        out_shape=jax.ShapeDtypeStruct((DEPTH * 3, B, D), F32),
        compiler_params=_params(2),
    )(c, ada_w, ada_b.reshape(DEPTH * 3, 1, D))
    return out.reshape(DEPTH * 3 * B, 1, D)


def _rope_kernel(pos_ref, f_ref, cos_ref, sin_ref):
    ang = pos_ref[...].astype(F32) * f_ref[...]
    cos_ref[...] = jnp.cos(ang)
    sin_ref[...] = jnp.sin(ang)


def _rope_tables(pos_offset, S):
    B = pos_offset.shape[0]
    half = MLA_ROPE // 2
    per_row = LANES // half
    inv_freq = ROPE_THETA ** (-jnp.arange(half, dtype=F32) / half)
    pos = pos_offset[:, None] + jnp.arange(S, dtype=jnp.int32)[None, :]
    pos = jnp.repeat(pos.reshape(B * S // per_row, per_row), half, axis=1)
    f = jnp.tile(inv_freq, per_row)[None, :]
    rows = B * S // per_row
    cos, sin = pl.pallas_call(
        _rope_kernel,
        name="rope_tab",
        out_shape=[jax.ShapeDtypeStruct((rows, LANES), F32)] * 2,
    )(pos, f)
    cos = cos.reshape(B * S, half)
    sin = sin.reshape(B * S, half)
    one = jnp.ones((B * S, MLA_NOPE), F32)
    zn = jnp.zeros((B * S, MLA_NOPE), F32)
    zp = jnp.zeros((B * S, LANES - MLA_NOPE - MLA_ROPE), F32)
    ctab = jnp.concatenate([one, cos, cos, zp], axis=1)
    stab = jnp.concatenate([zn, -sin, sin, zp], axis=1)
    return ctab, stab


AB_SEG = (MLA_Q_LORA, MLA_KV_LORA, SB_W, SB_W, SB_W, MLA_HEADS * MLA_V + SB_W, LANES, LANES)
AB_OFF = tuple(int(v) for v in np.cumsum((0,) + AB_SEG))


def _rms(x, eps):
    return x * lax.rsqrt(jnp.mean(x * x, axis=-1, keepdims=True) + eps)


def _in0_kernel(x_ref, shift_ref, scale_ref, g_ref, w_ref, wqa_ref, wqb_ref, wk_ref, wv_ref,
                qg_ref, kvg_ref, c_ref, s_ref,
                qm_ref, km_ref, vm_ref, qs_ref, ks_ref, vs_ref, z_ref):
    x = x_ref[...]
    h = _rms(x, NORM_EPS) * g_ref[...] * (1.0 + scale_ref[0]) + shift_ref[0]
    h = h.astype(BF16)

    def seg(i):
        return _dot(h, w_ref[:, AB_OFF[i]:AB_OFF[i + 1]])

    ctab = c_ref[...]
    stab = s_ref[...]
    nh = MLA_HEADS

    cq = (_rms(seg(0), NORM_EPS) * qg_ref[...]).astype(BF16)
    q = _dot(cq, wqa_ref[...]) * _rep(ctab, nh) + _dot(cq, wqb_ref[...]) * _rep(stab, nh)
    qm_ref[...] = (q * ((MLA_NOPE + MLA_ROPE) ** -0.5)).astype(BF16)

    ckv = (_rms(seg(1), NORM_EPS) * kvg_ref[...]).astype(BF16)
    krope = seg(6) * ctab + seg(7) * stab
    km_ref[...] = (_dot(ckv, wk_ref[...]) + _rep(krope, nh)).astype(BF16)
    vm_ref[...] = _dot(ckv, wv_ref[...]).astype(BF16)

    qs_ref[...] = seg(2).astype(BF16)
    ks_ref[...] = seg(3).astype(BF16)
    vs_ref[...] = seg(4).astype(BF16)
    z_ref[...] = seg(5).astype(BF16)


def _prep_ab_weights(w_in, w_uq, w_ukv):
    D = w_in.shape[0]
    c = np.cumsum([MLA_Q_LORA, MLA_KV_LORA, MLA_ROPE, SB_W, SB_W, SB_W]).tolist()
    half = MLA_ROPE // 2
    w_kr = w_in[:, c[1]:c[2]]
    zl = jnp.zeros((D, MLA_NOPE), w_in.dtype)
    zr = jnp.zeros((D, LANES - MLA_NOPE - MLA_ROPE), w_in.dtype)
    kr_a = jnp.concatenate([zl, w_kr, zr], axis=1)
    kr_b = jnp.concatenate([zl, w_kr[:, half:], w_kr[:, :half], zr], axis=1)
    w0 = jnp.concatenate([
        w_in[:, :c[1]],
        w_in[:, c[2]:c[3]] * SB_HEAD_DIM ** -0.5,
        w_in[:, c[3]:],
        kr_a, kr_b], axis=1).astype(BF16)

    hq = MLA_NOPE + MLA_ROPE
    uq = w_uq.reshape(MLA_Q_LORA, MLA_HEADS, hq)
    pad = jnp.zeros((MLA_Q_LORA, MLA_HEADS, LANES - hq), w_uq.dtype)
    zq = jnp.zeros((MLA_Q_LORA, MLA_HEADS, MLA_NOPE), w_uq.dtype)
    wqa = jnp.concatenate([uq, pad], axis=2).reshape(MLA_Q_LORA, MLA_HEADS * LANES).astype(BF16)
    wqb = jnp.concatenate([zq, uq[:, :, MLA_NOPE + half:], uq[:, :, MLA_NOPE:MLA_NOPE + half], pad],
                          axis=2).reshape(MLA_Q_LORA, MLA_HEADS * LANES).astype(BF16)
    ukv = w_ukv.reshape(MLA_KV_LORA, MLA_HEADS, MLA_NOPE + MLA_V)
    zk = jnp.zeros((MLA_KV_LORA, MLA_HEADS, LANES - MLA_NOPE), w_ukv.dtype)
    wk = jnp.concatenate([ukv[:, :, :MLA_NOPE], zk], axis=2).reshape(MLA_KV_LORA, MLA_HEADS * LANES).astype(BF16)
    wv = ukv[:, :, MLA_NOPE:].reshape(MLA_KV_LORA, MLA_HEADS * MLA_V).astype(BF16)
    return w0, wqa, wqb, wk, wv


def _layer0_in(x2, mod3, g, w0, wqa, wqb, wk, wv, qg, kvg, ctab, stab, B):
    T, D = x2.shape
    nt = T // TM
    per_b = nt // B
    full = lambda a: pl.BlockSpec(a.shape, lambda i: (0,) * a.ndim)
    tok = lambda w: pl.BlockSpec((TM, w), lambda i: (i, 0))
    widths = (MLA_HEADS * LANES, MLA_HEADS * LANES, MLA_HEADS * MLA_V, SB_W, SB_W, SB_W, MLA_HEADS * MLA_V + SB_W)
    return pl.pallas_call(
        _in0_kernel,
        name="l0_in",
        grid=(nt,),
        in_specs=[
            tok(D),
            pl.BlockSpec((1, 1, D), lambda i: (0 * B + i // per_b, 0, 0)),
            pl.BlockSpec((1, 1, D), lambda i: (1 * B + i // per_b, 0, 0)),
            full(g), full(w0), full(wqa), full(wqb), full(wk), full(wv), full(qg), full(kvg),
            tok(LANES), tok(LANES),
        ],
        out_specs=[tok(w) for w in widths],
        out_shape=[jax.ShapeDtypeStruct((T, w), BF16) for w in widths],
        compiler_params=_params(1),
    )(x2, mod3, mod3, g, w0, wqa, wqb, wk, wv, qg, kvg, ctab, stab)


def _chunk_mask():
    r = lax.broadcasted_iota(jnp.int32, (TQ, TK), 0)
    c = lax.broadcasted_iota(jnp.int32, (TQ, TK), 1)
    shift = CHUNK.bit_length() - 1
    return (c >> shift) <= (r >> shift)


def _softmax_step(s, v, m_ref, l_ref, acc_ref, rows):
    m_prev = m_ref[rows, :]
    m_new = jnp.maximum(m_prev, jnp.max(s, axis=-1, keepdims=True))
    alpha = jnp.exp(m_prev - m_new)
    p = jnp.exp(s - _rep(m_new, s.shape[1] // LANES))
    l_ref[rows, :] = alpha * l_ref[rows, :] + jnp.sum(p, axis=-1, keepdims=True)
    acc_ref[rows, :] = alpha * acc_ref[rows, :] + _dot(p.astype(BF16), v)
    m_ref[rows, :] = m_new


def _mla_attn_kernel(q_ref, k_ref, v_ref, o_ref, m_ref, l_ref, acc_ref):
    S = q_ref.shape[0]
    mask = _chunk_mask()
    lane = lax.broadcasted_iota(jnp.int32, (TQ, LANES), 1)

    def q_body(qi, carry):
        q0 = pl.multiple_of(qi * TQ, TQ)
        m_ref[...] = jnp.full(m_ref.shape, -jnp.inf, F32)
        l_ref[...] = jnp.zeros(l_ref.shape, F32)
        acc_ref[...] = jnp.zeros(acc_ref.shape, F32)
        qs = [q_ref[pl.ds(q0, TQ), a * LANES:(a + 1) * LANES] for a in range(2)]

        def step(k0, masked):
            v = v_ref[pl.ds(k0, TK), :]
            for a in range(2):
                s = _dot_nt(qs[a], k_ref[pl.ds(k0, TK), a * LANES:(a + 1) * LANES])
                if masked:
                    s = jnp.where(mask, s, NEG_INF)
                _softmax_step(s, v, m_ref, l_ref, acc_ref, pl.ds(a * TQ, TQ))

        def kv_body(kj, c):
            step(pl.multiple_of(kj * TK, TK), False)
            return c

        lax.fori_loop(0, qi, kv_body, 0)
        step(q0, True)
        o = acc_ref[...] / l_ref[...]
        o_ref[pl.ds(q0, TQ), :] = jnp.where(lane < MLA_V, o[:TQ], o[TQ:]).astype(BF16)
        return carry

    lax.fori_loop(0, S // TQ, q_body, 0)


def _sb_attn_kernel(q_ref, k_ref, v_ref, o_ref, r_ref, acc_ref):
    S = q_ref.shape[0]
    r_i = lax.broadcasted_iota(jnp.int32, (TQ, TK), 0)
    c_i = lax.broadcasted_iota(jnp.int32, (TQ, TK), 1)
    causal = c_i < r_i
    tri = jnp.where(lax.broadcasted_iota(jnp.int32, (TK, TK), 0) > lax.broadcasted_iota(jnp.int32, (TK, TK), 1),
                    1.0, 0.0).astype(BF16)
    lane = lax.broadcasted_iota(jnp.int32, (TQ, LANES), 1)
    first = lane < SB_HEAD_DIM

    def q_body(qi, carry):
        q0 = pl.multiple_of(qi * TQ, TQ)
        r_ref[...] = jnp.zeros(r_ref.shape, F32)
        acc_ref[...] = jnp.zeros(acc_ref.shape, F32)
        q = q_ref[pl.ds(q0, TQ), :]
        zero = jnp.zeros_like(q)
        qs = [jnp.where(first, q, zero), jnp.where(first, zero, q)]

        def step(k0, masked):
            k = k_ref[pl.ds(k0, TK), :]
            v = v_ref[pl.ds(k0, TK), :]
            for a in range(2):
                rows = pl.ds(a * TQ, TQ)
                z = _dot_nt(qs[a], k)
                sp = jnp.log(1.0 + jnp.exp(-jnp.abs(z)))
                ls = jnp.minimum(z, 0.0) - sp
                l1m = ls - z
                if masked:
                    l1m = jnp.where(causal, l1m, 0.0)
                hi = l1m.astype(BF16)
                lo = (l1m - hi.astype(F32)).astype(BF16)
                suffix = _dot(hi, tri) + _dot(lo, tri)
                r_prev = r_ref[rows, :]
                w = jnp.exp(ls + suffix + _rep(r_prev, TK // LANES))
                if masked:
                    w = jnp.where(causal, w, 0.0)
                acc_ref[rows, :] += _dot(w.astype(BF16), v)
                r_ref[rows, :] = r_prev + jnp.sum(l1m, axis=-1, keepdims=True)

        step(q0, True)

        def kv_body(t, c):
            step(pl.multiple_of((qi - 1 - t) * TK, TK), False)
            return c

        lax.fori_loop(0, qi, kv_body, 0)
        acc = acc_ref[...]
        o_ref[pl.ds(q0, TQ), :] = jnp.where(first, acc[:TQ], acc[TQ:]).astype(BF16)
        return carry

    lax.fori_loop(0, S // TQ, q_body, 0)


def _diff_attn_kernel(lam_init, q_ref, k_ref, v_ref, bias_ref, lamp_ref, g_ref, o_ref, m_ref, l_ref, acc_ref):
    S = q_ref.shape[0]
    mask = _chunk_mask()
    lane = lax.broadcasted_iota(jnp.int32, (TQ, LANES), 1)
    first = lane < DIFF_HEAD_DIM
    lp = lamp_ref[...]
    lam = (jnp.exp(jnp.sum(lp[0:1] * lp[1:2], axis=-1, keepdims=True))
           - jnp.exp(jnp.sum(lp[2:3] * lp[3:4], axis=-1, keepdims=True)) + lam_init)

    def q_body(qi, carry):
        q0 = pl.multiple_of(qi * TQ, TQ)
        m_ref[...] = jnp.full(m_ref.shape, -jnp.inf, F32)
        l_ref[...] = jnp.zeros(l_ref.shape, F32)
        acc_ref[...] = jnp.zeros(acc_ref.shape, F32)
        q = q_ref[pl.ds(q0, TQ), :]
        zero = jnp.zeros_like(q)
        q2 = jnp.concatenate([jnp.where(first, q, zero), jnp.where(first, zero, q)], axis=0)

        def step(k0, bias, masked):
            s = _dot_nt(q2, k_ref[pl.ds(k0, TK), :])
            if bias is not None:
                s = s + jnp.concatenate([bias, bias], axis=0)
            if masked:
                s = jnp.where(jnp.concatenate([mask, mask], axis=0), s, NEG_INF)
            _softmax_step(s, v_ref[pl.ds(k0, TK), :], m_ref, l_ref, acc_ref, pl.ds(0, 2 * TQ))

        def kv_body(kj, c):
            step(pl.multiple_of(kj * TK, TK), None, False)
            return c

        lax.fori_loop(0, qi - 1, kv_body, 0)

        @pl.when(qi > 0)
        def _():
            step(pl.multiple_of((qi - 1) * TK, TK), bias_ref[0, :, :TK], False)

        step(q0, bias_ref[0, :, TK:], True)
        on = acc_ref[...] / l_ref[...]
        o = on[:TQ] - lam * on[TQ:]
        o = _rms(o, SUBLN_EPS) * g_ref[...] * (1.0 - lam_init)
        o_ref[pl.ds(q0, TQ), :] = o.astype(BF16)
        return carry

    lax.fori_loop(0, S // TQ, q_body, 0)


def _bias_kernel(idx_ref, tab_ref, o_ref):
    h = pl.program_id(0)
    idx = idx_ref[...]
    far = tab_ref[REL_BUCKETS // 2 - 1, h]
    acc = jnp.zeros(idx.shape, F32)
    for b in range(REL_BUCKETS):
        acc = jnp.where(idx == b, tab_ref[b, h] - far, acc)
    o_ref[0] = acc


def _t5_bucket_np(rel):
    nb = REL_BUCKETS // 2
    max_exact = nb // 2
    ret = np.where(rel > 0, nb, 0)
    n = np.abs(rel)
    nf = np.maximum(n, 1).astype(np.float32)
    large = max_exact + (np.log(nf / np.float32(max_exact)) / np.float32(math.log(REL_MAX_DIST / max_exact))
                         * np.float32(nb - max_exact)).astype(np.int32)
    large = np.minimum(large, nb - 1)
    return (ret + np.where(n < max_exact, n, large)).astype(np.int32)


def _bias_tiles(rel_table):
    assert TK >= REL_MAX_DIST
    rel = (np.arange(2 * TK)[None, :] - TK) - np.arange(TQ)[:, None]
    idx = jnp.asarray(_t5_bucket_np(rel))
    return pl.pallas_call(
        _bias_kernel,
        name="rel_bias",
        grid=(DIFF_HEADS,),
        in_specs=[pl.BlockSpec((TQ, 2 * TK), lambda h: (0, 0)),
                  pl.BlockSpec(memory_space=pltpu.SMEM)],
        out_specs=pl.BlockSpec((1, TQ, 2 * TK), lambda h: (h, 0, 0)),
        out_shape=jax.ShapeDtypeStruct((DIFF_HEADS, TQ, 2 * TK), F32),
        compiler_params=_params(1),
    )(idx, rel_table.astype(F32))


def _mla_attention(qm, km, vm, B, S):
    T = qm.shape[0]
    npair = MLA_HEADS // 2
    return pl.pallas_call(
        _mla_attn_kernel,
        name="mla_attn",
        grid=(B, npair),
        in_specs=[pl.BlockSpec((S, 2 * LANES), lambda b, h: (b, h)),
                  pl.BlockSpec((S, 2 * LANES), lambda b, h: (b, h)),
                  pl.BlockSpec((S, LANES), lambda b, h: (b, h))],
        out_specs=pl.BlockSpec((S, LANES), lambda b, h: (b, h)),
        out_shape=jax.ShapeDtypeStruct((T, MLA_HEADS * MLA_V), BF16),
        scratch_shapes=[pltpu.VMEM((2 * TQ, LANES), F32)] * 3,
        compiler_params=_params(2),
    )(qm, km, vm)


def _sb_attention(qs, ks, vs, B, S):
    T = qs.shape[0]
    npair = SB_HEADS // 2
    spec = pl.BlockSpec((S, LANES), lambda b, h: (b, h))
    return pl.pallas_call(
        _sb_attn_kernel,
        name="sb_attn",
        grid=(B, npair),
        in_specs=[spec, spec, spec],
        out_specs=spec,
        out_shape=jax.ShapeDtypeStruct((T, SB_W), BF16),
        scratch_shapes=[pltpu.VMEM((2 * TQ, LANES), F32)] * 2,
        compiler_params=_params(2),
    )(qs, ks, vs)


def _diff_attention(qd, kd, vd, bias, lamp, g, lam_init, B, S):
    T = qd.shape[0]
    spec = pl.BlockSpec((S, LANES), lambda b, h: (b, h))
    return pl.pallas_call(
        functools.partial(_diff_attn_kernel, lam_init),
        name="diff_attn",
        grid=(B, DIFF_HEADS),
        in_specs=[spec, spec, spec,
                  pl.BlockSpec((1, TQ, 2 * TK), lambda b, h: (h, 0, 0)),
                  pl.BlockSpec(lamp.shape, lambda b, h: (0, 0)),
                  pl.BlockSpec(g.shape, lambda b, h: (0, 0))],
        out_specs=spec,
        out_shape=jax.ShapeDtypeStruct((T, DIFF_W), BF16),
        scratch_shapes=[pltpu.VMEM((2 * TQ, LANES), F32)] * 3,
        compiler_params=_params(2),
    )(qd, kd, vd, bias, lamp, g)


def _out0_in1_kernel(om_ref, os_ref, z_ref, x_ref, gate_ref, wo_ref, shift_ref, scale_ref, g_ref, w1_ref,
                     x1_ref, qd_ref, kd_ref, vd_ref, zd_ref):
    half = om_ref.shape[1]
    z = z_ref[...].astype(F32)
    y = jnp.concatenate([om_ref[...].astype(F32), os_ref[...].astype(F32)], axis=1) * _silu(z)
    out = _dot(y.astype(BF16), wo_ref[...])
    x1 = x_ref[...] + gate_ref[0] * out
    x1_ref[...] = x1
    h = (_rms(x1, NORM_EPS) * g_ref[...] * (1.0 + scale_ref[0]) + shift_ref[0]).astype(BF16)
    for i, ref in enumerate((qd_ref, kd_ref, vd_ref, zd_ref)):
        ref[...] = _dot(h, w1_ref[:, i * DIFF_W:(i + 1) * DIFF_W]).astype(BF16)


def _layer0_out_layer1_in(om, osb, z, x2, mod3, wo, g1, w1, B):
    T, D = x2.shape
    nt = T // TM
    per_b = nt // B
    full = lambda a: pl.BlockSpec(a.shape, lambda i: (0,) * a.ndim)
    tok = lambda w: pl.BlockSpec((TM, w), lambda i: (i, 0))
    mod = lambda row: pl.BlockSpec((1, 1, D), lambda i: (row * B + i // per_b, 0, 0))
    return pl.pallas_call(
        _out0_in1_kernel,
        name="l0_out_l1_in",
        grid=(nt,),
        in_specs=[tok(om.shape[1]), tok(osb.shape[1]), tok(D), tok(D), mod(2), full(wo),
                  mod(3), mod(4), full(g1), full(w1)],
        out_specs=[tok(D)] * 5,
        out_shape=[jax.ShapeDtypeStruct((T, D), F32)] + [jax.ShapeDtypeStruct((T, DIFF_W), BF16)] * 4,
        compiler_params=_params(1),
    )(om, osb, z, x2, mod3, wo, mod3, mod3, g1, w1)


def _out1_kernel(o_ref, z_ref, x_ref, gate_ref, wo_ref, g_ref, y_ref):
    y = o_ref[...].astype(F32) * _silu(z_ref[...].astype(F32))
    out = _dot(y.astype(BF16), wo_ref[...])
    x2 = x_ref[...] + gate_ref[0] * out
    y_ref[...] = _rms(x2, NORM_EPS) * g_ref[...]


def _layer1_out(od, zd, x1, mod3, wo, gf, B):
    T, D = x1.shape
    nt = T // TM
    per_b = nt // B
    full = lambda a: pl.BlockSpec(a.shape, lambda i: (0,) * a.ndim)
    tok = lambda w: pl.BlockSpec((TM, w), lambda i: (i, 0))
    return pl.pallas_call(
        _out1_kernel,
        name="l1_out",
        grid=(nt,),
        in_specs=[tok(D), tok(D), tok(D), pl.BlockSpec((1, 1, D), lambda i: (5 * B + i // per_b, 0, 0)),
                  full(wo), full(gf)],
        out_specs=tok(D),
        out_shape=jax.ShapeDtypeStruct((T, D), F32),
        compiler_params=_params(1),
    )(od, zd, x1, mod3, wo, gf)


def kernel(x, c, pos_offset, rel_bias_table, ada_w, ada_b, norm_g, final_g, ab_w_in, ab_q_norm_g, ab_kv_norm_g,
           ab_w_uq, ab_w_ukv, ab_w_out, dif_w_in, dif_lam_q1, dif_lam_k1, dif_lam_q2, dif_lam_k2, dif_subln_g,
           dif_w_out):
    B, S, D = x.shape
    assert D == D_MODEL and S % TQ == 0 and TQ == TK and (B * S) % TM == 0 and S % TM == 0
    x2 = x.reshape(B * S, D)

    mod3 = _modulation(c, ada_w, ada_b)
    ctab, stab = _rope_tables(pos_offset, S)

    w0, wqa, wqb, wk, wv = _prep_ab_weights(ab_w_in[0], ab_w_uq[0], ab_w_ukv[0])
    qm, km, vm, qs, ks, vs, z = _layer0_in(
        x2, mod3, norm_g[0:1], w0, wqa, wqb, wk, wv, ab_q_norm_g[0:1], ab_kv_norm_g[0:1], ctab, stab, B)
    om = _mla_attention(qm, km, vm, B, S)
    osb = _sb_attention(qs, ks, vs, B, S)

    w1 = jnp.concatenate([dif_w_in[0][:, :DIFF_W] * DIFF_HEAD_DIM ** -0.5, dif_w_in[0][:, DIFF_W:]], axis=1).astype(BF16)
    x1, qd, kd, vd, zd = _layer0_out_layer1_in(om, osb, z, x2, mod3, ab_w_out[0].astype(BF16), norm_g[1:2], w1, B)
    lam_init = 0.8 - 0.6 * math.exp(-0.3 * 1)
    lamp = jnp.stack([dif_lam_q1[0], dif_lam_k1[0], dif_lam_q2[0], dif_lam_k2[0]]).astype(F32)
    bias = _bias_tiles(rel_bias_table)
    od = _diff_attention(qd, kd, vd, bias, lamp, dif_subln_g[0:1], lam_init, B, S)
    y = _layer1_out(od, zd, x1, mod3, dif_w_out[0].astype(BF16), final_g[None, :], B)
    return y.reshape(B, S, D)
```

```python
import functools
import math

import numpy as np
import jax
import jax.numpy as jnp
from jax import lax
from jax.experimental import pallas as pl
from jax.experimental.pallas import tpu as pltpu

D_MODEL = 1024
DEPTH = 2
CHUNK = 64

MLA_HEADS = 8
MLA_Q_LORA = 384
MLA_KV_LORA = 256
MLA_NOPE = 64
MLA_ROPE = 32
MLA_V = 64
SB_HEADS = 8
SB_HEAD_DIM = 64
SB_W = SB_HEADS * SB_HEAD_DIM
DIFF_HEADS = 8
DIFF_HEAD_DIM = 64
DIFF_W = DIFF_HEADS * 2 * DIFF_HEAD_DIM

REL_BUCKETS = 32
REL_MAX_DIST = 128
ROPE_THETA = 10000.0
NORM_EPS = 1e-6
SUBLN_EPS = 1e-5
NEG_INF = -1e30
LOG2E = math.log2(math.e)

LANES = 128
V7X_VMEM_BYTES = 64 * 1024 * 1024
VMEM_LIMIT = V7X_VMEM_BYTES * 7 // 8

TM = 512
TQ = 256
TK = 256

F32 = jnp.float32
BF16 = jnp.bfloat16


def _silu(z):
    return z * (1.0 / (1.0 + jnp.exp(-z)))


def _dot(a, b):
    return jnp.dot(a, b, preferred_element_type=F32)


def _dot_nt(a, b):
    return lax.dot_general(a, b, (((1,), (1,)), ((), ())), preferred_element_type=F32)


def _rep(x, n):
    return x if n == 1 else jnp.concatenate([x] * n, axis=1)


def _params(n_axes=1):
    return pltpu.CompilerParams(dimension_semantics=("arbitrary",) * n_axes, vmem_limit_bytes=VMEM_LIMIT)


def _mod_kernel(c_ref, w_ref, b_ref, o_ref):
    ca = _silu(c_ref[...]).astype(BF16)
    o_ref[0] = _dot(ca, w_ref[0].astype(BF16)) + b_ref[0]


def _modulation(c, ada_w, ada_b):
    B, D = c.shape
    out = pl.pallas_call(
        _mod_kernel,
        name="ada_mod",
        grid=(DEPTH, 3),
        in_specs=[
            pl.BlockSpec((B, D), lambda i, j: (0, 0)),
            pl.BlockSpec((1, D, D), lambda i, j: (i, 0, j)),
            pl.BlockSpec((1, 1, D), lambda i, j: (i * 3 + j, 0, 0)),
        ],
        out_specs=pl.BlockSpec((1, B, D), lambda i, j: (i * 3 + j, 0, 0)),
        out_shape=jax.ShapeDtypeStruct((DEPTH * 3, B, D), F32),
        compiler_params=_params(2),
    )(c, ada_w, ada_b.reshape(DEPTH * 3, 1, D))
    return out.reshape(DEPTH * 3 * B, 1, D)


def _rope_kernel(pos_ref, f_ref, cos_ref, sin_ref):
    ang = pos_ref[...].astype(F32) * f_ref[...]
    cos_ref[...] = jnp.cos(ang)
    sin_ref[...] = jnp.sin(ang)


def _rope_tables(pos_offset, S):
    B = pos_offset.shape[0]
    half = MLA_ROPE // 2
    per_row = LANES // half
    inv_freq = ROPE_THETA ** (-jnp.arange(half, dtype=F32) / half)
    pos = pos_offset[:, None] + jnp.arange(S, dtype=jnp.int32)[None, :]
    pos = jnp.repeat(pos.reshape(B * S // per_row, per_row), half, axis=1)
    f = jnp.tile(inv_freq, per_row)[None, :]
    rows = B * S // per_row
    cos, sin = pl.pallas_call(
        _rope_kernel,
        name="rope_tab",
        out_shape=[jax.ShapeDtypeStruct((rows, LANES), F32)] * 2,
    )(pos, f)
    cos = cos.reshape(B * S, half)
    sin = sin.reshape(B * S, half)
    one = jnp.ones((B * S, MLA_NOPE), F32)
    zn = jnp.zeros((B * S, MLA_NOPE), F32)
    zp = jnp.zeros((B * S, LANES - MLA_NOPE - MLA_ROPE), F32)
    ctab = jnp.concatenate([one, cos, cos, zp], axis=1)
    stab = jnp.concatenate([zn, -sin, sin, zp], axis=1)
    return ctab, stab


AB_SEG = (MLA_Q_LORA, MLA_KV_LORA, SB_W, SB_W, SB_W, MLA_HEADS * MLA_V + SB_W, LANES, LANES)
AB_OFF = tuple(int(v) for v in np.cumsum((0,) + AB_SEG))


def _rms(x, eps):
    return x * lax.rsqrt(jnp.mean(x * x, axis=-1, keepdims=True) + eps)


def _in0_kernel(x_ref, shift_ref, scale_ref, g_ref, w_ref, wqa_ref, wqb_ref, wk_ref, wv_ref,
                qg_ref, kvg_ref, c_ref, s_ref,
                qm_ref, km_ref, vm_ref, qs_ref, ks_ref, vs_ref, z_ref):
    x = x_ref[...]
    h = _rms(x, NORM_EPS) * g_ref[...] * (1.0 + scale_ref[0]) + shift_ref[0]
    h = h.astype(BF16)

    def seg(i):
        return _dot(h, w_ref[:, AB_OFF[i]:AB_OFF[i + 1]])

    ctab = c_ref[...]
    stab = s_ref[...]
    nh = MLA_HEADS

    cq = (_rms(seg(0), NORM_EPS) * qg_ref[...]).astype(BF16)
    q = _dot(cq, wqa_ref[...]) * _rep(ctab, nh) + _dot(cq, wqb_ref[...]) * _rep(stab, nh)
    qm_ref[...] = (q * ((MLA_NOPE + MLA_ROPE) ** -0.5 * LOG2E)).astype(BF16)

    ckv = (_rms(seg(1), NORM_EPS) * kvg_ref[...]).astype(BF16)
    krope = seg(6) * ctab + seg(7) * stab
    km_ref[...] = (_dot(ckv, wk_ref[...]) + _rep(krope, nh)).astype(BF16)
    vm_ref[...] = _dot(ckv, wv_ref[...]).astype(BF16)

    qs_ref[...] = seg(2).astype(BF16)
    ks_ref[...] = seg(3).astype(BF16)
    vs_ref[...] = seg(4).astype(BF16)
    z_ref[...] = seg(5).astype(BF16)


def _prep_ab_weights(w_in, w_uq, w_ukv):
    D = w_in.shape[0]
    c = np.cumsum([MLA_Q_LORA, MLA_KV_LORA, MLA_ROPE, SB_W, SB_W, SB_W]).tolist()
    half = MLA_ROPE // 2
    w_kr = w_in[:, c[1]:c[2]]
    zl = jnp.zeros((D, MLA_NOPE), w_in.dtype)
    zr = jnp.zeros((D, LANES - MLA_NOPE - MLA_ROPE), w_in.dtype)
    kr_a = jnp.concatenate([zl, w_kr, zr], axis=1)
    kr_b = jnp.concatenate([zl, w_kr[:, half:], w_kr[:, :half], zr], axis=1)
    w0 = jnp.concatenate([
        w_in[:, :c[1]],
        w_in[:, c[2]:c[3]] * SB_HEAD_DIM ** -0.5,
        w_in[:, c[3]:],
        kr_a, kr_b], axis=1).astype(BF16)

    hq = MLA_NOPE + MLA_ROPE
    uq = w_uq.reshape(MLA_Q_LORA, MLA_HEADS, hq)
    pad = jnp.zeros((MLA_Q_LORA, MLA_HEADS, LANES - hq), w_uq.dtype)
    zq = jnp.zeros((MLA_Q_LORA, MLA_HEADS, MLA_NOPE), w_uq.dtype)
    wqa = jnp.concatenate([uq, pad], axis=2).reshape(MLA_Q_LORA, MLA_HEADS * LANES).astype(BF16)
    wqb = jnp.concatenate([zq, uq[:, :, MLA_NOPE + half:], uq[:, :, MLA_NOPE:MLA_NOPE + half], pad],
                          axis=2).reshape(MLA_Q_LORA, MLA_HEADS * LANES).astype(BF16)
    ukv = w_ukv.reshape(MLA_KV_LORA, MLA_HEADS, MLA_NOPE + MLA_V)
    zk = jnp.zeros((MLA_KV_LORA, MLA_HEADS, LANES - MLA_NOPE), w_ukv.dtype)
    wk = jnp.concatenate([ukv[:, :, :MLA_NOPE], zk], axis=2).reshape(MLA_KV_LORA, MLA_HEADS * LANES).astype(BF16)
    wv = ukv[:, :, MLA_NOPE:].reshape(MLA_KV_LORA, MLA_HEADS * MLA_V).astype(BF16)
    return w0, wqa, wqb, wk, wv


def _layer0_in(x2, mod3, g, w0, wqa, wqb, wk, wv, qg, kvg, ctab, stab, B):
    T, D = x2.shape
    nt = T // TM
    per_b = nt // B
    full = lambda a: pl.BlockSpec(a.shape, lambda i: (0,) * a.ndim)
    tok = lambda w: pl.BlockSpec((TM, w), lambda i: (i, 0))
    widths = (MLA_HEADS * LANES, MLA_HEADS * LANES, MLA_HEADS * MLA_V, SB_W, SB_W, SB_W, MLA_HEADS * MLA_V + SB_W)
    return pl.pallas_call(
        _in0_kernel,
        name="l0_in",
        grid=(nt,),
        in_specs=[
            tok(D),
            pl.BlockSpec((1, 1, D), lambda i: (0 * B + i // per_b, 0, 0)),
            pl.BlockSpec((1, 1, D), lambda i: (1 * B + i // per_b, 0, 0)),
            full(g), full(w0), full(wqa), full(wqb), full(wk), full(wv), full(qg), full(kvg),
            tok(LANES), tok(LANES),
        ],
        out_specs=[tok(w) for w in widths],
        out_shape=[jax.ShapeDtypeStruct((T, w), BF16) for w in widths],
        compiler_params=_params(1),
    )(x2, mod3, mod3, g, w0, wqa, wqb, wk, wv, qg, kvg, ctab, stab)


def _chunk_mask():
    r = lax.broadcasted_iota(jnp.int32, (TQ, TK), 0)
    c = lax.broadcasted_iota(jnp.int32, (TQ, TK), 1)
    shift = CHUNK.bit_length() - 1
    return (c >> shift) <= (r >> shift)


def _lane_fold(p):
    out = p[:, :LANES]
    for j in range(1, p.shape[1] // LANES):
        out = out + p[:, j * LANES:(j + 1) * LANES]
    return out


def _softmax_step(s, v, m_ref, l_ref, acc_ref, rows):
    m_prev = m_ref[rows, :]
    m_new = jnp.maximum(m_prev, jnp.max(s, axis=-1, keepdims=True))
    alpha = jnp.exp2(m_prev - m_new)
    p = jnp.exp2(s - _rep(m_new, s.shape[1] // LANES))
    l_ref[rows, :] = alpha * l_ref[rows, :] + _lane_fold(p)
    acc_ref[rows, :] = alpha * acc_ref[rows, :] + _dot(p.astype(BF16), v)
    m_ref[rows, :] = m_new


def _softmax_init(m_ref, l_ref, acc_ref):
    m_ref[...] = jnp.full(m_ref.shape, -jnp.inf, F32)
    l_ref[...] = jnp.zeros(l_ref.shape, F32)
    acc_ref[...] = jnp.zeros(acc_ref.shape, F32)


def _softmax_out(l_ref, acc_ref, rows):
    return acc_ref[rows, :] / jnp.sum(l_ref[rows, :], axis=-1, keepdims=True)


def _mla_attn_kernel(q_ref, k_ref, v_ref, o_ref, m_ref, l_ref, acc_ref):
    S = q_ref.shape[0]
    mask = _chunk_mask()
    lane = lax.broadcasted_iota(jnp.int32, (TQ, LANES), 1)

    def q_body(qi, carry):
        q0 = pl.multiple_of(qi * TQ, TQ)
        _softmax_init(m_ref, l_ref, acc_ref)

        def step(k0, masked):
            for h in range(MLA_HEADS):
                hl = slice(h * LANES, (h + 1) * LANES)
                pair = slice((h // 2) * LANES, (h // 2 + 1) * LANES)
                s = _dot_nt(q_ref[pl.ds(q0, TQ), hl], k_ref[pl.ds(k0, TK), hl])
                if masked:
                    s = jnp.where(mask, s, NEG_INF)
                _softmax_step(s, v_ref[pl.ds(k0, TK), pair], m_ref, l_ref, acc_ref, pl.ds(h * TQ, TQ))

        def kv_body(kj, c):
            step(pl.multiple_of(kj * TK, TK), False)
            return c

        lax.fori_loop(0, qi, kv_body, 0)
        step(q0, True)
        for j in range(MLA_HEADS // 2):
            oa = _softmax_out(l_ref, acc_ref, pl.ds((2 * j) * TQ, TQ))
            ob = _softmax_out(l_ref, acc_ref, pl.ds((2 * j + 1) * TQ, TQ))
            o_ref[pl.ds(q0, TQ), j * LANES:(j + 1) * LANES] = jnp.where(lane < MLA_V, oa, ob).astype(BF16)
        return carry

    lax.fori_loop(0, S // TQ, q_body, 0)


def _sb_attn_kernel(q_ref, k_ref, v_ref, o_ref, r_ref, acc_ref):
    S = q_ref.shape[0]
    r_i = lax.broadcasted_iota(jnp.int32, (TQ, TK), 0)
    c_i = lax.broadcasted_iota(jnp.int32, (TQ, TK), 1)
    causal = c_i < r_i
    tri = jnp.where(lax.broadcasted_iota(jnp.int32, (TK, TK), 0) > lax.broadcasted_iota(jnp.int32, (TK, TK), 1),
                    1.0, 0.0).astype(BF16)
    lane = lax.broadcasted_iota(jnp.int32, (TQ, LANES), 1)
    first = lane < SB_HEAD_DIM

    def q_body(qi, carry):
        q0 = pl.multiple_of(qi * TQ, TQ)
        r_ref[...] = jnp.zeros(r_ref.shape, F32)
        acc_ref[...] = jnp.zeros(acc_ref.shape, F32)

        def step(k0, masked):
            for h in range(SB_HEADS):
                pair = slice((h // 2) * LANES, (h // 2 + 1) * LANES)
                rows = pl.ds(h * TQ, TQ)
                q = q_ref[pl.ds(q0, TQ), pair]
                q = jnp.where(first, q, jnp.zeros_like(q)) if h % 2 == 0 else jnp.where(first, jnp.zeros_like(q), q)
                z = _dot_nt(q, k_ref[pl.ds(k0, TK), pair])
                sp = jnp.log(1.0 + jnp.exp(-jnp.abs(z)))
                ls = jnp.minimum(z, 0.0) - sp
                l1m = ls - z
                if masked:
                    l1m = jnp.where(causal, l1m, 0.0)
                hi = l1m.astype(BF16)
                lo = (l1m - hi.astype(F32)).astype(BF16)
                suffix = _dot(hi, tri) + _dot(lo, tri)
                r_prev = r_ref[rows, :]
                w = jnp.exp(ls + suffix + _rep(r_prev, TK // LANES))
                if masked:
                    w = jnp.where(causal, w, 0.0)
                acc_ref[rows, :] += _dot(w.astype(BF16), v_ref[pl.ds(k0, TK), pair])
                r_ref[rows, :] = r_prev + jnp.sum(l1m, axis=-1, keepdims=True)

        step(q0, True)

        def kv_body(t, c):
            step(pl.multiple_of((qi - 1 - t) * TK, TK), False)
            return c

        lax.fori_loop(0, qi, kv_body, 0)
        for j in range(SB_HEADS // 2):
            oa = acc_ref[pl.ds((2 * j) * TQ, TQ), :]
            ob = acc_ref[pl.ds((2 * j + 1) * TQ, TQ), :]
            o_ref[pl.ds(q0, TQ), j * LANES:(j + 1) * LANES] = jnp.where(first, oa, ob).astype(BF16)
        return carry

    lax.fori_loop(0, S // TQ, q_body, 0)


def _diff_attn_kernel(lam_init, q_ref, k_ref, v_ref, bias_ref, lamp_ref, g_ref, o_ref, m_ref, l_ref, acc_ref):
    S = q_ref.shape[0]
    mask = _chunk_mask()
    mask2 = jnp.concatenate([mask, mask], axis=0)
    lane = lax.broadcasted_iota(jnp.int32, (TQ, LANES), 1)
    first = lane < DIFF_HEAD_DIM
    lp = lamp_ref[...]
    lam = (jnp.exp(jnp.sum(lp[0:1] * lp[1:2], axis=-1, keepdims=True))
           - jnp.exp(jnp.sum(lp[2:3] * lp[3:4], axis=-1, keepdims=True)) + lam_init)

    def q_body(qi, carry):
        q0 = pl.multiple_of(qi * TQ, TQ)
        _softmax_init(m_ref, l_ref, acc_ref)

        def step(k0, near, masked):
            for h in range(DIFF_HEADS):
                hl = slice(h * LANES, (h + 1) * LANES)
                q = q_ref[pl.ds(q0, TQ), hl]
                zero = jnp.zeros_like(q)
                q2 = jnp.concatenate([jnp.where(first, q, zero), jnp.where(first, zero, q)], axis=0)
                s = _dot_nt(q2, k_ref[pl.ds(k0, TK), hl])
                if near is not None:
                    bias = bias_ref[h, :, near * TK:(near + 1) * TK]
                    s = s + jnp.concatenate([bias, bias], axis=0)
                if masked:
                    s = jnp.where(mask2, s, NEG_INF)
                _softmax_step(s, v_ref[pl.ds(k0, TK), hl], m_ref, l_ref, acc_ref, pl.ds(h * 2 * TQ, 2 * TQ))

        def kv_body(kj, c):
            step(pl.multiple_of(kj * TK, TK), None, False)
            return c

        lax.fori_loop(0, qi - 1, kv_body, 0)

        @pl.when(qi > 0)
        def _():
            step(pl.multiple_of((qi - 1) * TK, TK), 0, False)

        step(q0, 1, True)
        for h in range(DIFF_HEADS):
            o = (_softmax_out(l_ref, acc_ref, pl.ds(h * 2 * TQ, TQ))
                 - lam * _softmax_out(l_ref, acc_ref, pl.ds(h * 2 * TQ + TQ, TQ)))
            o = _rms(o, SUBLN_EPS) * g_ref[...] * (1.0 - lam_init)
            o_ref[pl.ds(q0, TQ), h * LANES:(h + 1) * LANES] = o.astype(BF16)
        return carry

    lax.fori_loop(0, S // TQ, q_body, 0)


def _bias_kernel(idx_ref, tab_ref, o_ref):
    h = pl.program_id(0)
    idx = idx_ref[...]
    far = tab_ref[REL_BUCKETS // 2 - 1, h]
    acc = jnp.zeros(idx.shape, F32)
    for b in range(REL_BUCKETS):
        acc = jnp.where(idx == b, (tab_ref[b, h] - far) * LOG2E, acc)
    o_ref[0] = acc


def _t5_bucket_np(rel):
    nb = REL_BUCKETS // 2
    max_exact = nb // 2
    ret = np.where(rel > 0, nb, 0)
    n = np.abs(rel)
    nf = np.maximum(n, 1).astype(np.float32)
    large = max_exact + (np.log(nf / np.float32(max_exact)) / np.float32(math.log(REL_MAX_DIST / max_exact))
                         * np.float32(nb - max_exact)).astype(np.int32)
    large = np.minimum(large, nb - 1)
    return (ret + np.where(n < max_exact, n, large)).astype(np.int32)


def _bias_tiles(rel_table):
    assert TK >= REL_MAX_DIST
    rel = (np.arange(2 * TK)[None, :] - TK) - np.arange(TQ)[:, None]
    idx = jnp.asarray(_t5_bucket_np(rel))
    return pl.pallas_call(
        _bias_kernel,
        name="rel_bias",
        grid=(DIFF_HEADS,),
        in_specs=[pl.BlockSpec((TQ, 2 * TK), lambda h: (0, 0)),
                  pl.BlockSpec(memory_space=pltpu.SMEM)],
        out_specs=pl.BlockSpec((1, TQ, 2 * TK), lambda h: (h, 0, 0)),
        out_shape=jax.ShapeDtypeStruct((DIFF_HEADS, TQ, 2 * TK), F32),
        compiler_params=_params(1),
    )(idx, rel_table.astype(F32))


def _attention_call(kernel_fn, name, ins, extra, out_width, scratch_rows, n_scratch, B, S):
    T = ins[0].shape[0]
    seq = lambda a: pl.BlockSpec((S, a.shape[1]), lambda b: (b, 0))
    const = lambda a: pl.BlockSpec(a.shape, lambda b: (0,) * a.ndim)
    return pl.pallas_call(
        kernel_fn,
        name=name,
        grid=(B,),
        in_specs=[seq(a) for a in ins] + [const(a) for a in extra],
        out_specs=pl.BlockSpec((S, out_width), lambda b: (b, 0)),
        out_shape=jax.ShapeDtypeStruct((T, out_width), BF16),
        scratch_shapes=[pltpu.VMEM((scratch_rows, LANES), F32)] * n_scratch,
        compiler_params=_params(1),
    )(*ins, *extra)


def _mla_attention(qm, km, vm, B, S):
    return _attention_call(_mla_attn_kernel, "mla_attn", (qm, km, vm), (), MLA_HEADS * MLA_V,
                           MLA_HEADS * TQ, 3, B, S)


def _sb_attention(qs, ks, vs, B, S):
    return _attention_call(_sb_attn_kernel, "sb_attn", (qs, ks, vs), (), SB_W, SB_HEADS * TQ, 2, B, S)


def _diff_attention(qd, kd, vd, bias, lamp, g, lam_init, B, S):
    return _attention_call(functools.partial(_diff_attn_kernel, lam_init), "diff_attn", (qd, kd, vd),
                           (bias, lamp, g), DIFF_W, DIFF_HEADS * 2 * TQ, 3, B, S)


def _out0_in1_kernel(om_ref, os_ref, z_ref, x_ref, gate_ref, wo_ref, shift_ref, scale_ref, g_ref, w1_ref,
                     x1_ref, qd_ref, kd_ref, vd_ref, zd_ref):
    z = z_ref[...].astype(F32)
    y = jnp.concatenate([om_ref[...].astype(F32), os_ref[...].astype(F32)], axis=1) * _silu(z)
    out = _dot(y.astype(BF16), wo_ref[...])
    x1 = x_ref[...] + gate_ref[0] * out
    x1_ref[...] = x1
    h = (_rms(x1, NORM_EPS) * g_ref[...] * (1.0 + scale_ref[0]) + shift_ref[0]).astype(BF16)
    scales = (DIFF_HEAD_DIM ** -0.5 * LOG2E, None, None, None)
    for i, ref in enumerate((qd_ref, kd_ref, vd_ref, zd_ref)):
        r = _dot(h, w1_ref[:, i * DIFF_W:(i + 1) * DIFF_W])
        ref[...] = (r if scales[i] is None else r * scales[i]).astype(BF16)


def _layer0_out_layer1_in(om, osb, z, x2, mod3, wo, g1, w1, B):
    T, D = x2.shape
    nt = T // TM
    per_b = nt // B
    full = lambda a: pl.BlockSpec(a.shape, lambda i: (0,) * a.ndim)
    tok = lambda w: pl.BlockSpec((TM, w), lambda i: (i, 0))
    mod = lambda row: pl.BlockSpec((1, 1, D), lambda i: (row * B + i // per_b, 0, 0))
    return pl.pallas_call(
        _out0_in1_kernel,
        name="l0_out_l1_in",
        grid=(nt,),
        in_specs=[tok(om.shape[1]), tok(osb.shape[1]), tok(D), tok(D), mod(2), full(wo),
                  mod(3), mod(4), full(g1), full(w1)],
        out_specs=[tok(D)] * 5,
        out_shape=[jax.ShapeDtypeStruct((T, D), F32)] + [jax.ShapeDtypeStruct((T, DIFF_W), BF16)] * 4,
        compiler_params=_params(1),
    )(om, osb, z, x2, mod3, wo, mod3, mod3, g1, w1)


def _out1_kernel(o_ref, z_ref, x_ref, gate_ref, wo_ref, g_ref, y_ref):
    y = o_ref[...].astype(F32) * _silu(z_ref[...].astype(F32))
    out = _dot(y.astype(BF16), wo_ref[...])
    x2 = x_ref[...] + gate_ref[0] * out
    y_ref[...] = _rms(x2, NORM_EPS) * g_ref[...]


def _layer1_out(od, zd, x1, mod3, wo, gf, B):
    T, D = x1.shape
    nt = T // TM
    per_b = nt // B
    full = lambda a: pl.BlockSpec(a.shape, lambda i: (0,) * a.ndim)
    tok = lambda w: pl.BlockSpec((TM, w), lambda i: (i, 0))
    return pl.pallas_call(
        _out1_kernel,
        name="l1_out",
        grid=(nt,),
        in_specs=[tok(D), tok(D), tok(D), pl.BlockSpec((1, 1, D), lambda i: (5 * B + i // per_b, 0, 0)),
                  full(wo), full(gf)],
        out_specs=tok(D),
        out_shape=jax.ShapeDtypeStruct((T, D), F32),
        compiler_params=_params(1),
    )(od, zd, x1, mod3, wo, gf)


def kernel(x, c, pos_offset, rel_bias_table, ada_w, ada_b, norm_g, final_g, ab_w_in, ab_q_norm_g, ab_kv_norm_g,
           ab_w_uq, ab_w_ukv, ab_w_out, dif_w_in, dif_lam_q1, dif_lam_k1, dif_lam_q2, dif_lam_k2, dif_subln_g,
           dif_w_out):
    B, S, D = x.shape
    assert D == D_MODEL and S % TQ == 0 and TQ == TK and (B * S) % TM == 0 and S % TM == 0
    x2 = x.reshape(B * S, D)

    mod3 = _modulation(c, ada_w, ada_b)
    ctab, stab = _rope_tables(pos_offset, S)

    w0, wqa, wqb, wk, wv = _prep_ab_weights(ab_w_in[0], ab_w_uq[0], ab_w_ukv[0])
    qm, km, vm, qs, ks, vs, z = _layer0_in(
        x2, mod3, norm_g[0:1], w0, wqa, wqb, wk, wv, ab_q_norm_g[0:1], ab_kv_norm_g[0:1], ctab, stab, B)
    om = _mla_attention(qm, km, vm, B, S)
    osb = _sb_attention(qs, ks, vs, B, S)

    x1, qd, kd, vd, zd = _layer0_out_layer1_in(
        om, osb, z, x2, mod3, ab_w_out[0].astype(BF16), norm_g[1:2], dif_w_in[0].astype(BF16), B)
    lam_init = 0.8 - 0.6 * math.exp(-0.3 * 1)
    lamp = jnp.stack([dif_lam_q1[0], dif_lam_k1[0], dif_lam_q2[0], dif_lam_k2[0]]).astype(F32)
    bias = _bias_tiles(rel_bias_table)
    od = _diff_attention(qd, kd, vd, bias, lamp, dif_subln_g[0:1], lam_init, B, S)
    y = _layer1_out(od, zd, x1, mod3, dif_w_out[0].astype(BF16), final_g[None, :], B)
    return y.reshape(B, S, D)
```

```python
import functools
import math

import numpy as np
import jax
import jax.numpy as jnp
from jax import lax
from jax.experimental import pallas as pl
from jax.experimental.pallas import tpu as pltpu

D_MODEL = 1024
DEPTH = 2
CHUNK = 64

MLA_HEADS = 8
MLA_Q_LORA = 384
MLA_KV_LORA = 256
MLA_NOPE = 64
MLA_ROPE = 32
MLA_V = 64
SB_HEADS = 8
SB_HEAD_DIM = 64
SB_W = SB_HEADS * SB_HEAD_DIM
DIFF_HEADS = 8
DIFF_HEAD_DIM = 64
DIFF_W = DIFF_HEADS * 2 * DIFF_HEAD_DIM

REL_BUCKETS = 32
REL_MAX_DIST = 128
ROPE_THETA = 10000.0
NORM_EPS = 1e-6
SUBLN_EPS = 1e-5
NEG_INF = -1e30
SB_UNDERFLOW = 105.0
LOG2E = math.log2(math.e)

LANES = 128
V7X_VMEM_BYTES = 64 * 1024 * 1024
VMEM_LIMIT = V7X_VMEM_BYTES * 7 // 8

TM = 512
TQ = 256
TK = 256

F32 = jnp.float32
BF16 = jnp.bfloat16


def _silu(z):
    return z * (1.0 / (1.0 + jnp.exp(-z)))


def _dot(a, b):
    return jnp.dot(a, b, preferred_element_type=F32)


def _dot_nt(a, b):
    return lax.dot_general(a, b, (((1,), (1,)), ((), ())), preferred_element_type=F32)


def _rep(x, n):
    return x if n == 1 else jnp.concatenate([x] * n, axis=1)


def _params(n_axes=1):
    return pltpu.CompilerParams(dimension_semantics=("arbitrary",) * n_axes, vmem_limit_bytes=VMEM_LIMIT)


def _store_kv_tiles(ref, xt):
    for j in range(xt.shape[1] // TK):
        ref[j] = xt[:, j * TK:(j + 1) * TK].astype(ref.dtype)


def _mod_kernel(c_ref, w_ref, b_ref, o_ref):
    ca = _silu(c_ref[...]).astype(BF16)
    o_ref[0] = _dot(ca, w_ref[0].astype(BF16)) + b_ref[0]


def _modulation(c, ada_w, ada_b):
    B, D = c.shape
    out = pl.pallas_call(
        _mod_kernel,
        name="ada_mod",
        grid=(DEPTH, 3),
        in_specs=[
            pl.BlockSpec((B, D), lambda i, j: (0, 0)),
            pl.BlockSpec((1, D, D), lambda i, j: (i, 0, j)),
            pl.BlockSpec((1, 1, D), lambda i, j: (i * 3 + j, 0, 0)),
        ],
        out_specs=pl.BlockSpec((1, B, D), lambda i, j: (i * 3 + j, 0, 0)),
        out_shape=jax.ShapeDtypeStruct((DEPTH * 3, B, D), F32),
        compiler_params=_params(2),
    )(c, ada_w, ada_b.reshape(DEPTH * 3, 1, D))
    return out.reshape(DEPTH * 3 * B, 1, D)


def _rope_kernel(pos_ref, f_ref, cos_ref, sin_ref):
    ang = pos_ref[...].astype(F32) * f_ref[...]
    cos_ref[...] = jnp.cos(ang)
    sin_ref[...] = jnp.sin(ang)


def _rope_tables(pos_offset, S):
    B = pos_offset.shape[0]
    half = MLA_ROPE // 2
    per_row = LANES // half
    inv_freq = ROPE_THETA ** (-jnp.arange(half, dtype=F32) / half)
    pos = pos_offset[:, None] + jnp.arange(S, dtype=jnp.int32)[None, :]
    pos = jnp.repeat(pos.reshape(B * S // per_row, per_row), half, axis=1)
    f = jnp.tile(inv_freq, per_row)[None, :]
    rows = B * S // per_row
    cos, sin = pl.pallas_call(
        _rope_kernel,
        name="rope_tab",
        out_shape=[jax.ShapeDtypeStruct((rows, LANES), F32)] * 2,
    )(pos, f)
    cos = cos.reshape(B * S, half)
    sin = sin.reshape(B * S, half)
    one = jnp.ones((B * S, MLA_NOPE), F32)
    zn = jnp.zeros((B * S, MLA_NOPE), F32)
    zp = jnp.zeros((B * S, LANES - MLA_NOPE - MLA_ROPE), F32)
    ctab = jnp.concatenate([one, cos, cos, zp], axis=1)
    stab = jnp.concatenate([zn, -sin, sin, zp], axis=1)
    return ctab, stab


AB_SEG = (MLA_Q_LORA, MLA_KV_LORA, SB_W, SB_W, MLA_HEADS * MLA_V + SB_W, LANES, LANES)
AB_OFF = tuple(int(v) for v in np.cumsum((0,) + AB_SEG))


def _rms(x, eps):
    return x * lax.rsqrt(jnp.mean(x * x, axis=-1, keepdims=True) + eps)


def _in0_kernel(x_ref, shift_ref, scale_ref, g_ref, w_ref, wqa_ref, wqb_ref, wk_ref, wvt_ref, wvst_ref,
                qg_ref, kvg_ref, c_ref, s_ref,
                qm_ref, km_ref, vmt_ref, qs_ref, ks_ref, vst_ref, z_ref):
    x = x_ref[...]
    h = _rms(x, NORM_EPS) * g_ref[...] * (1.0 + scale_ref[0]) + shift_ref[0]
    h = h.astype(BF16)

    def seg(i):
        return _dot(h, w_ref[:, AB_OFF[i]:AB_OFF[i + 1]])

    ctab = c_ref[...]
    stab = s_ref[...]
    nh = MLA_HEADS

    cq = (_rms(seg(0), NORM_EPS) * qg_ref[...]).astype(BF16)
    q = _dot(cq, wqa_ref[...]) * _rep(ctab, nh) + _dot(cq, wqb_ref[...]) * _rep(stab, nh)
    qm_ref[...] = (q * ((MLA_NOPE + MLA_ROPE) ** -0.5 * LOG2E)).astype(BF16)

    ckv = (_rms(seg(1), NORM_EPS) * kvg_ref[...]).astype(BF16)
    krope = seg(5) * ctab + seg(6) * stab
    km_ref[...] = (_dot(ckv, wk_ref[...]) + _rep(krope, nh)).astype(BF16)
    _store_kv_tiles(vmt_ref, _dot_nt(wvt_ref[...], ckv))

    qs_ref[...] = seg(2).astype(BF16)
    ks_ref[...] = seg(3).astype(BF16)
    _store_kv_tiles(vst_ref, _dot_nt(wvst_ref[...], h))
    z_ref[...] = seg(4).astype(BF16)


def _prep_ab_weights(w_in, w_uq, w_ukv):
    D = w_in.shape[0]
    c = np.cumsum([MLA_Q_LORA, MLA_KV_LORA, MLA_ROPE, SB_W, SB_W, SB_W]).tolist()
    half = MLA_ROPE // 2
    w_kr = w_in[:, c[1]:c[2]]
    zl = jnp.zeros((D, MLA_NOPE), w_in.dtype)
    zr = jnp.zeros((D, LANES - MLA_NOPE - MLA_ROPE), w_in.dtype)
    kr_a = jnp.concatenate([zl, w_kr, zr], axis=1)
    kr_b = jnp.concatenate([zl, w_kr[:, half:], w_kr[:, :half], zr], axis=1)
    w0 = jnp.concatenate([
        w_in[:, :c[1]],
        w_in[:, c[2]:c[3]] * SB_HEAD_DIM ** -0.5,
        w_in[:, c[3]:c[4]],
        w_in[:, c[5]:],
        kr_a, kr_b], axis=1).astype(BF16)
    wvst = w_in[:, c[4]:c[5]].T.astype(BF16)

    hq = MLA_NOPE + MLA_ROPE
    uq = w_uq.reshape(MLA_Q_LORA, MLA_HEADS, hq)
    pad = jnp.zeros((MLA_Q_LORA, MLA_HEADS, LANES - hq), w_uq.dtype)
    zq = jnp.zeros((MLA_Q_LORA, MLA_HEADS, MLA_NOPE), w_uq.dtype)
    wqa = jnp.concatenate([uq, pad], axis=2).reshape(MLA_Q_LORA, MLA_HEADS * LANES).astype(BF16)
    wqb = jnp.concatenate([zq, uq[:, :, MLA_NOPE + half:], uq[:, :, MLA_NOPE:MLA_NOPE + half], pad],
                          axis=2).reshape(MLA_Q_LORA, MLA_HEADS * LANES).astype(BF16)
    ukv = w_ukv.reshape(MLA_KV_LORA, MLA_HEADS, MLA_NOPE + MLA_V)
    zk = jnp.zeros((MLA_KV_LORA, MLA_HEADS, LANES - MLA_NOPE), w_ukv.dtype)
    wk = jnp.concatenate([ukv[:, :, :MLA_NOPE], zk], axis=2).reshape(MLA_KV_LORA, MLA_HEADS * LANES).astype(BF16)
    wvt = ukv[:, :, MLA_NOPE:].reshape(MLA_KV_LORA, MLA_HEADS * MLA_V).T.astype(BF16)
    return w0, wqa, wqb, wk, wvt, wvst


def _layer0_in(x2, mod3, g, w0, wqa, wqb, wk, wvt, wvst, qg, kvg, ctab, stab, B):
    T, D = x2.shape
    nt = T // TM
    per_b = nt // B
    full = lambda a: pl.BlockSpec(a.shape, lambda i: (0,) * a.ndim)
    tok = lambda w: pl.BlockSpec((TM, w), lambda i: (i, 0))
    kvt = lambda w: pl.BlockSpec((TM // TK, w, TK), lambda i: (i, 0, 0))
    tok_out = lambda w: jax.ShapeDtypeStruct((T, w), BF16)
    kvt_out = lambda w: jax.ShapeDtypeStruct((T // TK, w, TK), BF16)
    hv = MLA_HEADS * MLA_V
    return pl.pallas_call(
        _in0_kernel,
        name="l0_in",
        grid=(nt,),
        in_specs=[
            tok(D),
            pl.BlockSpec((1, 1, D), lambda i: (0 * B + i // per_b, 0, 0)),
            pl.BlockSpec((1, 1, D), lambda i: (1 * B + i // per_b, 0, 0)),
            full(g), full(w0), full(wqa), full(wqb), full(wk), full(wvt), full(wvst), full(qg), full(kvg),
            tok(LANES), tok(LANES),
        ],
        out_specs=[tok(MLA_HEADS * LANES), tok(MLA_HEADS * LANES), kvt(hv), tok(SB_W), tok(SB_W), kvt(SB_W),
                   tok(hv + SB_W)],
        out_shape=[tok_out(MLA_HEADS * LANES), tok_out(MLA_HEADS * LANES), kvt_out(hv), tok_out(SB_W),
                   tok_out(SB_W), kvt_out(SB_W), tok_out(hv + SB_W)],
        compiler_params=_params(1),
    )(x2, mod3, mod3, g, w0, wqa, wqb, wk, wvt, wvst, qg, kvg, ctab, stab)


def _chunk_mask_t():
    k = lax.broadcasted_iota(jnp.int32, (TK, TQ), 0)
    q = lax.broadcasted_iota(jnp.int32, (TK, TQ), 1)
    shift = CHUNK.bit_length() - 1
    return (k >> shift) <= (q >> shift)


def _tile_start(i, size):
    return i * size if isinstance(i, int) else pl.multiple_of(i * size, size)


def _tile_walk(n_tiles, body):
    n_items = n_tiles * (n_tiles + 1) // 2
    assert n_items % 2 == 0

    def visit(slot, qi, kj):
        diag = kj == qi
        nqi = jnp.where(diag, qi + 1, qi)
        nkj = jnp.where(diag, 0, kj + 1)
        body(slot, qi, kj, jnp.minimum(nqi, n_tiles - 1), nkj)
        return nqi, nkj

    def step(i, carry):
        return visit(1, *visit(0, *carry))

    lax.fori_loop(0, n_items // 2, step, (jnp.int32(0), jnp.int32(0)))


def _softmax_step(s_ref, mc_ref, m_ref, l_ref, alpha_ref, p_ref, slot, h, rows):
    m_prev = m_ref[h]
    m_new = jnp.maximum(m_prev, mc_ref[slot, h])
    alpha = jnp.exp2(m_prev - m_new)
    p = jnp.exp2(s_ref[slot, rows, :] - m_new)
    l_ref[h] = alpha * l_ref[h] + jnp.sum(p, axis=0, keepdims=True)
    p_ref[rows, :] = p.astype(BF16)
    alpha_ref[h] = alpha
    m_ref[h] = m_new


def _softmax_init(m_ref, l_ref, acc_ref):
    m_ref[...] = jnp.full(m_ref.shape, -jnp.inf, F32)
    l_ref[...] = jnp.zeros(l_ref.shape, F32)
    acc_ref[...] = jnp.zeros(acc_ref.shape, F32)


def _mla_attn_kernel(q_ref, k_ref, vt_ref, o_ref, m_ref, l_ref, alpha_ref, acc_ref, mc_ref, s_ref, p_ref):
    S = q_ref.shape[0]
    mask = _chunk_mask_t()
    krows = [pl.ds(h * TK, TK) for h in range(MLA_HEADS)]
    vrows = [pl.ds(h * MLA_V, MLA_V) for h in range(MLA_HEADS)]

    def scores(qi, kj, slot, masked):
        q0 = _tile_start(qi, TQ)
        k0 = _tile_start(kj, TK)
        for h in range(MLA_HEADS):
            hl = slice(h * LANES, (h + 1) * LANES)
            s = _dot_nt(k_ref[pl.ds(k0, TK), hl], q_ref[pl.ds(q0, TQ), hl])
            if masked:
                s = jnp.where(mask, s, NEG_INF)
            s_ref[slot, krows[h], :] = s
            mc_ref[slot, h] = jnp.max(s, axis=0, keepdims=True)

    def consume(kj, slot):
        for h in range(MLA_HEADS):
            _softmax_step(s_ref, mc_ref, m_ref, l_ref, alpha_ref, p_ref, slot, h, krows[h])
        for h in range(MLA_HEADS):
            acc_ref[vrows[h], :] = (alpha_ref[h] * acc_ref[vrows[h], :]
                                    + _dot(vt_ref[kj, vrows[h], :], p_ref[krows[h], :]))

    def finish(qi):
        q0 = pl.multiple_of(qi * TQ, TQ)
        for j in range(MLA_HEADS // 2):
            ot = jnp.concatenate([acc_ref[vrows[2 * j], :] / l_ref[2 * j],
                                  acc_ref[vrows[2 * j + 1], :] / l_ref[2 * j + 1]], axis=0)
            o_ref[pl.ds(q0, TQ), j * LANES:(j + 1) * LANES] = ot.T.astype(BF16)
        _softmax_init(m_ref, l_ref, acc_ref)

    def variant(slot, next_masked, cur_diag):
        def run(qi, kj, nqi, nkj):
            scores(nqi, nkj, 1 - slot, next_masked)
            consume(kj, slot)
            if cur_diag:
                finish(qi)
        return run

    variants = [[variant(slot, False, False), variant(slot, True, False), variant(slot, False, True)]
                for slot in range(2)]

    def body(slot, qi, kj, nqi, nkj):
        idx = jnp.where(kj == qi, 2, jnp.where(nkj == nqi, 1, 0))
        lax.switch(idx, variants[slot], qi, kj, nqi, nkj)

    _softmax_init(m_ref, l_ref, acc_ref)
    scores(0, 0, 0, True)
    _tile_walk(S // TQ, body)


def _sb_attn_kernel(q_ref, k_ref, vt_ref, o_ref, r_ref, rs_ref, acc_ref, z_ref, ls_ref, hi_ref, lo_ref):
    S = q_ref.shape[0]
    k_i = lax.broadcasted_iota(jnp.int32, (TK, TQ), 0)
    q_i = lax.broadcasted_iota(jnp.int32, (TK, TQ), 1)
    causal = k_i < q_i
    tri = jnp.where(lax.broadcasted_iota(jnp.int32, (TK, TK), 1) > lax.broadcasted_iota(jnp.int32, (TK, TK), 0),
                    1.0, 0.0).astype(BF16)
    lane = lax.broadcasted_iota(jnp.int32, (TQ, LANES), 1)
    first = lane < SB_HEAD_DIM
    krows = [pl.ds(h * TK, TK) for h in range(SB_HEADS)]
    vrows = [pl.ds(h * SB_HEAD_DIM, SB_HEAD_DIM) for h in range(SB_HEADS)]
    pairs = [slice((h // 2) * LANES, (h // 2 + 1) * LANES) for h in range(SB_HEADS)]

    def step(q0, kj, masked):
        k0 = pl.multiple_of(kj * TK, TK)
        for h in range(SB_HEADS):
            q = q_ref[pl.ds(q0, TQ), pairs[h]]
            q = jnp.where(first, q, jnp.zeros_like(q)) if h % 2 == 0 else jnp.where(first, jnp.zeros_like(q), q)
            z_ref[krows[h], :] = _dot_nt(k_ref[pl.ds(k0, TK), pairs[h]], q)
        for h in range(SB_HEADS):
            z = z_ref[krows[h], :]
            sp = jnp.log(1.0 + jnp.exp(-jnp.abs(z)))
            ls = jnp.minimum(z, 0.0) - sp
            l1m = ls - z
            if masked:
                l1m = jnp.where(causal, l1m, 0.0)
            hi = l1m.astype(BF16)
            ls_ref[krows[h], :] = ls
            hi_ref[krows[h], :] = hi
            lo_ref[krows[h], :] = (l1m - hi.astype(F32)).astype(BF16)
            rs_ref[h] = jnp.sum(l1m, axis=0, keepdims=True)
        for h in range(SB_HEADS):
            z_ref[krows[h], :] = _dot(tri, hi_ref[krows[h], :]) + _dot(tri, lo_ref[krows[h], :])
        for h in range(SB_HEADS):
            r_prev = r_ref[h]
            w = jnp.exp(ls_ref[krows[h], :] + z_ref[krows[h], :] + r_prev)
            if masked:
                w = jnp.where(causal, w, 0.0)
            hi_ref[krows[h], :] = w.astype(BF16)
            r_ref[h] = r_prev + rs_ref[h]
        for h in range(SB_HEADS):
            acc_ref[vrows[h], :] += _dot(vt_ref[kj, vrows[h], :], hi_ref[krows[h], :])

    def q_body(qi, carry):
        q0 = pl.multiple_of(qi * TQ, TQ)
        r_ref[...] = jnp.zeros(r_ref.shape, F32)
        acc_ref[...] = jnp.zeros(acc_ref.shape, F32)
        step(q0, qi, True)

        def cond(c):
            t, live = c
            return jnp.logical_and(t < qi, live > 0)

        def body(c):
            t, _ = c
            step(q0, qi - 1 - t, False)
            return t + 1, (jnp.max(r_ref[...]) > -SB_UNDERFLOW).astype(jnp.int32)

        lax.while_loop(cond, body, (jnp.int32(0), jnp.int32(1)))
        for j in range(SB_HEADS // 2):
            ot = acc_ref[pl.ds(j * LANES, LANES), :]
            o_ref[pl.ds(q0, TQ), j * LANES:(j + 1) * LANES] = ot.T.astype(BF16)
        return carry

    lax.fori_loop(0, S // TQ, q_body, 0)


def _diff_attn_kernel(lam_init, q_ref, k_ref, vt_ref, bias_ref, lamp_ref, g_ref, o_ref,
                      m_ref, l_ref, alpha_ref, acc_ref, mc_ref, s_ref, p_ref):
    S = q_ref.shape[0]
    mask = _chunk_mask_t()
    mask2 = jnp.concatenate([mask, mask], axis=1)
    lane = lax.broadcasted_iota(jnp.int32, (TQ, LANES), 1)
    first = lane < DIFF_HEAD_DIM
    lp = lamp_ref[...]
    lam = (jnp.exp(jnp.sum(lp[0:1] * lp[1:2], axis=-1, keepdims=True))
           - jnp.exp(jnp.sum(lp[2:3] * lp[3:4], axis=-1, keepdims=True)) + lam_init)
    krows = [pl.ds(h * TK, TK) for h in range(DIFF_HEADS)]
    vrows = [pl.ds(h * LANES, LANES) for h in range(DIFF_HEADS)]
    FAR, NEAR, DIAG = 0, 1, 2

    def scores(qi, kj, slot, kind):
        q0 = _tile_start(qi, TQ)
        k0 = _tile_start(kj, TK)
        for h in range(DIFF_HEADS):
            hl = slice(h * LANES, (h + 1) * LANES)
            q = q_ref[pl.ds(q0, TQ), hl]
            zero = jnp.zeros_like(q)
            q2 = jnp.concatenate([jnp.where(first, q, zero), jnp.where(first, zero, q)], axis=0)
            s = _dot_nt(k_ref[pl.ds(k0, TK), hl], q2)
            if kind != FAR:
                s = s + _rep(bias_ref[h, kind - 1], 2)
            if kind == DIAG:
                s = jnp.where(mask2, s, NEG_INF)
            s_ref[slot, krows[h], :] = s
            mc_ref[slot, h] = jnp.max(s, axis=0, keepdims=True)

    def consume(kj, slot):
        for h in range(DIFF_HEADS):
            _softmax_step(s_ref, mc_ref, m_ref, l_ref, alpha_ref, p_ref, slot, h, krows[h])
        for h in range(DIFF_HEADS):
            acc_ref[vrows[h], :] = (alpha_ref[h] * acc_ref[vrows[h], :]
                                    + _dot(vt_ref[kj, vrows[h], :], p_ref[krows[h], :]))

    def finish(qi):
        q0 = pl.multiple_of(qi * TQ, TQ)
        for h in range(DIFF_HEADS):
            on = acc_ref[vrows[h], :] / l_ref[h]
            ot = on[:, :TQ] - lam * on[:, TQ:]
            ot = ot * lax.rsqrt(jnp.mean(ot * ot, axis=0, keepdims=True) + SUBLN_EPS)
            o_ref[pl.ds(q0, TQ), h * LANES:(h + 1) * LANES] = (ot.T * g_ref[...] * (1.0 - lam_init)).astype(BF16)
        _softmax_init(m_ref, l_ref, acc_ref)

    def variant(slot, next_kind, cur_diag):
        def run(qi, kj, nqi, nkj):
            scores(nqi, nkj, 1 - slot, next_kind)
            consume(kj, slot)
            if cur_diag:
                finish(qi)
        return run

    variants = [[variant(slot, FAR, False), variant(slot, NEAR, False), variant(slot, DIAG, False),
                 variant(slot, FAR, True), variant(slot, NEAR, True)] for slot in range(2)]

    def body(slot, qi, kj, nqi, nkj):
        next_near = (nkj == nqi - 1).astype(jnp.int32)
        idx = jnp.where(kj == qi, 3 + next_near, jnp.where(kj == qi - 1, 2, next_near))
        lax.switch(idx, variants[slot], qi, kj, nqi, nkj)

    _softmax_init(m_ref, l_ref, acc_ref)
    scores(0, 0, 0, DIAG)
    _tile_walk(S // TQ, body)


def _bias_kernel(idx_ref, tab_ref, o_ref):
    h = pl.program_id(0)
    far = tab_ref[REL_BUCKETS // 2 - 1, h]
    for t in range(2):
        idx = idx_ref[t]
        acc = jnp.zeros(idx.shape, F32)
        for b in range(REL_BUCKETS):
            acc = jnp.where(idx == b, (tab_ref[b, h] - far) * LOG2E, acc)
        o_ref[0, t] = acc


def _t5_bucket_np(rel):
    nb = REL_BUCKETS // 2
    max_exact = nb // 2
    ret = np.where(rel > 0, nb, 0)
    n = np.abs(rel)
    nf = np.maximum(n, 1).astype(np.float32)
    large = max_exact + (np.log(nf / np.float32(max_exact)) / np.float32(math.log(REL_MAX_DIST / max_exact))
                         * np.float32(nb - max_exact)).astype(np.int32)
    large = np.minimum(large, nb - 1)
    return (ret + np.where(n < max_exact, n, large)).astype(np.int32)


def _bias_tiles(rel_table):
    assert TK >= REL_MAX_DIST
    rel = np.stack([(np.arange(TK)[:, None] + (t - 1) * TK) - np.arange(TQ)[None, :] for t in range(2)])
    idx = jnp.asarray(_t5_bucket_np(rel))
    return pl.pallas_call(
        _bias_kernel,
        name="rel_bias",
        grid=(DIFF_HEADS,),
        in_specs=[pl.BlockSpec((2, TK, TQ), lambda h: (0, 0, 0)),
                  pl.BlockSpec(memory_space=pltpu.SMEM)],
        out_specs=pl.BlockSpec((1, 2, TK, TQ), lambda h: (h, 0, 0, 0)),
        out_shape=jax.ShapeDtypeStruct((DIFF_HEADS, 2, TK, TQ), F32),
        compiler_params=_params(1),
    )(idx, rel_table.astype(F32))


def _attention_call(kernel_fn, name, q, k, vt, extra, out_width, scratch, B, S):
    T = q.shape[0]
    seq = lambda a: pl.BlockSpec((S, a.shape[1]), lambda b: (b, 0))
    const = lambda a: pl.BlockSpec(a.shape, lambda b: (0,) * a.ndim, pipeline_mode=pl.Buffered(1))
    return pl.pallas_call(
        kernel_fn,
        name=name,
        grid=(B,),
        in_specs=[seq(q), seq(k), pl.BlockSpec((S // TK,) + vt.shape[1:], lambda b: (b, 0, 0))]
        + [const(a) for a in extra],
        out_specs=pl.BlockSpec((S, out_width), lambda b: (b, 0)),
        out_shape=jax.ShapeDtypeStruct((T, out_width), BF16),
        scratch_shapes=scratch,
        compiler_params=_params(1),
    )(q, k, vt, *extra)


def _softmax_scratch(heads, n, v_rows):
    return ([pltpu.VMEM((heads, 1, n), F32)] * 3 + [pltpu.VMEM((v_rows, n), F32)]
            + [pltpu.VMEM((2, heads, 1, n), F32), pltpu.VMEM((2, heads * TK, n), F32)]
            + [pltpu.VMEM((heads * TK, n), BF16)])


def _mla_attention(qm, km, vmt, B, S):
    return _attention_call(_mla_attn_kernel, "mla_attn", qm, km, vmt, (), MLA_HEADS * MLA_V,
                           _softmax_scratch(MLA_HEADS, TQ, MLA_HEADS * MLA_V), B, S)


def _sb_attention(qs, ks, vst, B, S):
    n = SB_HEADS * TK
    scratch = ([pltpu.VMEM((SB_HEADS, 1, TQ), F32)] * 2 + [pltpu.VMEM((SB_W, TQ), F32)]
               + [pltpu.VMEM((n, TQ), F32)] * 2 + [pltpu.VMEM((n, TQ), BF16)] * 2)
    return _attention_call(_sb_attn_kernel, "sb_attn", qs, ks, vst, (), SB_W, scratch, B, S)


def _diff_attention(qd, kd, vdt, bias, lamp, g, lam_init, B, S):
    return _attention_call(functools.partial(_diff_attn_kernel, lam_init), "diff_attn", qd, kd, vdt,
                           (bias, lamp, g), DIFF_W, _softmax_scratch(DIFF_HEADS, 2 * TQ, DIFF_W), B, S)


def _out0_in1_kernel(om_ref, os_ref, z_ref, x_ref, gate_ref, wo_ref, shift_ref, scale_ref, g_ref, w1_ref, w1vt_ref,
                     x1_ref, qd_ref, kd_ref, vdt_ref, zd_ref):
    z = z_ref[...].astype(F32)
    y = jnp.concatenate([om_ref[...].astype(F32), os_ref[...].astype(F32)], axis=1) * _silu(z)
    out = _dot(y.astype(BF16), wo_ref[...])
    x1 = x_ref[...] + gate_ref[0] * out
    x1_ref[...] = x1
    h = (_rms(x1, NORM_EPS) * g_ref[...] * (1.0 + scale_ref[0]) + shift_ref[0]).astype(BF16)
    qd_ref[...] = (_dot(h, w1_ref[:, 0:DIFF_W]) * (DIFF_HEAD_DIM ** -0.5 * LOG2E)).astype(BF16)
    kd_ref[...] = _dot(h, w1_ref[:, DIFF_W:2 * DIFF_W]).astype(BF16)
    zd_ref[...] = _dot(h, w1_ref[:, 2 * DIFF_W:3 * DIFF_W]).astype(BF16)
    _store_kv_tiles(vdt_ref, _dot_nt(w1vt_ref[...], h))


def _layer0_out_layer1_in(om, osb, z, x2, mod3, wo, g1, w1, w1vt, B):
    T, D = x2.shape
    nt = T // TM
    per_b = nt // B
    full = lambda a: pl.BlockSpec(a.shape, lambda i: (0,) * a.ndim)
    tok = lambda w: pl.BlockSpec((TM, w), lambda i: (i, 0))
    mod = lambda row: pl.BlockSpec((1, 1, D), lambda i: (row * B + i // per_b, 0, 0))
    tok_out = jax.ShapeDtypeStruct((T, DIFF_W), BF16)
    return pl.pallas_call(
        _out0_in1_kernel,
        name="l0_out_l1_in",
        grid=(nt,),
        in_specs=[tok(om.shape[1]), tok(osb.shape[1]), tok(D), tok(D), mod(2), full(wo),
                  mod(3), mod(4), full(g1), full(w1), full(w1vt)],
        out_specs=[tok(D), tok(DIFF_W), tok(DIFF_W),
                   pl.BlockSpec((TM // TK, DIFF_W, TK), lambda i: (i, 0, 0)), tok(DIFF_W)],
        out_shape=[jax.ShapeDtypeStruct((T, D), F32), tok_out, tok_out,
                   jax.ShapeDtypeStruct((T // TK, DIFF_W, TK), BF16), tok_out],
        compiler_params=_params(1),
    )(om, osb, z, x2, mod3, wo, mod3, mod3, g1, w1, w1vt)


def _out1_kernel(o_ref, z_ref, x_ref, gate_ref, wo_ref, g_ref, y_ref):
    y = o_ref[...].astype(F32) * _silu(z_ref[...].astype(F32))
    out = _dot(y.astype(BF16), wo_ref[...])
    x2 = x_ref[...] + gate_ref[0] * out
    y_ref[...] = _rms(x2, NORM_EPS) * g_ref[...]


def _layer1_out(od, zd, x1, mod3, wo, gf, B):
    T, D = x1.shape
    nt = T // TM
    per_b = nt // B
    full = lambda a: pl.BlockSpec(a.shape, lambda i: (0,) * a.ndim)
    tok = lambda w: pl.BlockSpec((TM, w), lambda i: (i, 0))
    return pl.pallas_call(
        _out1_kernel,
        name="l1_out",
        grid=(nt,),
        in_specs=[tok(D), tok(D), tok(D), pl.BlockSpec((1, 1, D), lambda i: (5 * B + i // per_b, 0, 0)),
                  full(wo), full(gf)],
        out_specs=tok(D),
        out_shape=jax.ShapeDtypeStruct((T, D), F32),
        compiler_params=_params(1),
    )(od, zd, x1, mod3, wo, gf)


def kernel(x, c, pos_offset, rel_bias_table, ada_w, ada_b, norm_g, final_g, ab_w_in, ab_q_norm_g, ab_kv_norm_g,
           ab_w_uq, ab_w_ukv, ab_w_out, dif_w_in, dif_lam_q1, dif_lam_k1, dif_lam_q2, dif_lam_k2, dif_subln_g,
           dif_w_out):
    B, S, D = x.shape
    assert D == D_MODEL and S % TQ == 0 and TQ == TK and (B * S) % TM == 0 and S % TM == 0 and TM % TK == 0
    x2 = x.reshape(B * S, D)

    mod3 = _modulation(c, ada_w, ada_b)
    ctab, stab = _rope_tables(pos_offset, S)

    w0, wqa, wqb, wk, wvt, wvst = _prep_ab_weights(ab_w_in[0], ab_w_uq[0], ab_w_ukv[0])
    qm, km, vmt, qs, ks, vst, z = _layer0_in(
        x2, mod3, norm_g[0:1], w0, wqa, wqb, wk, wvt, wvst, ab_q_norm_g[0:1], ab_kv_norm_g[0:1], ctab, stab, B)
    om = _mla_attention(qm, km, vmt, B, S)
    osb = _sb_attention(qs, ks, vst, B, S)

    wd = dif_w_in[0]
    w1 = jnp.concatenate([wd[:, :2 * DIFF_W], wd[:, 3 * DIFF_W:]], axis=1).astype(BF16)
    w1vt = wd[:, 2 * DIFF_W:3 * DIFF_W].T.astype(BF16)
    x1, qd, kd, vdt, zd = _layer0_out_layer1_in(
        om, osb, z, x2, mod3, ab_w_out[0].astype(BF16), norm_g[1:2], w1, w1vt, B)
    lam_init = 0.8 - 0.6 * math.exp(-0.3 * 1)
    lamp = jnp.stack([dif_lam_q1[0], dif_lam_k1[0], dif_lam_q2[0], dif_lam_k2[0]]).astype(F32)
    bias = _bias_tiles(rel_bias_table)
    od = _diff_attention(qd, kd, vdt, bias, lamp, dif_subln_g[0:1], lam_init, B, S)
    y = _layer1_out(od, zd, x1, mod3, dif_w_out[0].astype(BF16), final_g[None, :], B)
    return y.reshape(B, S, D)
```

```python
import functools
import math

import numpy as np
import jax
import jax.numpy as jnp
from jax import lax
from jax.experimental import pallas as pl
from jax.experimental.pallas import tpu as pltpu

D_MODEL = 1024
DEPTH = 2
CHUNK = 64

MLA_HEADS = 8
MLA_Q_LORA = 384
MLA_KV_LORA = 256
MLA_NOPE = 64
MLA_ROPE = 32
MLA_V = 64
SB_HEADS = 8
SB_HEAD_DIM = 64
SB_W = SB_HEADS * SB_HEAD_DIM
DIFF_HEADS = 8
DIFF_HEAD_DIM = 64
DIFF_W = DIFF_HEADS * 2 * DIFF_HEAD_DIM

REL_BUCKETS = 32
REL_MAX_DIST = 128
ROPE_THETA = 10000.0
NORM_EPS = 1e-6
SUBLN_EPS = 1e-5
NEG_INF = -1e30
SB_UNDERFLOW_LOG2 = 152.0
LOG2E = math.log2(math.e)

LANES = 128
V7X_VMEM_BYTES = 64 * 1024 * 1024
VMEM_LIMIT = V7X_VMEM_BYTES * 7 // 8

TM = 512
TQ = 256
TK = 256
SKEW = 0
ONES_ROWS = 16

F32 = jnp.float32
BF16 = jnp.bfloat16


def _silu(z):
    return z * (1.0 / (1.0 + jnp.exp(-z)))


def _dot(a, b):
    return jnp.dot(a, b, preferred_element_type=F32)


def _dot_nt(a, b):
    return lax.dot_general(a, b, (((1,), (1,)), ((), ())), preferred_element_type=F32)


def _rep(x, n):
    return x if n == 1 else jnp.concatenate([x] * n, axis=1)


def _params(n_axes=1):
    return pltpu.CompilerParams(dimension_semantics=("arbitrary",) * n_axes, vmem_limit_bytes=VMEM_LIMIT)


def _store_kv_tiles(ref, xt):
    for j in range(xt.shape[1] // TK):
        ref[j] = xt[:, j * TK:(j + 1) * TK].astype(ref.dtype)


def _mod_kernel(c_ref, w_ref, b_ref, o_ref):
    ca = _silu(c_ref[...]).astype(BF16)
    o_ref[0] = _dot(ca, w_ref[0].astype(BF16)) + b_ref[0]


def _modulation(c, ada_w, ada_b):
    B, D = c.shape
    out = pl.pallas_call(
        _mod_kernel,
        name="ada_mod",
        grid=(DEPTH, 3),
        in_specs=[
            pl.BlockSpec((B, D), lambda i, j: (0, 0)),
            pl.BlockSpec((1, D, D), lambda i, j: (i, 0, j)),
            pl.BlockSpec((1, 1, D), lambda i, j: (i * 3 + j, 0, 0)),
        ],
        out_specs=pl.BlockSpec((1, B, D), lambda i, j: (i * 3 + j, 0, 0)),
        out_shape=jax.ShapeDtypeStruct((DEPTH * 3, B, D), F32),
        compiler_params=_params(2),
    )(c, ada_w, ada_b.reshape(DEPTH * 3, 1, D))
    return out.reshape(DEPTH * 3 * B, 1, D)


def _rope_static_tables(S):
    half = MLA_ROPE // 2
    inv_freq = np.float32(ROPE_THETA) ** (-(np.arange(half, dtype=np.float32) / np.float32(half)))
    inv_freq = inv_freq.astype(np.float32)
    ang = np.arange(S, dtype=np.float64)[:, None] * inv_freq.astype(np.float64)[None, :]
    cos, sin = np.cos(ang), np.sin(ang)
    one = np.ones((S, MLA_NOPE))
    zn = np.zeros((S, MLA_NOPE))
    zp = np.zeros((S, LANES - MLA_NOPE - MLA_ROPE))
    tabs = [np.concatenate(parts, axis=1).astype(np.float32) for parts in (
        (one, cos, cos, zp), (zn, sin, sin, zp), (zn, -cos, cos, zp), (zn, -sin, sin, zp))]
    f = np.concatenate([np.zeros(MLA_NOPE, np.float32), inv_freq, inv_freq,
                        np.zeros(LANES - MLA_NOPE - MLA_ROPE, np.float32)])[None, :]
    return [jnp.asarray(t) for t in tabs], jnp.asarray(f)


def _rope_tiles(pos0, f_ref, cs_ref, ss_ref, cs2_ref, ss2_ref):
    ang0 = pos0.astype(F32) * f_ref[...]
    a, b = jnp.cos(ang0), jnp.sin(ang0)
    ctab = a * cs_ref[...] - b * ss_ref[...]
    stab = b * cs2_ref[...] + a * ss2_ref[...]
    return ctab, stab


AB_SEG = (MLA_Q_LORA, MLA_KV_LORA, SB_W, SB_W, MLA_HEADS * MLA_V + SB_W, LANES, LANES)
AB_OFF = tuple(int(v) for v in np.cumsum((0,) + AB_SEG))


def _rms(x, eps):
    return x * lax.rsqrt(jnp.mean(x * x, axis=-1, keepdims=True) + eps)


def _in0_kernel(per_b, pos_ref, x_ref, shift_ref, scale_ref, g_ref, w_ref, wqa_ref, wqb_ref, wk_ref, wvt_ref,
                wvst_ref, qg_ref, kvg_ref, f_ref, cs_ref, ss_ref, cs2_ref, ss2_ref,
                qm_ref, km_ref, vmt_ref, qs_ref, ks_ref, vst_ref, z_ref):
    x = x_ref[...]
    h = _rms(x, NORM_EPS) * g_ref[...] * (1.0 + scale_ref[0]) + shift_ref[0]
    h = h.astype(BF16)

    def seg(i):
        return _dot(h, w_ref[:, AB_OFF[i]:AB_OFF[i + 1]])

    ctab, stab = _rope_tiles(pos_ref[pl.program_id(0) // per_b], f_ref, cs_ref, ss_ref, cs2_ref, ss2_ref)
    nh = MLA_HEADS

    cq = (_rms(seg(0), NORM_EPS) * qg_ref[...]).astype(BF16)
    q = _dot(cq, wqa_ref[...]) * _rep(ctab, nh) + _dot(cq, wqb_ref[...]) * _rep(stab, nh)
    qm_ref[...] = (q * ((MLA_NOPE + MLA_ROPE) ** -0.5 * LOG2E)).astype(BF16)

    ckv = (_rms(seg(1), NORM_EPS) * kvg_ref[...]).astype(BF16)
    krope = seg(5) * ctab + seg(6) * stab
    km_ref[...] = (_dot(ckv, wk_ref[...]) + _rep(krope, nh)).astype(BF16)
    _store_kv_tiles(vmt_ref, _dot_nt(wvt_ref[...], ckv))

    qs_ref[...] = (seg(2) * (SB_HEAD_DIM ** -0.5 * LOG2E)).astype(BF16)
    ks_ref[...] = seg(3).astype(BF16)
    _store_kv_tiles(vst_ref, _dot_nt(wvst_ref[...], h))
    z_ref[...] = seg(4).astype(BF16)


def _prep_ab_weights(w_in, w_uq, w_ukv):
    D = w_in.shape[0]
    c = np.cumsum([MLA_Q_LORA, MLA_KV_LORA, MLA_ROPE, SB_W, SB_W, SB_W]).tolist()
    half = MLA_ROPE // 2
    w_kr = w_in[:, c[1]:c[2]]
    zl = jnp.zeros((D, MLA_NOPE), w_in.dtype)
    zr = jnp.zeros((D, LANES - MLA_NOPE - MLA_ROPE), w_in.dtype)
    kr_a = jnp.concatenate([zl, w_kr, zr], axis=1)
    kr_b = jnp.concatenate([zl, w_kr[:, half:], w_kr[:, :half], zr], axis=1)
    w0 = jnp.concatenate([
        w_in[:, :c[1]],
        w_in[:, c[2]:c[3]],
        w_in[:, c[3]:c[4]],
        w_in[:, c[5]:],
        kr_a, kr_b], axis=1).astype(BF16)
    wvst = w_in[:, c[4]:c[5]].T.astype(BF16)

    hq = MLA_NOPE + MLA_ROPE
    uq = w_uq.reshape(MLA_Q_LORA, MLA_HEADS, hq)
    pad = jnp.zeros((MLA_Q_LORA, MLA_HEADS, LANES - hq), w_uq.dtype)
    zq = jnp.zeros((MLA_Q_LORA, MLA_HEADS, MLA_NOPE), w_uq.dtype)
    wqa = jnp.concatenate([uq, pad], axis=2).reshape(MLA_Q_LORA, MLA_HEADS * LANES).astype(BF16)
    wqb = jnp.concatenate([zq, uq[:, :, MLA_NOPE + half:], uq[:, :, MLA_NOPE:MLA_NOPE + half], pad],
                          axis=2).reshape(MLA_Q_LORA, MLA_HEADS * LANES).astype(BF16)
    ukv = w_ukv.reshape(MLA_KV_LORA, MLA_HEADS, MLA_NOPE + MLA_V)
    zk = jnp.zeros((MLA_KV_LORA, MLA_HEADS, LANES - MLA_NOPE), w_ukv.dtype)
    wk = jnp.concatenate([ukv[:, :, :MLA_NOPE], zk], axis=2).reshape(MLA_KV_LORA, MLA_HEADS * LANES).astype(BF16)
    wvt = ukv[:, :, MLA_NOPE:].reshape(MLA_KV_LORA, MLA_HEADS * MLA_V).T.astype(BF16)
    return w0, wqa, wqb, wk, wvt, wvst


def _layer0_in(x2, pos_offset, mod3, g, w0, wqa, wqb, wk, wvt, wvst, qg, kvg, B):
    T, D = x2.shape
    nt = T // TM
    per_b = nt // B
    rope_tabs, rope_f = _rope_static_tables(T // B)
    full = lambda a: pl.BlockSpec(a.shape, lambda i: (0,) * a.ndim)
    tok = lambda w: pl.BlockSpec((TM, w), lambda i: (i, 0))
    seq = pl.BlockSpec((TM, LANES), lambda i: (i % per_b, 0))
    kvt = lambda w: pl.BlockSpec((TM // TK, w, TK), lambda i: (i, 0, 0))
    tok_out = lambda w: jax.ShapeDtypeStruct((T, w), BF16)
    kvt_out = lambda w: jax.ShapeDtypeStruct((T // TK, w, TK), BF16)
    hv = MLA_HEADS * MLA_V
    return pl.pallas_call(
        functools.partial(_in0_kernel, per_b),
        name="l0_in",
        grid=(nt,),
        in_specs=[
            pl.BlockSpec(memory_space=pltpu.SMEM),
            tok(D),
            pl.BlockSpec((1, 1, D), lambda i: (0 * B + i // per_b, 0, 0)),
            pl.BlockSpec((1, 1, D), lambda i: (1 * B + i // per_b, 0, 0)),
            full(g), full(w0), full(wqa), full(wqb), full(wk), full(wvt), full(wvst), full(qg), full(kvg),
            full(rope_f), seq, seq, seq, seq,
        ],
        out_specs=[tok(MLA_HEADS * LANES), tok(MLA_HEADS * LANES), kvt(hv), tok(SB_W), tok(SB_W), kvt(SB_W),
                   tok(hv + SB_W)],
        out_shape=[tok_out(MLA_HEADS * LANES), tok_out(MLA_HEADS * LANES), kvt_out(hv), tok_out(SB_W),
                   tok_out(SB_W), kvt_out(SB_W), tok_out(hv + SB_W)],
        compiler_params=_params(1),
    )(pos_offset, x2, mod3, mod3, g, w0, wqa, wqb, wk, wvt, wvst, qg, kvg, rope_f, *rope_tabs)


def _chunk_mask_t():
    k = lax.broadcasted_iota(jnp.int32, (TK, TQ), 0)
    q = lax.broadcasted_iota(jnp.int32, (TK, TQ), 1)
    shift = CHUNK.bit_length() - 1
    return (k >> shift) <= (q >> shift)


def _tile_start(i, size):
    return i * size if isinstance(i, int) else pl.multiple_of(i * size, size)


def _tile_walk(n_tiles, body):
    n_items = n_tiles * (n_tiles + 1) // 2
    assert n_items % 2 == 0

    def visit(slot, qi, kj):
        diag = kj == qi
        nqi = jnp.where(diag, qi + 1, qi)
        nkj = jnp.where(diag, 0, kj + 1)
        body(slot, qi, kj, jnp.minimum(nqi, n_tiles - 1), nkj)
        return nqi, nkj

    def step(i, carry):
        return visit(1, *visit(0, *carry))

    lax.fori_loop(0, n_items // 2, step, (jnp.int32(0), jnp.int32(0)))


def _softmax_step(s_ref, mc_ref, m_ref, alpha_ref, p_ref, slot, h, rows):
    m_prev = m_ref[h]
    m_new = jnp.maximum(m_prev, mc_ref[slot, h])
    p_ref[rows, :] = jnp.exp2(s_ref[slot, rows, :] - m_new).astype(BF16)
    alpha_ref[h] = jnp.exp2(m_prev - m_new)
    m_ref[h] = m_new


def _values_lhs(vt):
    return jnp.concatenate([vt, jnp.ones((ONES_ROWS, vt.shape[1]), vt.dtype)], axis=0)


def _softmax_init(m_ref, acc_ref):
    m_ref[...] = jnp.full(m_ref.shape, -jnp.inf, F32)
    acc_ref[...] = jnp.zeros(acc_ref.shape, F32)


def _mla_attn_kernel(q_ref, k_ref, vt_ref, o_ref, m_ref, alpha_ref, acc_ref, mc_ref, s_ref, p_ref):
    S = q_ref.shape[0]
    mask = _chunk_mask_t()
    krows = [pl.ds(h * TK, TK) for h in range(MLA_HEADS)]
    vrows = [pl.ds(h * MLA_V, MLA_V) for h in range(MLA_HEADS)]
    arows = [pl.ds(h * (MLA_V + ONES_ROWS), MLA_V + ONES_ROWS) for h in range(MLA_HEADS)]

    def out_t(h):
        base = h * (MLA_V + ONES_ROWS)
        return acc_ref[pl.ds(base, MLA_V), :] * (1.0 / acc_ref[pl.ds(base + MLA_V, 1), :])

    def scores_head(h, qi, kj, slot, masked):
        q0 = _tile_start(qi, TQ)
        k0 = _tile_start(kj, TK)
        hl = slice(h * LANES, (h + 1) * LANES)
        s = _dot_nt(k_ref[pl.ds(k0, TK), hl], q_ref[pl.ds(q0, TQ), hl])
        if masked:
            s = jnp.where(mask, s, NEG_INF)
        s_ref[slot, krows[h], :] = s
        mc_ref[slot, h] = jnp.max(s, axis=0, keepdims=True)

    def scores(qi, kj, slot, masked):
        for h in range(MLA_HEADS):
            scores_head(h, qi, kj, slot, masked)

    def values_head(h, kj):
        acc_ref[arows[h], :] = (alpha_ref[h] * acc_ref[arows[h], :]
                                + _dot(_values_lhs(vt_ref[kj, vrows[h], :]), p_ref[krows[h], :]))

    def step(qi, kj, nqi, nkj, slot, next_masked):
        for h in range(SKEW):
            scores_head(h, nqi, nkj, 1 - slot, next_masked)
        for h in range(MLA_HEADS):
            if h + SKEW < MLA_HEADS:
                scores_head(h + SKEW, nqi, nkj, 1 - slot, next_masked)
            _softmax_step(s_ref, mc_ref, m_ref, alpha_ref, p_ref, slot, h, krows[h])
            if h > 0:
                values_head(h - 1, kj)
        values_head(MLA_HEADS - 1, kj)

    def finish(qi):
        q0 = pl.multiple_of(qi * TQ, TQ)
        for j in range(MLA_HEADS // 2):
            ot = jnp.concatenate([out_t(2 * j), out_t(2 * j + 1)], axis=0)
            o_ref[pl.ds(q0, TQ), j * LANES:(j + 1) * LANES] = ot.T.astype(BF16)
        _softmax_init(m_ref, acc_ref)

    def variant(slot, next_masked, cur_diag):
        def run(qi, kj, nqi, nkj):
            step(qi, kj, nqi, nkj, slot, next_masked)
            if cur_diag:
                finish(qi)
        return run

    variants = [[variant(slot, False, False), variant(slot, True, False), variant(slot, False, True)]
                for slot in range(2)]

    def body(slot, qi, kj, nqi, nkj):
        idx = jnp.where(kj == qi, 2, jnp.where(nkj == nqi, 1, 0))
        lax.switch(idx, variants[slot], qi, kj, nqi, nkj)

    _softmax_init(m_ref, acc_ref)
    scores(0, 0, 0, True)
    _tile_walk(S // TQ, body)


def _sb_attn_kernel(q_ref, k_ref, vt_ref, o_ref, r_ref, rs_ref, acc_ref, z_ref, ls_ref, hi_ref, lo_ref):
    S = q_ref.shape[0]
    k_i = lax.broadcasted_iota(jnp.int32, (TK, TQ), 0)
    q_i = lax.broadcasted_iota(jnp.int32, (TK, TQ), 1)
    causal = k_i < q_i
    tri = jnp.where(lax.broadcasted_iota(jnp.int32, (TK, TK), 1) > lax.broadcasted_iota(jnp.int32, (TK, TK), 0),
                    1.0, 0.0).astype(BF16)
    lane = lax.broadcasted_iota(jnp.int32, (TQ, LANES), 1)
    first = lane < SB_HEAD_DIM
    krows = [pl.ds(h * TK, TK) for h in range(SB_HEADS)]
    vrows = [pl.ds(h * SB_HEAD_DIM, SB_HEAD_DIM) for h in range(SB_HEADS)]
    pairs = [slice((h // 2) * LANES, (h // 2 + 1) * LANES) for h in range(SB_HEADS)]

    def scores_head(h, q0, k0):
        q = q_ref[pl.ds(q0, TQ), pairs[h]]
        q = jnp.where(first, q, jnp.zeros_like(q)) if h % 2 == 0 else jnp.where(first, jnp.zeros_like(q), q)
        z_ref[krows[h], :] = _dot_nt(k_ref[pl.ds(k0, TK), pairs[h]], q)

    def logs_head(h, masked):
        z = z_ref[krows[h], :]
        sp = jnp.log(1.0 + jnp.exp2(-jnp.abs(z))) * LOG2E
        ls = jnp.minimum(z, 0.0) - sp
        l1m = ls - z
        if masked:
            l1m = jnp.where(causal, l1m, 0.0)
        hi = l1m.astype(BF16)
        ls_ref[krows[h], :] = ls
        hi_ref[krows[h], :] = hi
        lo_ref[krows[h], :] = (l1m - hi.astype(F32)).astype(BF16)
        rs_ref[h] = l1m[0:1, :]

    def suffix_head(h):
        z_ref[krows[h], :] = _dot(tri, hi_ref[krows[h], :]) + _dot(tri, lo_ref[krows[h], :])

    def weights_head(h, masked):
        r_prev = r_ref[h]
        suffix = z_ref[krows[h], :]
        w = jnp.exp2(ls_ref[krows[h], :] + suffix + r_prev)
        if masked:
            w = jnp.where(causal, w, 0.0)
        hi_ref[krows[h], :] = w.astype(BF16)
        r_ref[h] = r_prev + (rs_ref[h] + suffix[0:1, :])

    def values_head(h, kj):
        acc_ref[vrows[h], :] += _dot(vt_ref[kj, vrows[h], :], hi_ref[krows[h], :])

    def step(q0, kj, masked):
        k0 = pl.multiple_of(kj * TK, TK)
        lead, lag = 2, 2
        for h in range(lead):
            scores_head(h, q0, k0)
        for t in range(SB_HEADS + lag):
            if t + lead < SB_HEADS:
                scores_head(t + lead, q0, k0)
            if t < SB_HEADS:
                logs_head(t, masked)
                suffix_head(t)
            if t >= lag:
                weights_head(t - lag, masked)
                values_head(t - lag, kj)

    def q_body(qi, carry):
        q0 = pl.multiple_of(qi * TQ, TQ)
        r_ref[...] = jnp.zeros(r_ref.shape, F32)
        acc_ref[...] = jnp.zeros(acc_ref.shape, F32)
        step(q0, qi, True)

        def cond(c):
            t, live = c
            return jnp.logical_and(t < qi, live > 0)

        def body(c):
            t, _ = c
            step(q0, qi - 1 - t, False)
            return t + 1, (jnp.max(r_ref[...]) > -SB_UNDERFLOW_LOG2).astype(jnp.int32)

        lax.while_loop(cond, body, (jnp.int32(0), jnp.int32(1)))
        for j in range(SB_HEADS // 2):
            ot = acc_ref[pl.ds(j * LANES, LANES), :]
            o_ref[pl.ds(q0, TQ), j * LANES:(j + 1) * LANES] = ot.T.astype(BF16)
        return carry

    lax.fori_loop(0, S // TQ, q_body, 0)


def _diff_attn_kernel(lam_init, q_ref, k_ref, vt_ref, bias_ref, lamp_ref, g_ref, o_ref,
                      m_ref, alpha_ref, acc_ref, mc_ref, s_ref, p_ref):
    S = q_ref.shape[0]
    mask = _chunk_mask_t()
    mask2 = jnp.concatenate([mask, mask], axis=1)
    lane = lax.broadcasted_iota(jnp.int32, (TQ, LANES), 1)
    first = lane < DIFF_HEAD_DIM
    lp = lamp_ref[...]
    lam = (jnp.exp(jnp.sum(lp[0:1] * lp[1:2], axis=-1, keepdims=True))
           - jnp.exp(jnp.sum(lp[2:3] * lp[3:4], axis=-1, keepdims=True)) + lam_init)
    krows = [pl.ds(h * TK, TK) for h in range(DIFF_HEADS)]
    vrows = [pl.ds(h * LANES, LANES) for h in range(DIFF_HEADS)]
    arows = [pl.ds(h * (LANES + ONES_ROWS), LANES + ONES_ROWS) for h in range(DIFF_HEADS)]
    FAR, NEAR, DIAG = 0, 1, 2

    def scores_head(h, qi, kj, slot, kind):
        q0 = _tile_start(qi, TQ)
        k0 = _tile_start(kj, TK)
        hl = slice(h * LANES, (h + 1) * LANES)
        q = q_ref[pl.ds(q0, TQ), hl]
        zero = jnp.zeros_like(q)
        q2 = jnp.concatenate([jnp.where(first, q, zero), jnp.where(first, zero, q)], axis=0)
        s = _dot_nt(k_ref[pl.ds(k0, TK), hl], q2)
        if kind != FAR:
            s = s + _rep(bias_ref[h, kind - 1], 2)
        if kind == DIAG:
            s = jnp.where(mask2, s, NEG_INF)
        s_ref[slot, krows[h], :] = s
        mc_ref[slot, h] = jnp.max(s, axis=0, keepdims=True)

    def scores(qi, kj, slot, kind):
        for h in range(DIFF_HEADS):
            scores_head(h, qi, kj, slot, kind)

    def values_head(h, kj):
        acc_ref[arows[h], :] = (alpha_ref[h] * acc_ref[arows[h], :]
                                + _dot(_values_lhs(vt_ref[kj, vrows[h], :]), p_ref[krows[h], :]))

    def step(qi, kj, nqi, nkj, slot, next_kind):
        for h in range(SKEW):
            scores_head(h, nqi, nkj, 1 - slot, next_kind)
        for h in range(DIFF_HEADS):
            if h + SKEW < DIFF_HEADS:
                scores_head(h + SKEW, nqi, nkj, 1 - slot, next_kind)
            _softmax_step(s_ref, mc_ref, m_ref, alpha_ref, p_ref, slot, h, krows[h])
            if h > 0:
                values_head(h - 1, kj)
        values_head(DIFF_HEADS - 1, kj)

    def finish(qi):
        q0 = pl.multiple_of(qi * TQ, TQ)
        for h in range(DIFF_HEADS):
            base = h * (LANES + ONES_ROWS)
            on = acc_ref[pl.ds(base, LANES), :] * (1.0 / acc_ref[pl.ds(base + LANES, 1), :])
            ot = on[:, :TQ] - lam * on[:, TQ:]
            ot = ot * lax.rsqrt(jnp.mean(ot * ot, axis=0, keepdims=True) + SUBLN_EPS)
            o_ref[pl.ds(q0, TQ), h * LANES:(h + 1) * LANES] = (ot.T * g_ref[...] * (1.0 - lam_init)).astype(BF16)
        _softmax_init(m_ref, acc_ref)

    def variant(slot, next_kind, cur_diag):
        def run(qi, kj, nqi, nkj):
            step(qi, kj, nqi, nkj, slot, next_kind)
            if cur_diag:
                finish(qi)
        return run

    variants = [[variant(slot, FAR, False), variant(slot, NEAR, False), variant(slot, DIAG, False),
                 variant(slot, FAR, True), variant(slot, NEAR, True)] for slot in range(2)]

    def body(slot, qi, kj, nqi, nkj):
        next_near = (nkj == nqi - 1).astype(jnp.int32)
        idx = jnp.where(kj == qi, 3 + next_near, jnp.where(kj == qi - 1, 2, next_near))
        lax.switch(idx, variants[slot], qi, kj, nqi, nkj)

    _softmax_init(m_ref, acc_ref)
    scores(0, 0, 0, DIAG)
    _tile_walk(S // TQ, body)


def _bias_kernel(idx_ref, tab_ref, o_ref):
    h = pl.program_id(0)
    far = tab_ref[REL_BUCKETS // 2 - 1, h]
    for t in range(2):
        idx = idx_ref[t]
        acc = jnp.zeros(idx.shape, F32)
        for b in range(REL_BUCKETS):
            acc = jnp.where(idx == b, (tab_ref[b, h] - far) * LOG2E, acc)
        o_ref[0, t] = acc


def _t5_bucket_np(rel):
    nb = REL_BUCKETS // 2
    max_exact = nb // 2
    ret = np.where(rel > 0, nb, 0)
    n = np.abs(rel)
    nf = np.maximum(n, 1).astype(np.float32)
    large = max_exact + (np.log(nf / np.float32(max_exact)) / np.float32(math.log(REL_MAX_DIST / max_exact))
                         * np.float32(nb - max_exact)).astype(np.int32)
    large = np.minimum(large, nb - 1)
    return (ret + np.where(n < max_exact, n, large)).astype(np.int32)


def _bias_tiles(rel_table):
    assert TK >= REL_MAX_DIST
    rel = np.stack([(np.arange(TK)[:, None] + (t - 1) * TK) - np.arange(TQ)[None, :] for t in range(2)])
    idx = jnp.asarray(_t5_bucket_np(rel))
    return pl.pallas_call(
        _bias_kernel,
        name="rel_bias",
        grid=(DIFF_HEADS,),
        in_specs=[pl.BlockSpec((2, TK, TQ), lambda h: (0, 0, 0)),
                  pl.BlockSpec(memory_space=pltpu.SMEM)],
        out_specs=pl.BlockSpec((1, 2, TK, TQ), lambda h: (h, 0, 0, 0)),
        out_shape=jax.ShapeDtypeStruct((DIFF_HEADS, 2, TK, TQ), F32),
        compiler_params=_params(1),
    )(idx, rel_table.astype(F32))


def _attention_call(kernel_fn, name, q, k, vt, extra, out_width, scratch, B, S):
    T = q.shape[0]
    seq = lambda a: pl.BlockSpec((S, a.shape[1]), lambda b: (b, 0))
    const = lambda a: pl.BlockSpec(a.shape, lambda b: (0,) * a.ndim, pipeline_mode=pl.Buffered(1))
    return pl.pallas_call(
        kernel_fn,
        name=name,
        grid=(B,),
        in_specs=[seq(q), seq(k), pl.BlockSpec((S // TK,) + vt.shape[1:], lambda b: (b, 0, 0))]
        + [const(a) for a in extra],
        out_specs=pl.BlockSpec((S, out_width), lambda b: (b, 0)),
        out_shape=jax.ShapeDtypeStruct((T, out_width), BF16),
        scratch_shapes=scratch,
        compiler_params=_params(1),
    )(q, k, vt, *extra)


def _softmax_scratch(heads, n, dv):
    return ([pltpu.VMEM((heads, 1, n), F32)] * 2 + [pltpu.VMEM((heads * (dv + ONES_ROWS), n), F32)]
            + [pltpu.VMEM((2, heads, 1, n), F32), pltpu.VMEM((2, heads * TK, n), F32)]
            + [pltpu.VMEM((heads * TK, n), BF16)])


def _mla_attention(qm, km, vmt, B, S):
    return _attention_call(_mla_attn_kernel, "mla_attn", qm, km, vmt, (), MLA_HEADS * MLA_V,
                           _softmax_scratch(MLA_HEADS, TQ, MLA_V), B, S)


def _sb_attention(qs, ks, vst, B, S):
    n = SB_HEADS * TK
    scratch = ([pltpu.VMEM((SB_HEADS, 1, TQ), F32)] * 2 + [pltpu.VMEM((SB_W, TQ), F32)]
               + [pltpu.VMEM((n, TQ), F32)] * 2 + [pltpu.VMEM((n, TQ), BF16)] * 2)
    return _attention_call(_sb_attn_kernel, "sb_attn", qs, ks, vst, (), SB_W, scratch, B, S)


def _diff_attention(qd, kd, vdt, bias, lamp, g, lam_init, B, S):
    return _attention_call(functools.partial(_diff_attn_kernel, lam_init), "diff_attn", qd, kd, vdt,
                           (bias, lamp, g), DIFF_W, _softmax_scratch(DIFF_HEADS, 2 * TQ, LANES), B, S)


def _out0_in1_kernel(om_ref, os_ref, z_ref, x_ref, gate_ref, wo_ref, shift_ref, scale_ref, g_ref, w1_ref, w1vt_ref,
                     x1_ref, qd_ref, kd_ref, vdt_ref, zd_ref):
    z = z_ref[...].astype(F32)
    y = jnp.concatenate([om_ref[...].astype(F32), os_ref[...].astype(F32)], axis=1) * _silu(z)
    out = _dot(y.astype(BF16), wo_ref[...])
    x1 = x_ref[...] + gate_ref[0] * out
    x1_ref[...] = x1
    h = (_rms(x1, NORM_EPS) * g_ref[...] * (1.0 + scale_ref[0]) + shift_ref[0]).astype(BF16)
    qd_ref[...] = (_dot(h, w1_ref[:, 0:DIFF_W]) * (DIFF_HEAD_DIM ** -0.5 * LOG2E)).astype(BF16)
    kd_ref[...] = _dot(h, w1_ref[:, DIFF_W:2 * DIFF_W]).astype(BF16)
    zd_ref[...] = _dot(h, w1_ref[:, 2 * DIFF_W:3 * DIFF_W]).astype(BF16)
    _store_kv_tiles(vdt_ref, _dot_nt(w1vt_ref[...], h))


def _layer0_out_layer1_in(om, osb, z, x2, mod3, wo, g1, w1, w1vt, B):
    T, D = x2.shape
    nt = T // TM
    per_b = nt // B
    full = lambda a: pl.BlockSpec(a.shape, lambda i: (0,) * a.ndim)
    tok = lambda w: pl.BlockSpec((TM, w), lambda i: (i, 0))
    mod = lambda row: pl.BlockSpec((1, 1, D), lambda i: (row * B + i // per_b, 0, 0))
    tok_out = jax.ShapeDtypeStruct((T, DIFF_W), BF16)
    return pl.pallas_call(
        _out0_in1_kernel,
        name="l0_out_l1_in",
        grid=(nt,),
        in_specs=[tok(om.shape[1]), tok(osb.shape[1]), tok(D), tok(D), mod(2), full(wo),
                  mod(3), mod(4), full(g1), full(w1), full(w1vt)],
        out_specs=[tok(D), tok(DIFF_W), tok(DIFF_W),
                   pl.BlockSpec((TM // TK, DIFF_W, TK), lambda i: (i, 0, 0)), tok(DIFF_W)],
        out_shape=[jax.ShapeDtypeStruct((T, D), F32), tok_out, tok_out,
                   jax.ShapeDtypeStruct((T // TK, DIFF_W, TK), BF16), tok_out],
        compiler_params=_params(1),
    )(om, osb, z, x2, mod3, wo, mod3, mod3, g1, w1, w1vt)


def _out1_kernel(o_ref, z_ref, x_ref, gate_ref, wo_ref, g_ref, y_ref):
    y = o_ref[...].astype(F32) * _silu(z_ref[...].astype(F32))
    out = _dot(y.astype(BF16), wo_ref[...])
    x2 = x_ref[...] + gate_ref[0] * out
    y_ref[...] = _rms(x2, NORM_EPS) * g_ref[...]


def _layer1_out(od, zd, x1, mod3, wo, gf, B):
    T, D = x1.shape
    nt = T // TM
    per_b = nt // B
    full = lambda a: pl.BlockSpec(a.shape, lambda i: (0,) * a.ndim)
    tok = lambda w: pl.BlockSpec((TM, w), lambda i: (i, 0))
    return pl.pallas_call(
        _out1_kernel,
        name="l1_out",
        grid=(nt,),
        in_specs=[tok(D), tok(D), tok(D), pl.BlockSpec((1, 1, D), lambda i: (5 * B + i // per_b, 0, 0)),
                  full(wo), full(gf)],
        out_specs=tok(D),
        out_shape=jax.ShapeDtypeStruct((T, D), F32),
        compiler_params=_params(1),
    )(od, zd, x1, mod3, wo, gf)


def kernel(x, c, pos_offset, rel_bias_table, ada_w, ada_b, norm_g, final_g, ab_w_in, ab_q_norm_g, ab_kv_norm_g,
           ab_w_uq, ab_w_ukv, ab_w_out, dif_w_in, dif_lam_q1, dif_lam_k1, dif_lam_q2, dif_lam_k2, dif_subln_g,
           dif_w_out):
    B, S, D = x.shape
    assert D == D_MODEL and S % TQ == 0 and TQ == TK and (B * S) % TM == 0 and S % TM == 0 and TM % TK == 0
    x2 = x.reshape(B * S, D)

    mod3 = _modulation(c, ada_w, ada_b)

    w0, wqa, wqb, wk, wvt, wvst = _prep_ab_weights(ab_w_in[0], ab_w_uq[0], ab_w_ukv[0])
    qm, km, vmt, qs, ks, vst, z = _layer0_in(
        x2, pos_offset, mod3, norm_g[0:1], w0, wqa, wqb, wk, wvt, wvst, ab_q_norm_g[0:1], ab_kv_norm_g[0:1], B)
    om = _mla_attention(qm, km, vmt, B, S)
    osb = _sb_attention(qs, ks, vst, B, S)

    wd = dif_w_in[0]
    w1 = jnp.concatenate([wd[:, :2 * DIFF_W], wd[:, 3 * DIFF_W:]], axis=1).astype(BF16)
    w1vt = wd[:, 2 * DIFF_W:3 * DIFF_W].T.astype(BF16)
    x1, qd, kd, vdt, zd = _layer0_out_layer1_in(
        om, osb, z, x2, mod3, ab_w_out[0].astype(BF16), norm_g[1:2], w1, w1vt, B)
    lam_init = 0.8 - 0.6 * math.exp(-0.3 * 1)
    lamp = jnp.stack([dif_lam_q1[0], dif_lam_k1[0], dif_lam_q2[0], dif_lam_k2[0]]).astype(F32)
    bias = _bias_tiles(rel_bias_table)
    od = _diff_attention(qd, kd, vdt, bias, lamp, dif_subln_g[0:1], lam_init, B, S)
    y = _layer1_out(od, zd, x1, mod3, dif_w_out[0].astype(BF16), final_g[None, :], B)
    return y.reshape(B, S, D)
```

```python
import functools
import math

import numpy as np
import jax
import jax.numpy as jnp
from jax import lax
from jax.experimental import pallas as pl
from jax.experimental.pallas import tpu as pltpu

D_MODEL = 1024
DEPTH = 2
CHUNK = 64

MLA_HEADS = 8
MLA_Q_LORA = 384
MLA_KV_LORA = 256
MLA_NOPE = 64
MLA_ROPE = 32
MLA_V = 64
SB_HEADS = 8
SB_HEAD_DIM = 64
SB_W = SB_HEADS * SB_HEAD_DIM
DIFF_HEADS = 8
DIFF_HEAD_DIM = 64
DIFF_W = DIFF_HEADS * 2 * DIFF_HEAD_DIM

REL_BUCKETS = 32
REL_MAX_DIST = 128
ROPE_THETA = 10000.0
NORM_EPS = 1e-6
SUBLN_EPS = 1e-5
NEG_INF = -1e30
SB_UNDERFLOW_LOG2 = 152.0
LOG2E = math.log2(math.e)

LANES = 128
V7X_VMEM_BYTES = 64 * 1024 * 1024
VMEM_LIMIT = V7X_VMEM_BYTES * 7 // 8

TM = 512
TQ = 256
TK = 256
SKEW = 0
ONES_ROWS = 16

F32 = jnp.float32
BF16 = jnp.bfloat16


def _silu(z):
    return z * (1.0 / (1.0 + jnp.exp(-z)))


def _dot(a, b):
    return jnp.dot(a, b, preferred_element_type=F32)


def _dot_nt(a, b):
    return lax.dot_general(a, b, (((1,), (1,)), ((), ())), preferred_element_type=F32)


def _rep(x, n):
    return x if n == 1 else jnp.concatenate([x] * n, axis=1)


def _params(n_axes=1):
    return pltpu.CompilerParams(dimension_semantics=("arbitrary",) * n_axes, vmem_limit_bytes=VMEM_LIMIT)


def _store_kv_tiles(ref, xt):
    for j in range(xt.shape[1] // TK):
        ref[j] = xt[:, j * TK:(j + 1) * TK].astype(ref.dtype)


def _mod_kernel(c_ref, w_ref, b_ref, o_ref):
    ca = _silu(c_ref[...]).astype(BF16)
    o_ref[0] = _dot(ca, w_ref[0].astype(BF16)) + b_ref[0]


def _modulation(c, ada_w, ada_b):
    B, D = c.shape
    out = pl.pallas_call(
        _mod_kernel,
        name="ada_mod",
        grid=(DEPTH, 3),
        in_specs=[
            pl.BlockSpec((B, D), lambda i, j: (0, 0)),
            pl.BlockSpec((1, D, D), lambda i, j: (i, 0, j)),
            pl.BlockSpec((1, 1, D), lambda i, j: (i * 3 + j, 0, 0)),
        ],
        out_specs=pl.BlockSpec((1, B, D), lambda i, j: (i * 3 + j, 0, 0)),
        out_shape=jax.ShapeDtypeStruct((DEPTH * 3, B, D), F32),
        compiler_params=_params(2),
    )(c, ada_w, ada_b.reshape(DEPTH * 3, 1, D))
    return out.reshape(DEPTH * 3 * B, 1, D)


def _rope_static_tables(S):
    half = MLA_ROPE // 2
    inv_freq = np.float32(ROPE_THETA) ** (-(np.arange(half, dtype=np.float32) / np.float32(half)))
    inv_freq = inv_freq.astype(np.float32)
    ang = np.arange(S, dtype=np.float64)[:, None] * inv_freq.astype(np.float64)[None, :]
    cos, sin = np.cos(ang), np.sin(ang)
    one = np.ones((S, MLA_NOPE))
    zn = np.zeros((S, MLA_NOPE))
    zp = np.zeros((S, LANES - MLA_NOPE - MLA_ROPE))
    tabs = [np.concatenate(parts, axis=1).astype(np.float32) for parts in (
        (one, cos, cos, zp), (zn, sin, sin, zp), (zn, -cos, cos, zp), (zn, -sin, sin, zp))]
    f = np.concatenate([np.zeros(MLA_NOPE, np.float32), inv_freq, inv_freq,
                        np.zeros(LANES - MLA_NOPE - MLA_ROPE, np.float32)])[None, :]
    return [jnp.asarray(t) for t in tabs], jnp.asarray(f)


def _rope_tiles(pos0, f_ref, cs_ref, ss_ref, cs2_ref, ss2_ref):
    ang0 = pos0.astype(F32) * f_ref[...]
    a, b = jnp.cos(ang0), jnp.sin(ang0)
    ctab = a * cs_ref[...] - b * ss_ref[...]
    stab = b * cs2_ref[...] + a * ss2_ref[...]
    return ctab, stab


AB_SEG = (MLA_Q_LORA, MLA_KV_LORA, SB_W, SB_W, MLA_HEADS * MLA_V + SB_W, LANES, LANES)
AB_OFF = tuple(int(v) for v in np.cumsum((0,) + AB_SEG))


def _rms(x, eps):
    return x * lax.rsqrt(jnp.mean(x * x, axis=-1, keepdims=True) + eps)


def _in0_kernel(per_b, pos_ref, x_ref, shift_ref, scale_ref, g_ref, w_ref, wqa_ref, wqb_ref, wk_ref, wvt_ref,
                wvst_ref, qg_ref, kvg_ref, f_ref, cs_ref, ss_ref, cs2_ref, ss2_ref,
                qm_ref, km_ref, vmt_ref, qs_ref, ks_ref, vst_ref, z_ref):
    x = x_ref[...]
    h = _rms(x, NORM_EPS) * g_ref[...] * (1.0 + scale_ref[0]) + shift_ref[0]
    h = h.astype(BF16)

    def seg(i):
        return _dot(h, w_ref[:, AB_OFF[i]:AB_OFF[i + 1]])

    ctab, stab = _rope_tiles(pos_ref[pl.program_id(0) // per_b], f_ref, cs_ref, ss_ref, cs2_ref, ss2_ref)
    nh = MLA_HEADS

    cq = (_rms(seg(0), NORM_EPS) * qg_ref[...]).astype(BF16)
    q = _dot(cq, wqa_ref[...]) * _rep(ctab, nh) + _dot(cq, wqb_ref[...]) * _rep(stab, nh)
    qm_ref[...] = (q * ((MLA_NOPE + MLA_ROPE) ** -0.5 * LOG2E)).astype(BF16)

    ckv = (_rms(seg(1), NORM_EPS) * kvg_ref[...]).astype(BF16)
    krope = seg(5) * ctab + seg(6) * stab
    km_ref[...] = (_dot(ckv, wk_ref[...]) + _rep(krope, nh)).astype(BF16)
    _store_kv_tiles(vmt_ref, _dot(ckv, wvt_ref[...]).T)

    qs_ref[...] = (seg(2) * (SB_HEAD_DIM ** -0.5 * LOG2E)).astype(BF16)
    ks_ref[...] = seg(3).astype(BF16)
    _store_kv_tiles(vst_ref, _dot(h, wvst_ref[...]).T)
    z_ref[...] = seg(4).astype(BF16)


def _prep_ab_weights(w_in, w_uq, w_ukv):
    D = w_in.shape[0]
    c = np.cumsum([MLA_Q_LORA, MLA_KV_LORA, MLA_ROPE, SB_W, SB_W, SB_W]).tolist()
    half = MLA_ROPE // 2
    w_kr = w_in[:, c[1]:c[2]]
    zl = jnp.zeros((D, MLA_NOPE), w_in.dtype)
    zr = jnp.zeros((D, LANES - MLA_NOPE - MLA_ROPE), w_in.dtype)
    kr_a = jnp.concatenate([zl, w_kr, zr], axis=1)
    kr_b = jnp.concatenate([zl, w_kr[:, half:], w_kr[:, :half], zr], axis=1)
    w0 = jnp.concatenate([
        w_in[:, :c[1]],
        w_in[:, c[2]:c[3]],
        w_in[:, c[3]:c[4]],
        w_in[:, c[5]:],
        kr_a, kr_b], axis=1).astype(BF16)
    wvst = w_in[:, c[4]:c[5]].astype(BF16)

    hq = MLA_NOPE + MLA_ROPE
    uq = w_uq.reshape(MLA_Q_LORA, MLA_HEADS, hq)
    pad = jnp.zeros((MLA_Q_LORA, MLA_HEADS, LANES - hq), w_uq.dtype)
    zq = jnp.zeros((MLA_Q_LORA, MLA_HEADS, MLA_NOPE), w_uq.dtype)
    wqa = jnp.concatenate([uq, pad], axis=2).reshape(MLA_Q_LORA, MLA_HEADS * LANES).astype(BF16)
    wqb = jnp.concatenate([zq, uq[:, :, MLA_NOPE + half:], uq[:, :, MLA_NOPE:MLA_NOPE + half], pad],
                          axis=2).reshape(MLA_Q_LORA, MLA_HEADS * LANES).astype(BF16)
    ukv = w_ukv.reshape(MLA_KV_LORA, MLA_HEADS, MLA_NOPE + MLA_V)
    zk = jnp.zeros((MLA_KV_LORA, MLA_HEADS, LANES - MLA_NOPE), w_ukv.dtype)
    wk = jnp.concatenate([ukv[:, :, :MLA_NOPE], zk], axis=2).reshape(MLA_KV_LORA, MLA_HEADS * LANES).astype(BF16)
    wvt = ukv[:, :, MLA_NOPE:].reshape(MLA_KV_LORA, MLA_HEADS * MLA_V).astype(BF16)
    return w0, wqa, wqb, wk, wvt, wvst


def _layer0_in(x2, pos_offset, mod3, g, w0, wqa, wqb, wk, wvt, wvst, qg, kvg, B):
    T, D = x2.shape
    nt = T // TM
    per_b = nt // B
    rope_tabs, rope_f = _rope_static_tables(T // B)
    full = lambda a: pl.BlockSpec(a.shape, lambda i: (0,) * a.ndim)
    tok = lambda w: pl.BlockSpec((TM, w), lambda i: (i, 0))
    seq = pl.BlockSpec((TM, LANES), lambda i: (i % per_b, 0))
    kvt = lambda w: pl.BlockSpec((TM // TK, w, TK), lambda i: (i, 0, 0))
    tok_out = lambda w: jax.ShapeDtypeStruct((T, w), BF16)
    kvt_out = lambda w: jax.ShapeDtypeStruct((T // TK, w, TK), BF16)
    hv = MLA_HEADS * MLA_V
    return pl.pallas_call(
        functools.partial(_in0_kernel, per_b),
        name="l0_in",
        grid=(nt,),
        in_specs=[
            pl.BlockSpec(memory_space=pltpu.SMEM),
            tok(D),
            pl.BlockSpec((1, 1, D), lambda i: (0 * B + i // per_b, 0, 0)),
            pl.BlockSpec((1, 1, D), lambda i: (1 * B + i // per_b, 0, 0)),
            full(g), full(w0), full(wqa), full(wqb), full(wk), full(wvt), full(wvst), full(qg), full(kvg),
            full(rope_f), seq, seq, seq, seq,
        ],
        out_specs=[tok(MLA_HEADS * LANES), tok(MLA_HEADS * LANES), kvt(hv), tok(SB_W), tok(SB_W), kvt(SB_W),
                   tok(hv + SB_W)],
        out_shape=[tok_out(MLA_HEADS * LANES), tok_out(MLA_HEADS * LANES), kvt_out(hv), tok_out(SB_W),
                   tok_out(SB_W), kvt_out(SB_W), tok_out(hv + SB_W)],
        compiler_params=_params(1),
    )(pos_offset, x2, mod3, mod3, g, w0, wqa, wqb, wk, wvt, wvst, qg, kvg, rope_f, *rope_tabs)


def _chunk_mask_t():
    k = lax.broadcasted_iota(jnp.int32, (TK, TQ), 0)
    q = lax.broadcasted_iota(jnp.int32, (TK, TQ), 1)
    shift = CHUNK.bit_length() - 1
    return (k >> shift) <= (q >> shift)


def _tile_start(i, size):
    return i * size if isinstance(i, int) else pl.multiple_of(i * size, size)


FAR, NEAR, DIAG = 0, 1, 2
PROGRAM_ROW = 5


def _tile_kind(qi, kj):
    return DIAG if kj == qi else NEAR if kj == qi - 1 else FAR


def _tile_program(n_tiles, kind_of):
    items = [(qi, kj) for qi in range(n_tiles) for kj in range(qi + 1)]
    assert len(items) % 2 == 0 and n_tiles >= 2
    items.append((n_tiles - 1, 0))
    kinds = [kind_of(qi, kj) for qi, kj in items]
    keys, rows = [], []
    for t in range(len(items) - 1):
        key = (kinds[t], kinds[t + 1])
        if key not in keys:
            keys.append(key)
        rows.append([keys.index(key), *items[t], *items[t + 1]])
    return np.asarray(rows, np.int32).reshape(-1), keys


def _run_tile_program(tab_ref, keys, step, finish):
    def variant(key, slot):
        def run(qi, kj, nqi, nkj):
            step(qi, kj, nqi, nkj, slot, key[1])
            if key[0] == DIAG:
                finish(qi)
        return run

    variants = [[variant(key, slot) for key in keys] for slot in range(2)]

    def trip(t, carry):
        for slot in range(2):
            base = (2 * t + slot) * PROGRAM_ROW
            lax.switch(tab_ref[base], variants[slot], *[tab_ref[base + 1 + j] for j in range(PROGRAM_ROW - 1)])
        return carry

    lax.fori_loop(0, tab_ref.shape[0] // (2 * PROGRAM_ROW), trip, 0)


def _softmax_step(s_ref, mc_ref, m_ref, alpha_ref, p_ref, slot, h, rows):
    m_prev = m_ref[h]
    m_new = jnp.maximum(m_prev, mc_ref[slot, h])
    p_ref[rows, :] = jnp.exp2(s_ref[slot, rows, :] - m_new).astype(BF16)
    alpha_ref[h] = jnp.exp2(m_prev - m_new)
    m_ref[h] = m_new


def _values_lhs(vt):
    return jnp.concatenate([vt, jnp.ones((ONES_ROWS, vt.shape[1]), vt.dtype)], axis=0)


def _softmax_init(m_ref, acc_ref):
    m_ref[...] = jnp.full(m_ref.shape, -jnp.inf, F32)
    acc_ref[...] = jnp.zeros(acc_ref.shape, F32)


def _mla_attn_kernel(keys, tab_ref, q_ref, k_ref, vt_ref, o_ref, m_ref, alpha_ref, acc_ref, mc_ref, s_ref, p_ref):
    S = q_ref.shape[0]
    mask = _chunk_mask_t()
    krows = [pl.ds(h * TK, TK) for h in range(MLA_HEADS)]
    vrows = [pl.ds(h * MLA_V, MLA_V) for h in range(MLA_HEADS)]
    arows = [pl.ds(h * (MLA_V + ONES_ROWS), MLA_V + ONES_ROWS) for h in range(MLA_HEADS)]

    def out_t(h):
        base = h * (MLA_V + ONES_ROWS)
        return acc_ref[pl.ds(base, MLA_V), :] * (1.0 / acc_ref[pl.ds(base + MLA_V, 1), :])

    def scores_head(h, qi, kj, slot, masked):
        q0 = _tile_start(qi, TQ)
        k0 = _tile_start(kj, TK)
        hl = slice(h * LANES, (h + 1) * LANES)
        s = _dot_nt(k_ref[pl.ds(k0, TK), hl], q_ref[pl.ds(q0, TQ), hl])
        if masked:
            s = jnp.where(mask, s, NEG_INF)
        s_ref[slot, krows[h], :] = s
        mc_ref[slot, h] = jnp.max(s, axis=0, keepdims=True)

    def scores(qi, kj, slot, masked):
        for h in range(MLA_HEADS):
            scores_head(h, qi, kj, slot, masked)

    def values_head(h, kj):
        acc_ref[arows[h], :] = (alpha_ref[h] * acc_ref[arows[h], :]
                                + _dot(_values_lhs(vt_ref[kj, vrows[h], :]), p_ref[krows[h], :]))

    def step(qi, kj, nqi, nkj, slot, next_masked):
        for h in range(SKEW):
            scores_head(h, nqi, nkj, 1 - slot, next_masked)
        for h in range(MLA_HEADS):
            if h + SKEW < MLA_HEADS:
                scores_head(h + SKEW, nqi, nkj, 1 - slot, next_masked)
            _softmax_step(s_ref, mc_ref, m_ref, alpha_ref, p_ref, slot, h, krows[h])
            if h > 0:
                values_head(h - 1, kj)
        values_head(MLA_HEADS - 1, kj)

    def finish(qi):
        q0 = pl.multiple_of(qi * TQ, TQ)
        for j in range(MLA_HEADS // 2):
            ot = jnp.concatenate([out_t(2 * j), out_t(2 * j + 1)], axis=0)
            o_ref[pl.ds(q0, TQ), j * LANES:(j + 1) * LANES] = ot.T.astype(BF16)
        _softmax_init(m_ref, acc_ref)

    _softmax_init(m_ref, acc_ref)
    scores(0, 0, 0, True)
    _run_tile_program(tab_ref, keys, lambda qi, kj, nqi, nkj, slot, kind: step(qi, kj, nqi, nkj, slot, kind == DIAG),
                      finish)


def _sb_attn_kernel(q_ref, k_ref, vt_ref, o_ref, r_ref, rs_ref, acc_ref, z_ref, ls_ref, hi_ref, lo_ref):
    S = q_ref.shape[0]
    k_i = lax.broadcasted_iota(jnp.int32, (TK, TQ), 0)
    q_i = lax.broadcasted_iota(jnp.int32, (TK, TQ), 1)
    causal = k_i < q_i
    tri = jnp.where(lax.broadcasted_iota(jnp.int32, (TK, TK), 1) > lax.broadcasted_iota(jnp.int32, (TK, TK), 0),
                    1.0, 0.0).astype(BF16)
    lane = lax.broadcasted_iota(jnp.int32, (TQ, LANES), 1)
    first = lane < SB_HEAD_DIM
    krows = [pl.ds(h * TK, TK) for h in range(SB_HEADS)]
    vrows = [pl.ds(h * SB_HEAD_DIM, SB_HEAD_DIM) for h in range(SB_HEADS)]
    pairs = [slice((h // 2) * LANES, (h // 2 + 1) * LANES) for h in range(SB_HEADS)]

    def scores_head(h, q0, k0):
        q = q_ref[pl.ds(q0, TQ), pairs[h]]
        q = jnp.where(first, q, jnp.zeros_like(q)) if h % 2 == 0 else jnp.where(first, jnp.zeros_like(q), q)
        z_ref[krows[h], :] = _dot_nt(k_ref[pl.ds(k0, TK), pairs[h]], q)

    def logs_head(h, masked):
        z = z_ref[krows[h], :]
        sp = jnp.log(1.0 + jnp.exp2(-jnp.abs(z))) * LOG2E
        ls = jnp.minimum(z, 0.0) - sp
        l1m = ls - z
        if masked:
            l1m = jnp.where(causal, l1m, 0.0)
        hi = l1m.astype(BF16)
        ls_ref[krows[h], :] = ls
        hi_ref[krows[h], :] = hi
        lo_ref[krows[h], :] = (l1m - hi.astype(F32)).astype(BF16)
        rs_ref[h] = l1m[0:1, :]

    def suffix_head(h):
        z_ref[krows[h], :] = _dot(tri, hi_ref[krows[h], :]) + _dot(tri, lo_ref[krows[h], :])

    def weights_head(h, masked):
        r_prev = r_ref[h]
        suffix = z_ref[krows[h], :]
        w = jnp.exp2(ls_ref[krows[h], :] + suffix + r_prev)
        if masked:
            w = jnp.where(causal, w, 0.0)
        hi_ref[krows[h], :] = w.astype(BF16)
        r_ref[h] = r_prev + (rs_ref[h] + suffix[0:1, :])

    def values_head(h, kj):
        acc_ref[vrows[h], :] += _dot(vt_ref[kj, vrows[h], :], hi_ref[krows[h], :])

    def step(q0, kj, masked):
        k0 = pl.multiple_of(kj * TK, TK)
        lead, lag = 2, 2
        for h in range(lead):
            scores_head(h, q0, k0)
        for t in range(SB_HEADS + lag):
            if t + lead < SB_HEADS:
                scores_head(t + lead, q0, k0)
            if t < SB_HEADS:
                logs_head(t, masked)
                suffix_head(t)
            if t >= lag:
                weights_head(t - lag, masked)
                values_head(t - lag, kj)

    def q_body(qi, carry):
        q0 = pl.multiple_of(qi * TQ, TQ)
        r_ref[...] = jnp.zeros(r_ref.shape, F32)
        acc_ref[...] = jnp.zeros(acc_ref.shape, F32)
        step(q0, qi, True)

        def cond(c):
            t, live = c
            return jnp.logical_and(t < qi, live > 0)

        def body(c):
            t, _ = c
            step(q0, qi - 1 - t, False)
            return t + 1, (jnp.max(r_ref[...]) > -SB_UNDERFLOW_LOG2).astype(jnp.int32)

        lax.while_loop(cond, body, (jnp.int32(0), jnp.int32(1)))
        for j in range(SB_HEADS // 2):
            ot = acc_ref[pl.ds(j * LANES, LANES), :]
            o_ref[pl.ds(q0, TQ), j * LANES:(j + 1) * LANES] = ot.T.astype(BF16)
        return carry

    lax.fori_loop(0, S // TQ, q_body, 0)


def _diff_attn_kernel(lam_init, keys, tab_ref, q_ref, k_ref, vt_ref, bias_ref, lamp_ref, g_ref, o_ref,
                      m_ref, alpha_ref, acc_ref, mc_ref, s_ref, p_ref):
    S = q_ref.shape[0]
    mask = _chunk_mask_t()
    mask2 = jnp.concatenate([mask, mask], axis=1)
    lane = lax.broadcasted_iota(jnp.int32, (TQ, LANES), 1)
    first = lane < DIFF_HEAD_DIM
    lp = lamp_ref[...]
    lam = (jnp.exp(jnp.sum(lp[0:1] * lp[1:2], axis=-1, keepdims=True))
           - jnp.exp(jnp.sum(lp[2:3] * lp[3:4], axis=-1, keepdims=True)) + lam_init)
    krows = [pl.ds(h * TK, TK) for h in range(DIFF_HEADS)]
    vrows = [pl.ds(h * LANES, LANES) for h in range(DIFF_HEADS)]
    arows = [pl.ds(h * (LANES + ONES_ROWS), LANES + ONES_ROWS) for h in range(DIFF_HEADS)]
    gain = g_ref[...] * (1.0 - lam_init)

    def scores_head(h, qi, kj, slot, kind):
        q0 = _tile_start(qi, TQ)
        k0 = _tile_start(kj, TK)
        hl = slice(h * LANES, (h + 1) * LANES)
        q = q_ref[pl.ds(q0, TQ), hl]
        zero = jnp.zeros_like(q)
        q2 = jnp.concatenate([jnp.where(first, q, zero), jnp.where(first, zero, q)], axis=0)
        s = _dot_nt(k_ref[pl.ds(k0, TK), hl], q2)
        if kind != FAR:
            s = s + _rep(bias_ref[h, kind - 1], 2)
        if kind == DIAG:
            s = jnp.where(mask2, s, NEG_INF)
        s_ref[slot, krows[h], :] = s
        mc_ref[slot, h] = jnp.max(s, axis=0, keepdims=True)

    def scores(qi, kj, slot, kind):
        for h in range(DIFF_HEADS):
            scores_head(h, qi, kj, slot, kind)

    def values_head(h, kj):
        acc_ref[arows[h], :] = (alpha_ref[h] * acc_ref[arows[h], :]
                                + _dot(_values_lhs(vt_ref[kj, vrows[h], :]), p_ref[krows[h], :]))

    def step(qi, kj, nqi, nkj, slot, next_kind):
        for h in range(SKEW):
            scores_head(h, nqi, nkj, 1 - slot, next_kind)
        for h in range(DIFF_HEADS):
            if h + SKEW < DIFF_HEADS:
                scores_head(h + SKEW, nqi, nkj, 1 - slot, next_kind)
            _softmax_step(s_ref, mc_ref, m_ref, alpha_ref, p_ref, slot, h, krows[h])
            if h > 0:
                values_head(h - 1, kj)
        values_head(DIFF_HEADS - 1, kj)

    def finish(qi):
        q0 = pl.multiple_of(qi * TQ, TQ)
        for h in range(DIFF_HEADS):
            base = h * (LANES + ONES_ROWS)
            inv_l = 1.0 / acc_ref[pl.ds(base + LANES, 1), :]
            ot = (acc_ref[pl.ds(base, LANES), pl.ds(0, TQ)] * inv_l[:, :TQ]
                  - acc_ref[pl.ds(base, LANES), pl.ds(TQ, TQ)] * (lam * inv_l[:, TQ:]))
            ot = ot * lax.rsqrt(jnp.mean(ot * ot, axis=0, keepdims=True) + SUBLN_EPS)
            o_ref[pl.ds(q0, TQ), h * LANES:(h + 1) * LANES] = (ot.T * gain).astype(BF16)
        _softmax_init(m_ref, acc_ref)

    _softmax_init(m_ref, acc_ref)
    scores(0, 0, 0, DIAG)
    _run_tile_program(tab_ref, keys, step, finish)


def _bias_kernel(idx_ref, tab_ref, o_ref):
    h = pl.program_id(0)
    far = tab_ref[REL_BUCKETS // 2 - 1, h]
    for t in range(2):
        idx = idx_ref[t]
        acc = jnp.zeros(idx.shape, F32)
        for b in range(REL_BUCKETS):
            acc = jnp.where(idx == b, (tab_ref[b, h] - far) * LOG2E, acc)
        o_ref[0, t] = acc


def _t5_bucket_np(rel):
    nb = REL_BUCKETS // 2
    max_exact = nb // 2
    ret = np.where(rel > 0, nb, 0)
    n = np.abs(rel)
    nf = np.maximum(n, 1).astype(np.float32)
    large = max_exact + (np.log(nf / np.float32(max_exact)) / np.float32(math.log(REL_MAX_DIST / max_exact))
                         * np.float32(nb - max_exact)).astype(np.int32)
    large = np.minimum(large, nb - 1)
    return (ret + np.where(n < max_exact, n, large)).astype(np.int32)


def _bias_tiles(rel_table):
    assert TK >= REL_MAX_DIST
    rel = np.stack([(np.arange(TK)[:, None] + (t - 1) * TK) - np.arange(TQ)[None, :] for t in range(2)])
    idx = jnp.asarray(_t5_bucket_np(rel))
    return pl.pallas_call(
        _bias_kernel,
        name="rel_bias",
        grid=(DIFF_HEADS,),
        in_specs=[pl.BlockSpec((2, TK, TQ), lambda h: (0, 0, 0)),
                  pl.BlockSpec(memory_space=pltpu.SMEM)],
        out_specs=pl.BlockSpec((1, 2, TK, TQ), lambda h: (h, 0, 0, 0)),
        out_shape=jax.ShapeDtypeStruct((DIFF_HEADS, 2, TK, TQ), F32),
        compiler_params=_params(1),
    )(idx, rel_table.astype(F32))


def _attention_call(kernel_fn, name, q, k, vt, extra, out_width, scratch, B, S, program=None):
    T = q.shape[0]
    smem = [] if program is None else [jnp.asarray(program)]
    seq = lambda a: pl.BlockSpec((S, a.shape[1]), lambda b: (b, 0))
    const = lambda a: pl.BlockSpec(a.shape, lambda b: (0,) * a.ndim, pipeline_mode=pl.Buffered(1))
    return pl.pallas_call(
        kernel_fn,
        name=name,
        grid=(B,),
        in_specs=[pl.BlockSpec(memory_space=pltpu.SMEM) for _ in smem]
        + [seq(q), seq(k), pl.BlockSpec((S // TK,) + vt.shape[1:], lambda b: (b, 0, 0))]
        + [const(a) for a in extra],
        out_specs=pl.BlockSpec((S, out_width), lambda b: (b, 0)),
        out_shape=jax.ShapeDtypeStruct((T, out_width), BF16),
        scratch_shapes=scratch,
        compiler_params=_params(1),
    )(*smem, q, k, vt, *extra)


def _softmax_scratch(heads, n, dv):
    return ([pltpu.VMEM((heads, 1, n), F32)] * 2 + [pltpu.VMEM((heads * (dv + ONES_ROWS), n), F32)]
            + [pltpu.VMEM((2, heads, 1, n), F32), pltpu.VMEM((2, heads * TK, n), F32)]
            + [pltpu.VMEM((heads * TK, n), BF16)])


def _mla_attention(qm, km, vmt, B, S):
    table, keys = _tile_program(S // TQ, lambda qi, kj: DIAG if kj == qi else FAR)
    return _attention_call(functools.partial(_mla_attn_kernel, keys), "mla_attn", qm, km, vmt, (),
                           MLA_HEADS * MLA_V, _softmax_scratch(MLA_HEADS, TQ, MLA_V), B, S, program=table)


def _sb_attention(qs, ks, vst, B, S):
    n = SB_HEADS * TK
    scratch = ([pltpu.VMEM((SB_HEADS, 1, TQ), F32)] * 2 + [pltpu.VMEM((SB_W, TQ), F32)]
               + [pltpu.VMEM((n, TQ), F32)] * 2 + [pltpu.VMEM((n, TQ), BF16)] * 2)
    return _attention_call(_sb_attn_kernel, "sb_attn", qs, ks, vst, (), SB_W, scratch, B, S)


def _diff_attention(qd, kd, vdt, bias, lamp, g, lam_init, B, S):
    table, keys = _tile_program(S // TQ, _tile_kind)
    return _attention_call(functools.partial(_diff_attn_kernel, lam_init, keys), "diff_attn", qd, kd, vdt,
                           (bias, lamp, g), DIFF_W, _softmax_scratch(DIFF_HEADS, 2 * TQ, LANES), B, S,
                           program=table)


def _out0_in1_kernel(om_ref, os_ref, z_ref, x_ref, gate_ref, wo_ref, shift_ref, scale_ref, g_ref, w1_ref, w1vt_ref,
                     x1_ref, qd_ref, kd_ref, vdt_ref, zd_ref):
    z = z_ref[...].astype(F32)
    y = jnp.concatenate([om_ref[...].astype(F32), os_ref[...].astype(F32)], axis=1) * _silu(z)
    out = _dot(y.astype(BF16), wo_ref[...])
    x1 = x_ref[...] + gate_ref[0] * out
    x1_ref[...] = x1
    h = (_rms(x1, NORM_EPS) * g_ref[...] * (1.0 + scale_ref[0]) + shift_ref[0]).astype(BF16)
    qd_ref[...] = (_dot(h, w1_ref[:, 0:DIFF_W]) * (DIFF_HEAD_DIM ** -0.5 * LOG2E)).astype(BF16)
    kd_ref[...] = _dot(h, w1_ref[:, DIFF_W:2 * DIFF_W]).astype(BF16)
    zd_ref[...] = _dot(h, w1_ref[:, 2 * DIFF_W:3 * DIFF_W]).astype(BF16)
    _store_kv_tiles(vdt_ref, _dot(h, w1vt_ref[...]).T)


def _layer0_out_layer1_in(om, osb, z, x2, mod3, wo, g1, w1, w1vt, B):
    T, D = x2.shape
    nt = T // TM
    per_b = nt // B
    full = lambda a: pl.BlockSpec(a.shape, lambda i: (0,) * a.ndim)
    tok = lambda w: pl.BlockSpec((TM, w), lambda i: (i, 0))
    mod = lambda row: pl.BlockSpec((1, 1, D), lambda i: (row * B + i // per_b, 0, 0))
    tok_out = jax.ShapeDtypeStruct((T, DIFF_W), BF16)
    return pl.pallas_call(
        _out0_in1_kernel,
        name="l0_out_l1_in",
        grid=(nt,),
        in_specs=[tok(om.shape[1]), tok(osb.shape[1]), tok(D), tok(D), mod(2), full(wo),
                  mod(3), mod(4), full(g1), full(w1), full(w1vt)],
        out_specs=[tok(D), tok(DIFF_W), tok(DIFF_W),
                   pl.BlockSpec((TM // TK, DIFF_W, TK), lambda i: (i, 0, 0)), tok(DIFF_W)],
        out_shape=[jax.ShapeDtypeStruct((T, D), F32), tok_out, tok_out,
                   jax.ShapeDtypeStruct((T // TK, DIFF_W, TK), BF16), tok_out],
        compiler_params=_params(1),
    )(om, osb, z, x2, mod3, wo, mod3, mod3, g1, w1, w1vt)


def _out1_kernel(o_ref, z_ref, x_ref, gate_ref, wo_ref, g_ref, y_ref):
    y = o_ref[...].astype(F32) * _silu(z_ref[...].astype(F32))
    out = _dot(y.astype(BF16), wo_ref[...])
    x2 = x_ref[...] + gate_ref[0] * out
    y_ref[...] = _rms(x2, NORM_EPS) * g_ref[...]


def _layer1_out(od, zd, x1, mod3, wo, gf, B):
    T, D = x1.shape
    nt = T // TM
    per_b = nt // B
    full = lambda a: pl.BlockSpec(a.shape, lambda i: (0,) * a.ndim)
    tok = lambda w: pl.BlockSpec((TM, w), lambda i: (i, 0))
    return pl.pallas_call(
        _out1_kernel,
        name="l1_out",
        grid=(nt,),
        in_specs=[tok(D), tok(D), tok(D), pl.BlockSpec((1, 1, D), lambda i: (5 * B + i // per_b, 0, 0)),
                  full(wo), full(gf)],
        out_specs=tok(D),
        out_shape=jax.ShapeDtypeStruct((T, D), F32),
        compiler_params=_params(1),
    )(od, zd, x1, mod3, wo, gf)


def kernel(x, c, pos_offset, rel_bias_table, ada_w, ada_b, norm_g, final_g, ab_w_in, ab_q_norm_g, ab_kv_norm_g,
           ab_w_uq, ab_w_ukv, ab_w_out, dif_w_in, dif_lam_q1, dif_lam_k1, dif_lam_q2, dif_lam_k2, dif_subln_g,
           dif_w_out):
    B, S, D = x.shape
    assert D == D_MODEL and S % TQ == 0 and TQ == TK and (B * S) % TM == 0 and S % TM == 0 and TM % TK == 0
    x2 = x.reshape(B * S, D)

    mod3 = _modulation(c, ada_w, ada_b)

    w0, wqa, wqb, wk, wvt, wvst = _prep_ab_weights(ab_w_in[0], ab_w_uq[0], ab_w_ukv[0])
    qm, km, vmt, qs, ks, vst, z = _layer0_in(
        x2, pos_offset, mod3, norm_g[0:1], w0, wqa, wqb, wk, wvt, wvst, ab_q_norm_g[0:1], ab_kv_norm_g[0:1], B)
    om = _mla_attention(qm, km, vmt, B, S)
    osb = _sb_attention(qs, ks, vst, B, S)

    wd = dif_w_in[0]
    w1 = jnp.concatenate([wd[:, :2 * DIFF_W], wd[:, 3 * DIFF_W:]], axis=1).astype(BF16)
    w1vt = wd[:, 2 * DIFF_W:3 * DIFF_W].astype(BF16)
    x1, qd, kd, vdt, zd = _layer0_out_layer1_in(
        om, osb, z, x2, mod3, ab_w_out[0].astype(BF16), norm_g[1:2], w1, w1vt, B)
    lam_init = 0.8 - 0.6 * math.exp(-0.3 * 1)
    lamp = jnp.stack([dif_lam_q1[0], dif_lam_k1[0], dif_lam_q2[0], dif_lam_k2[0]]).astype(F32)
    bias = _bias_tiles(rel_bias_table)
    od = _diff_attention(qd, kd, vdt, bias, lamp, dif_subln_g[0:1], lam_init, B, S)
    y = _layer1_out(od, zd, x1, mod3, dif_w_out[0].astype(BF16), final_g[None, :], B)
    return y.reshape(B, S, D)
```

```python
import functools
import math

import numpy as np
import jax
import jax.numpy as jnp
from jax import lax
from jax.experimental import pallas as pl
from jax.experimental.pallas import tpu as pltpu

D_MODEL = 1024
DEPTH = 2
CHUNK = 64

MLA_HEADS = 8
MLA_Q_LORA = 384
MLA_KV_LORA = 256
MLA_NOPE = 64
MLA_ROPE = 32
MLA_V = 64
SB_HEADS = 8
SB_HEAD_DIM = 64
SB_W = SB_HEADS * SB_HEAD_DIM
DIFF_HEADS = 8
DIFF_HEAD_DIM = 64
DIFF_W = DIFF_HEADS * 2 * DIFF_HEAD_DIM

REL_BUCKETS = 32
REL_MAX_DIST = 128
ROPE_THETA = 10000.0
NORM_EPS = 1e-6
SUBLN_EPS = 1e-5
NEG_INF = -1e30
SB_UNDERFLOW_LOG2 = 152.0
LOG2E = math.log2(math.e)

LANES = 128
V7X_VMEM_BYTES = 64 * 1024 * 1024
VMEM_LIMIT = V7X_VMEM_BYTES * 7 // 8

TM = 512
TQ = 256
TK = 256
SKEW = 0
ONES_ROWS = 16

F32 = jnp.float32
BF16 = jnp.bfloat16


def _silu(z):
    return z * (1.0 / (1.0 + jnp.exp(-z)))


def _dot(a, b):
    return jnp.dot(a, b, preferred_element_type=F32)


def _dot_nt(a, b):
    return lax.dot_general(a, b, (((1,), (1,)), ((), ())), preferred_element_type=F32)


def _rep(x, n):
    return x if n == 1 else jnp.concatenate([x] * n, axis=1)


def _params(n_axes=1):
    return pltpu.CompilerParams(dimension_semantics=("arbitrary",) * n_axes, vmem_limit_bytes=VMEM_LIMIT)


def _store_kv_tiles(ref, xt):
    for j in range(xt.shape[1] // TK):
        ref[j] = xt[:, j * TK:(j + 1) * TK].astype(ref.dtype)


def _mod_kernel(c_ref, w_ref, b_ref, o_ref):
    ca = _silu(c_ref[...]).astype(BF16)
    o_ref[0] = _dot(ca, w_ref[0].astype(BF16)) + b_ref[0]


def _modulation(c, ada_w, ada_b):
    B, D = c.shape
    out = pl.pallas_call(
        _mod_kernel,
        name="ada_mod",
        grid=(DEPTH, 3),
        in_specs=[
            pl.BlockSpec((B, D), lambda i, j: (0, 0)),
            pl.BlockSpec((1, D, D), lambda i, j: (i, 0, j)),
            pl.BlockSpec((1, 1, D), lambda i, j: (i * 3 + j, 0, 0)),
        ],
        out_specs=pl.BlockSpec((1, B, D), lambda i, j: (i * 3 + j, 0, 0)),
        out_shape=jax.ShapeDtypeStruct((DEPTH * 3, B, D), F32),
        compiler_params=_params(2),
    )(c, ada_w, ada_b.reshape(DEPTH * 3, 1, D))
    return out.reshape(DEPTH * 3 * B, 1, D)


def _rope_static_tables(S):
    half = MLA_ROPE // 2
    inv_freq = np.float32(ROPE_THETA) ** (-(np.arange(half, dtype=np.float32) / np.float32(half)))
    inv_freq = inv_freq.astype(np.float32)
    ang = np.arange(S, dtype=np.float64)[:, None] * inv_freq.astype(np.float64)[None, :]
    cos, sin = np.cos(ang), np.sin(ang)
    one = np.ones((S, MLA_NOPE))
    zn = np.zeros((S, MLA_NOPE))
    zp = np.zeros((S, LANES - MLA_NOPE - MLA_ROPE))
    tabs = [np.concatenate(parts, axis=1).astype(np.float32) for parts in (
        (one, cos, cos, zp), (zn, sin, sin, zp), (zn, -cos, cos, zp), (zn, -sin, sin, zp))]
    f = np.concatenate([np.zeros(MLA_NOPE, np.float32), inv_freq, inv_freq,
                        np.zeros(LANES - MLA_NOPE - MLA_ROPE, np.float32)])[None, :]
    return [jnp.asarray(t) for t in tabs], jnp.asarray(f)


def _rope_tiles(pos0, f_ref, cs_ref, ss_ref, cs2_ref, ss2_ref):
    ang0 = pos0.astype(F32) * f_ref[...]
    a, b = jnp.cos(ang0), jnp.sin(ang0)
    ctab = a * cs_ref[...] - b * ss_ref[...]
    stab = b * cs2_ref[...] + a * ss2_ref[...]
    return ctab, stab


AB_SEG = (MLA_Q_LORA, MLA_KV_LORA, SB_W, SB_W, MLA_HEADS * MLA_V + SB_W, LANES)
AB_OFF = tuple(int(v) for v in np.cumsum((0,) + AB_SEG))


def _rms(x, eps):
    return x * lax.rsqrt(jnp.mean(x * x, axis=-1, keepdims=True) + eps)


def _swap_rope_halves(v):
    half = MLA_ROPE // 2
    n = v.shape[1]
    lane = lax.broadcasted_iota(jnp.int32, v.shape, 1) % LANES
    return jnp.where(lane < MLA_NOPE + half, pltpu.roll(v, n - half, 1), pltpu.roll(v, half, 1))


def _in0_kernel(per_b, pos_ref, x_ref, shift_ref, scale_ref, g_ref, w_ref, wqa_ref, wk_ref, wvt_ref,
                wvst_ref, qg_ref, kvg_ref, f_ref, cs_ref, ss_ref, cs2_ref, ss2_ref,
                qm_ref, km_ref, vmt_ref, qs_ref, ks_ref, vst_ref, z_ref):
    x = x_ref[...]
    h = _rms(x, NORM_EPS) * g_ref[...] * (1.0 + scale_ref[0]) + shift_ref[0]
    h = h.astype(BF16)

    def seg(i):
        return _dot(h, w_ref[:, AB_OFF[i]:AB_OFF[i + 1]])

    ctab, stab = _rope_tiles(pos_ref[pl.program_id(0) // per_b], f_ref, cs_ref, ss_ref, cs2_ref, ss2_ref)
    nh = MLA_HEADS

    cq = (_rms(seg(0), NORM_EPS) * qg_ref[...]).astype(BF16)
    q = _dot(cq, wqa_ref[...])
    q = q * _rep(ctab, nh) + _swap_rope_halves(q) * _rep(stab, nh)
    qm_ref[...] = (q * ((MLA_NOPE + MLA_ROPE) ** -0.5 * LOG2E)).astype(BF16)

    ckv = (_rms(seg(1), NORM_EPS) * kvg_ref[...]).astype(BF16)
    kr = seg(5)
    krope = kr * ctab + _swap_rope_halves(kr) * stab
    km_ref[...] = (_dot(ckv, wk_ref[...]) + _rep(krope, nh)).astype(BF16)
    _store_kv_tiles(vmt_ref, _dot(ckv, wvt_ref[...]).T)

    qs_ref[...] = (seg(2) * (SB_HEAD_DIM ** -0.5 * LOG2E)).astype(BF16)
    ks_ref[...] = seg(3).astype(BF16)
    _store_kv_tiles(vst_ref, _dot(h, wvst_ref[...]).T)
    z_ref[...] = seg(4).astype(BF16)


def _prep_ab_weights(w_in, w_uq, w_ukv):
    D = w_in.shape[0]
    c = np.cumsum([MLA_Q_LORA, MLA_KV_LORA, MLA_ROPE, SB_W, SB_W, SB_W]).tolist()
    w_kr = w_in[:, c[1]:c[2]]
    zl = jnp.zeros((D, MLA_NOPE), w_in.dtype)
    zr = jnp.zeros((D, LANES - MLA_NOPE - MLA_ROPE), w_in.dtype)
    kr_a = jnp.concatenate([zl, w_kr, zr], axis=1)
    w0 = jnp.concatenate([
        w_in[:, :c[1]],
        w_in[:, c[2]:c[3]],
        w_in[:, c[3]:c[4]],
        w_in[:, c[5]:],
        kr_a], axis=1).astype(BF16)
    wvst = w_in[:, c[4]:c[5]].astype(BF16)

    hq = MLA_NOPE + MLA_ROPE
    uq = w_uq.reshape(MLA_Q_LORA, MLA_HEADS, hq)
    pad = jnp.zeros((MLA_Q_LORA, MLA_HEADS, LANES - hq), w_uq.dtype)
    wqa = jnp.concatenate([uq, pad], axis=2).reshape(MLA_Q_LORA, MLA_HEADS * LANES).astype(BF16)
    ukv = w_ukv.reshape(MLA_KV_LORA, MLA_HEADS, MLA_NOPE + MLA_V)
    zk = jnp.zeros((MLA_KV_LORA, MLA_HEADS, LANES - MLA_NOPE), w_ukv.dtype)
    wk = jnp.concatenate([ukv[:, :, :MLA_NOPE], zk], axis=2).reshape(MLA_KV_LORA, MLA_HEADS * LANES).astype(BF16)
    wvt = ukv[:, :, MLA_NOPE:].reshape(MLA_KV_LORA, MLA_HEADS * MLA_V).astype(BF16)
    return w0, wqa, wk, wvt, wvst


def _layer0_in(x2, pos_offset, mod3, g, w0, wqa, wk, wvt, wvst, qg, kvg, B):
    T, D = x2.shape
    nt = T // TM
    per_b = nt // B
    rope_tabs, rope_f = _rope_static_tables(T // B)
    full = lambda a: pl.BlockSpec(a.shape, lambda i: (0,) * a.ndim)
    tok = lambda w: pl.BlockSpec((TM, w), lambda i: (i, 0))
    seq = pl.BlockSpec((TM, LANES), lambda i: (i % per_b, 0))
    kvt = lambda w: pl.BlockSpec((TM // TK, w, TK), lambda i: (i, 0, 0))
    tok_out = lambda w: jax.ShapeDtypeStruct((T, w), BF16)
    kvt_out = lambda w: jax.ShapeDtypeStruct((T // TK, w, TK), BF16)
    hv = MLA_HEADS * MLA_V
    return pl.pallas_call(
        functools.partial(_in0_kernel, per_b),
        name="l0_in",
        grid=(nt,),
        in_specs=[
            pl.BlockSpec(memory_space=pltpu.SMEM),
            tok(D),
            pl.BlockSpec((1, 1, D), lambda i: (0 * B + i // per_b, 0, 0)),
            pl.BlockSpec((1, 1, D), lambda i: (1 * B + i // per_b, 0, 0)),
            full(g), full(w0), full(wqa), full(wk), full(wvt), full(wvst), full(qg), full(kvg),
            full(rope_f), seq, seq, seq, seq,
        ],
        out_specs=[tok(MLA_HEADS * LANES), tok(MLA_HEADS * LANES), kvt(hv), tok(SB_W), tok(SB_W), kvt(SB_W),
                   tok(hv + SB_W)],
        out_shape=[tok_out(MLA_HEADS * LANES), tok_out(MLA_HEADS * LANES), kvt_out(hv), tok_out(SB_W),
                   tok_out(SB_W), kvt_out(SB_W), tok_out(hv + SB_W)],
        compiler_params=_params(1),
    )(pos_offset, x2, mod3, mod3, g, w0, wqa, wk, wvt, wvst, qg, kvg, rope_f, *rope_tabs)


def _chunk_mask_t():
    k = lax.broadcasted_iota(jnp.int32, (TK, TQ), 0)
    q = lax.broadcasted_iota(jnp.int32, (TK, TQ), 1)
    shift = CHUNK.bit_length() - 1
    return (k >> shift) <= (q >> shift)


def _tile_start(i, size):
    return i * size if isinstance(i, int) else pl.multiple_of(i * size, size)


FAR, NEAR, DIAG = 0, 1, 2
PROGRAM_ROW = 5


def _tile_kind(qi, kj):
    return DIAG if kj == qi else NEAR if kj == qi - 1 else FAR


def _tile_program(n_tiles, kind_of):
    items = [(qi, kj) for qi in range(n_tiles) for kj in range(qi + 1)]
    assert len(items) % 2 == 0 and n_tiles >= 2
    items.append((n_tiles - 1, 0))
    kinds = [kind_of(qi, kj) for qi, kj in items]
    keys, rows = [], []
    for t in range(len(items) - 1):
        key = (kinds[t], kinds[t + 1])
        if key not in keys:
            keys.append(key)
        rows.append([keys.index(key), *items[t], *items[t + 1]])
    return np.asarray(rows, np.int32).reshape(-1), keys


def _run_tile_program(tab_ref, keys, step, finish, prepare=None):
    def variant(key, slot):
        def run(qi, kj, nqi, nkj):
            if key[0] == DIAG and prepare is not None:
                prepare(nqi)
            step(qi, kj, nqi, nkj, slot, key[1])
            if key[0] == DIAG:
                finish(qi)
        return run

    variants = [[variant(key, slot) for key in keys] for slot in range(2)]

    def trip(t, carry):
        for slot in range(2):
            base = (2 * t + slot) * PROGRAM_ROW
            lax.switch(tab_ref[base], variants[slot], *[tab_ref[base + 1 + j] for j in range(PROGRAM_ROW - 1)])
        return carry

    lax.fori_loop(0, tab_ref.shape[0] // (2 * PROGRAM_ROW), trip, 0)


def _softmax_step(s_ref, mc_ref, m_ref, alpha_ref, p_ref, slot, h, rows):
    m_prev = m_ref[h]
    m_new = jnp.maximum(m_prev, mc_ref[slot, h])
    p_ref[rows, :] = jnp.exp2(s_ref[slot, rows, :] - m_new).astype(BF16)
    alpha_ref[h] = jnp.exp2(m_prev - m_new)
    m_ref[h] = m_new


def _values_lhs(vt):
    return jnp.concatenate([vt, jnp.ones((ONES_ROWS, vt.shape[1]), vt.dtype)], axis=0)


def _softmax_init(m_ref, acc_ref):
    m_ref[...] = jnp.full(m_ref.shape, -jnp.inf, F32)
    acc_ref[...] = jnp.zeros(acc_ref.shape, F32)


def _mla_attn_kernel(keys, tab_ref, q_ref, k_ref, vt_ref, o_ref, m_ref, alpha_ref, acc_ref, mc_ref, s_ref, p_ref):
    S = q_ref.shape[0]
    mask = _chunk_mask_t()
    krows = [pl.ds(h * TK, TK) for h in range(MLA_HEADS)]
    vrows = [pl.ds(h * MLA_V, MLA_V) for h in range(MLA_HEADS)]
    arows = [pl.ds(h * (MLA_V + ONES_ROWS), MLA_V + ONES_ROWS) for h in range(MLA_HEADS)]

    def out_t(h):
        base = h * (MLA_V + ONES_ROWS)
        return acc_ref[pl.ds(base, MLA_V), :] * (1.0 / acc_ref[pl.ds(base + MLA_V, 1), :])

    def scores_head(h, qi, kj, slot, masked):
        q0 = _tile_start(qi, TQ)
        k0 = _tile_start(kj, TK)
        hl = slice(h * LANES, (h + 1) * LANES)
        s = _dot_nt(k_ref[pl.ds(k0, TK), hl], q_ref[pl.ds(q0, TQ), hl])
        if masked:
            s = jnp.where(mask, s, NEG_INF)
        s_ref[slot, krows[h], :] = s
        mc_ref[slot, h] = jnp.max(s, axis=0, keepdims=True)

    def scores(qi, kj, slot, masked):
        for h in range(MLA_HEADS):
            scores_head(h, qi, kj, slot, masked)

    def values_head(h, kj):
        acc_ref[arows[h], :] = (alpha_ref[h] * acc_ref[arows[h], :]
                                + _dot(_values_lhs(vt_ref[kj, vrows[h], :]), p_ref[krows[h], :]))

    def step(qi, kj, nqi, nkj, slot, next_masked):
        for h in range(SKEW):
            scores_head(h, nqi, nkj, 1 - slot, next_masked)
        for h in range(MLA_HEADS):
            if h + SKEW < MLA_HEADS:
                scores_head(h + SKEW, nqi, nkj, 1 - slot, next_masked)
            _softmax_step(s_ref, mc_ref, m_ref, alpha_ref, p_ref, slot, h, krows[h])
            if h > 0:
                values_head(h - 1, kj)
        values_head(MLA_HEADS - 1, kj)

    def finish(qi):
        q0 = pl.multiple_of(qi * TQ, TQ)
        for j in range(MLA_HEADS // 2):
            ot = jnp.concatenate([out_t(2 * j), out_t(2 * j + 1)], axis=0)
            o_ref[pl.ds(q0, TQ), j * LANES:(j + 1) * LANES] = ot.T.astype(BF16)
        _softmax_init(m_ref, acc_ref)

    _softmax_init(m_ref, acc_ref)
    scores(0, 0, 0, True)
    _run_tile_program(tab_ref, keys, lambda qi, kj, nqi, nkj, slot, kind: step(qi, kj, nqi, nkj, slot, kind == DIAG),
                      finish)


def _sb_attn_kernel(q_ref, k_ref, vt_ref, o_ref, r_ref, rs_ref, acc_ref, z_ref, ls_ref, hi_ref, lo_ref):
    S = q_ref.shape[0]
    k_i = lax.broadcasted_iota(jnp.int32, (TK, TQ), 0)
    q_i = lax.broadcasted_iota(jnp.int32, (TK, TQ), 1)
    causal = k_i < q_i
    tri = jnp.where(lax.broadcasted_iota(jnp.int32, (TK, TK), 1) > lax.broadcasted_iota(jnp.int32, (TK, TK), 0),
                    1.0, 0.0).astype(BF16)
    lane = lax.broadcasted_iota(jnp.int32, (TQ, LANES), 1)
    first = lane < SB_HEAD_DIM
    krows = [pl.ds(h * TK, TK) for h in range(SB_HEADS)]
    vrows = [pl.ds(h * SB_HEAD_DIM, SB_HEAD_DIM) for h in range(SB_HEADS)]
    pairs = [slice((h // 2) * LANES, (h // 2 + 1) * LANES) for h in range(SB_HEADS)]

    def scores_head(h, q0, k0):
        q = q_ref[pl.ds(q0, TQ), pairs[h]]
        q = jnp.where(first, q, jnp.zeros_like(q)) if h % 2 == 0 else jnp.where(first, jnp.zeros_like(q), q)
        z_ref[krows[h], :] = _dot_nt(k_ref[pl.ds(k0, TK), pairs[h]], q)

    def logs_head(h, masked):
        z = z_ref[krows[h], :]
        sp = jnp.log(1.0 + jnp.exp2(-jnp.abs(z))) * LOG2E
        ls = jnp.minimum(z, 0.0) - sp
        l1m = ls - z
        if masked:
            l1m = jnp.where(causal, l1m, 0.0)
        hi = l1m.astype(BF16)
        ls_ref[krows[h], :] = ls
        hi_ref[krows[h], :] = hi
        lo_ref[krows[h], :] = (l1m - hi.astype(F32)).astype(BF16)
        rs_ref[h] = l1m[0:1, :]

    def suffix_head(h):
        z_ref[krows[h], :] = _dot(tri, hi_ref[krows[h], :]) + _dot(tri, lo_ref[krows[h], :])

    def weights_head(h, masked):
        r_prev = r_ref[h]
        suffix = z_ref[krows[h], :]
        w = jnp.exp2(ls_ref[krows[h], :] + suffix + r_prev)
        if masked:
            w = jnp.where(causal, w, 0.0)
        hi_ref[krows[h], :] = w.astype(BF16)
        r_ref[h] = r_prev + (rs_ref[h] + suffix[0:1, :])

    def values_head(h, kj):
        acc_ref[vrows[h], :] += _dot(vt_ref[kj, vrows[h], :], hi_ref[krows[h], :])

    def step(q0, kj, masked):
        k0 = pl.multiple_of(kj * TK, TK)
        lead, lag = 2, 2
        for h in range(lead):
            scores_head(h, q0, k0)
        for t in range(SB_HEADS + lag):
            if t + lead < SB_HEADS:
                scores_head(t + lead, q0, k0)
            if t < SB_HEADS:
                logs_head(t, masked)
                suffix_head(t)
            if t >= lag:
                weights_head(t - lag, masked)
                values_head(t - lag, kj)

    def q_body(qi, carry):
        q0 = pl.multiple_of(qi * TQ, TQ)
        r_ref[...] = jnp.zeros(r_ref.shape, F32)
        acc_ref[...] = jnp.zeros(acc_ref.shape, F32)
        step(q0, qi, True)

        def cond(c):
            t, live = c
            return jnp.logical_and(t < qi, live > 0)

        def body(c):
            t, _ = c
            step(q0, qi - 1 - t, False)
            return t + 1, (jnp.max(r_ref[...]) > -SB_UNDERFLOW_LOG2).astype(jnp.int32)

        lax.while_loop(cond, body, (jnp.int32(0), jnp.int32(1)))
        for j in range(SB_HEADS // 2):
            ot = acc_ref[pl.ds(j * LANES, LANES), :]
            o_ref[pl.ds(q0, TQ), j * LANES:(j + 1) * LANES] = ot.T.astype(BF16)
        return carry

    lax.fori_loop(0, S // TQ, q_body, 0)


def _diff_attn_kernel(lam_init, keys, tab_ref, q_ref, k_ref, vt_ref, bias_ref, lamp_ref, g_ref, o_ref,
                      m_ref, alpha_ref, acc_ref, mc_ref, s_ref, p_ref, q2_ref):
    S = q_ref.shape[0]
    mask = _chunk_mask_t()
    mask2 = jnp.concatenate([mask, mask], axis=1)
    lane = lax.broadcasted_iota(jnp.int32, (TQ, LANES), 1)
    first = lane < DIFF_HEAD_DIM
    lp = lamp_ref[...]
    lam = (jnp.exp(jnp.sum(lp[0:1] * lp[1:2], axis=-1, keepdims=True))
           - jnp.exp(jnp.sum(lp[2:3] * lp[3:4], axis=-1, keepdims=True)) + lam_init)
    krows = [pl.ds(h * TK, TK) for h in range(DIFF_HEADS)]
    vrows = [pl.ds(h * LANES, LANES) for h in range(DIFF_HEADS)]
    arows = [pl.ds(h * (LANES + ONES_ROWS), LANES + ONES_ROWS) for h in range(DIFF_HEADS)]
    gain = g_ref[...] * (1.0 - lam_init)

    def scores_head(h, qi, kj, slot, kind):
        q0 = _tile_start(qi, TQ)
        k0 = _tile_start(kj, TK)
        hl = slice(h * LANES, (h + 1) * LANES)
        s = _dot_nt(k_ref[pl.ds(k0, TK), hl], q2_ref[pl.ds(h * 2 * TQ, 2 * TQ), :])
        if kind != FAR:
            s = s + _rep(bias_ref[h, kind - 1], 2)
        if kind == DIAG:
            s = jnp.where(mask2, s, NEG_INF)
        s_ref[slot, krows[h], :] = s
        mc_ref[slot, h] = jnp.max(s, axis=0, keepdims=True)

    def scores(qi, kj, slot, kind):
        for h in range(DIFF_HEADS):
            scores_head(h, qi, kj, slot, kind)

    def values_head(h, kj):
        acc_ref[arows[h], :] = (alpha_ref[h] * acc_ref[arows[h], :]
                                + _dot(_values_lhs(vt_ref[kj, vrows[h], :]), p_ref[krows[h], :]))

    def step(qi, kj, nqi, nkj, slot, next_kind):
        for h in range(SKEW):
            scores_head(h, nqi, nkj, 1 - slot, next_kind)
        for h in range(DIFF_HEADS):
            if h + SKEW < DIFF_HEADS:
                scores_head(h + SKEW, nqi, nkj, 1 - slot, next_kind)
            _softmax_step(s_ref, mc_ref, m_ref, alpha_ref, p_ref, slot, h, krows[h])
            if h > 0:
                values_head(h - 1, kj)
        values_head(DIFF_HEADS - 1, kj)

    def finish(qi):
        q0 = pl.multiple_of(qi * TQ, TQ)
        for h in range(DIFF_HEADS):
            base = h * (LANES + ONES_ROWS)
            inv_l = 1.0 / acc_ref[pl.ds(base + LANES, 1), :]
            ot = (acc_ref[pl.ds(base, LANES), pl.ds(0, TQ)] * inv_l[:, :TQ]
                  - acc_ref[pl.ds(base, LANES), pl.ds(TQ, TQ)] * (lam * inv_l[:, TQ:]))
            ot = ot * lax.rsqrt(jnp.mean(ot * ot, axis=0, keepdims=True) + SUBLN_EPS)
            o_ref[pl.ds(q0, TQ), h * LANES:(h + 1) * LANES] = (ot.T * gain).astype(BF16)
        _softmax_init(m_ref, acc_ref)

    def split_queries(qi):
        q0 = _tile_start(qi, TQ)
        for h in range(DIFF_HEADS):
            q = q_ref[pl.ds(q0, TQ), h * LANES:(h + 1) * LANES]
            zero = jnp.zeros_like(q)
            q2_ref[pl.ds(h * 2 * TQ, TQ), :] = jnp.where(first, q, zero)
            q2_ref[pl.ds(h * 2 * TQ + TQ, TQ), :] = jnp.where(first, zero, q)

    _softmax_init(m_ref, acc_ref)
    split_queries(0)
    scores(0, 0, 0, DIAG)
    _run_tile_program(tab_ref, keys, step, finish, prepare=split_queries)


def _bias_kernel(idx_ref, tab_ref, o_ref):
    h = pl.program_id(0)
    far = tab_ref[REL_BUCKETS // 2 - 1, h]
    for t in range(2):
        idx = idx_ref[t]
        acc = jnp.zeros(idx.shape, F32)
        for b in range(REL_BUCKETS):
            acc = jnp.where(idx == b, (tab_ref[b, h] - far) * LOG2E, acc)
        o_ref[0, t] = acc


def _t5_bucket_np(rel):
    nb = REL_BUCKETS // 2
    max_exact = nb // 2
    ret = np.where(rel > 0, nb, 0)
    n = np.abs(rel)
    nf = np.maximum(n, 1).astype(np.float32)
    large = max_exact + (np.log(nf / np.float32(max_exact)) / np.float32(math.log(REL_MAX_DIST / max_exact))
                         * np.float32(nb - max_exact)).astype(np.int32)
    large = np.minimum(large, nb - 1)
    return (ret + np.where(n < max_exact, n, large)).astype(np.int32)


def _bias_tiles(rel_table):
    assert TK >= REL_MAX_DIST
    rel = np.stack([(np.arange(TK)[:, None] + (t - 1) * TK) - np.arange(TQ)[None, :] for t in range(2)])
    idx = jnp.asarray(_t5_bucket_np(rel))
    return pl.pallas_call(
        _bias_kernel,
        name="rel_bias",
        grid=(DIFF_HEADS,),
        in_specs=[pl.BlockSpec((2, TK, TQ), lambda h: (0, 0, 0)),
                  pl.BlockSpec(memory_space=pltpu.SMEM)],
        out_specs=pl.BlockSpec((1, 2, TK, TQ), lambda h: (h, 0, 0, 0)),
        out_shape=jax.ShapeDtypeStruct((DIFF_HEADS, 2, TK, TQ), F32),
        compiler_params=_params(1),
    )(idx, rel_table.astype(F32))


def _attention_call(kernel_fn, name, q, k, vt, extra, out_width, scratch, B, S, program=None):
    T = q.shape[0]
    smem = [] if program is None else [jnp.asarray(program)]
    seq = lambda a: pl.BlockSpec((S, a.shape[1]), lambda b: (b, 0))
    const = lambda a: pl.BlockSpec(a.shape, lambda b: (0,) * a.ndim, pipeline_mode=pl.Buffered(1))
    return pl.pallas_call(
        kernel_fn,
        name=name,
        grid=(B,),
        in_specs=[pl.BlockSpec(memory_space=pltpu.SMEM) for _ in smem]
        + [seq(q), seq(k), pl.BlockSpec((S // TK,) + vt.shape[1:], lambda b: (b, 0, 0))]
        + [const(a) for a in extra],
        out_specs=pl.BlockSpec((S, out_width), lambda b: (b, 0)),
        out_shape=jax.ShapeDtypeStruct((T, out_width), BF16),
        scratch_shapes=scratch,
        compiler_params=_params(1),
    )(*smem, q, k, vt, *extra)


def _softmax_scratch(heads, n, dv):
    return ([pltpu.VMEM((heads, 1, n), F32)] * 2 + [pltpu.VMEM((heads * (dv + ONES_ROWS), n), F32)]
            + [pltpu.VMEM((2, heads, 1, n), F32), pltpu.VMEM((2, heads * TK, n), F32)]
            + [pltpu.VMEM((heads * TK, n), BF16)])


def _mla_attention(qm, km, vmt, B, S):
    table, keys = _tile_program(S // TQ, lambda qi, kj: DIAG if kj == qi else FAR)
    return _attention_call(functools.partial(_mla_attn_kernel, keys), "mla_attn", qm, km, vmt, (),
                           MLA_HEADS * MLA_V, _softmax_scratch(MLA_HEADS, TQ, MLA_V), B, S, program=table)


def _sb_attention(qs, ks, vst, B, S):
    n = SB_HEADS * TK
    scratch = ([pltpu.VMEM((SB_HEADS, 1, TQ), F32)] * 2 + [pltpu.VMEM((SB_W, TQ), F32)]
               + [pltpu.VMEM((n, TQ), F32)] * 2 + [pltpu.VMEM((n, TQ), BF16)] * 2)
    return _attention_call(_sb_attn_kernel, "sb_attn", qs, ks, vst, (), SB_W, scratch, B, S)


def _diff_attention(qd, kd, vdt, bias, lamp, g, lam_init, B, S):
    table, keys = _tile_program(S // TQ, _tile_kind)
    return _attention_call(functools.partial(_diff_attn_kernel, lam_init, keys), "diff_attn", qd, kd, vdt,
                           (bias, lamp, g), DIFF_W,
                           _softmax_scratch(DIFF_HEADS, 2 * TQ, LANES) + [pltpu.VMEM((DIFF_HEADS * 2 * TQ, LANES), BF16)],
                           B, S,
                           program=table)


def _out0_in1_kernel(om_ref, os_ref, z_ref, x_ref, gate_ref, wo_ref, shift_ref, scale_ref, g_ref, w1_ref,
                     x1_ref, qd_ref, kd_ref, vdt_ref, zd_ref):
    z = z_ref[...].astype(F32)
    y = jnp.concatenate([om_ref[...].astype(F32), os_ref[...].astype(F32)], axis=1) * _silu(z)
    out = _dot(y.astype(BF16), wo_ref[...])
    x1 = x_ref[...] + gate_ref[0] * out
    x1_ref[...] = x1
    h = (_rms(x1, NORM_EPS) * g_ref[...] * (1.0 + scale_ref[0]) + shift_ref[0]).astype(BF16)
    qd_ref[...] = (_dot(h, w1_ref[:, 0:DIFF_W]) * (DIFF_HEAD_DIM ** -0.5 * LOG2E)).astype(BF16)
    kd_ref[...] = _dot(h, w1_ref[:, DIFF_W:2 * DIFF_W]).astype(BF16)
    _store_kv_tiles(vdt_ref, _dot(h, w1_ref[:, 2 * DIFF_W:3 * DIFF_W]).T)
    zd_ref[...] = _dot(h, w1_ref[:, 3 * DIFF_W:4 * DIFF_W]).astype(BF16)


def _layer0_out_layer1_in(om, osb, z, x2, mod3, wo, g1, w1, B):
    T, D = x2.shape
    nt = T // TM
    per_b = nt // B
    full = lambda a: pl.BlockSpec(a.shape, lambda i: (0,) * a.ndim)
    tok = lambda w: pl.BlockSpec((TM, w), lambda i: (i, 0))
    mod = lambda row: pl.BlockSpec((1, 1, D), lambda i: (row * B + i // per_b, 0, 0))
    tok_out = jax.ShapeDtypeStruct((T, DIFF_W), BF16)
    return pl.pallas_call(
        _out0_in1_kernel,
        name="l0_out_l1_in",
        grid=(nt,),
        in_specs=[tok(om.shape[1]), tok(osb.shape[1]), tok(D), tok(D), mod(2), full(wo),
                  mod(3), mod(4), full(g1), full(w1)],
        out_specs=[tok(D), tok(DIFF_W), tok(DIFF_W),
                   pl.BlockSpec((TM // TK, DIFF_W, TK), lambda i: (i, 0, 0)), tok(DIFF_W)],
        out_shape=[jax.ShapeDtypeStruct((T, D), F32), tok_out, tok_out,
                   jax.ShapeDtypeStruct((T // TK, DIFF_W, TK), BF16), tok_out],
        compiler_params=_params(1),
    )(om, osb, z, x2, mod3, wo, mod3, mod3, g1, w1)


def _out1_kernel(o_ref, z_ref, x_ref, gate_ref, wo_ref, g_ref, y_ref):
    y = o_ref[...].astype(F32) * _silu(z_ref[...].astype(F32))
    out = _dot(y.astype(BF16), wo_ref[...])
    x2 = x_ref[...] + gate_ref[0] * out
    y_ref[...] = _rms(x2, NORM_EPS) * g_ref[...]


def _layer1_out(od, zd, x1, mod3, wo, gf, B):
    T, D = x1.shape
    nt = T // TM
    per_b = nt // B
    full = lambda a: pl.BlockSpec(a.shape, lambda i: (0,) * a.ndim)
    tok = lambda w: pl.BlockSpec((TM, w), lambda i: (i, 0))
    return pl.pallas_call(
        _out1_kernel,
        name="l1_out",
        grid=(nt,),
        in_specs=[tok(D), tok(D), tok(D), pl.BlockSpec((1, 1, D), lambda i: (5 * B + i // per_b, 0, 0)),
                  full(wo), full(gf)],
        out_specs=tok(D),
        out_shape=jax.ShapeDtypeStruct((T, D), F32),
        compiler_params=_params(1),
    )(od, zd, x1, mod3, wo, gf)


def kernel(x, c, pos_offset, rel_bias_table, ada_w, ada_b, norm_g, final_g, ab_w_in, ab_q_norm_g, ab_kv_norm_g,
           ab_w_uq, ab_w_ukv, ab_w_out, dif_w_in, dif_lam_q1, dif_lam_k1, dif_lam_q2, dif_lam_k2, dif_subln_g,
           dif_w_out):
    B, S, D = x.shape
    assert D == D_MODEL and S % TQ == 0 and TQ == TK and (B * S) % TM == 0 and S % TM == 0 and TM % TK == 0
    x2 = x.reshape(B * S, D)

    mod3 = _modulation(c, ada_w, ada_b)

    w0, wqa, wk, wvt, wvst = _prep_ab_weights(ab_w_in[0], ab_w_uq[0], ab_w_ukv[0])
    qm, km, vmt, qs, ks, vst, z = _layer0_in(
        x2, pos_offset, mod3, norm_g[0:1], w0, wqa, wk, wvt, wvst, ab_q_norm_g[0:1], ab_kv_norm_g[0:1], B)
    om = _mla_attention(qm, km, vmt, B, S)
    osb = _sb_attention(qs, ks, vst, B, S)

    x1, qd, kd, vdt, zd = _layer0_out_layer1_in(
        om, osb, z, x2, mod3, ab_w_out[0].astype(BF16), norm_g[1:2], dif_w_in[0].astype(BF16), B)
    lam_init = 0.8 - 0.6 * math.exp(-0.3 * 1)
    lamp = jnp.stack([dif_lam_q1[0], dif_lam_k1[0], dif_lam_q2[0], dif_lam_k2[0]]).astype(F32)
    bias = _bias_tiles(rel_bias_table)
    od = _diff_attention(qd, kd, vdt, bias, lamp, dif_subln_g[0:1], lam_init, B, S)
    y = _layer1_out(od, zd, x1, mod3, dif_w_out[0].astype(BF16), final_g[None, :], B)
    return y.reshape(B, S, D)
```

```python
import functools
import math

import numpy as np
import jax
import jax.numpy as jnp
from jax import lax
from jax.experimental import pallas as pl
from jax.experimental.pallas import tpu as pltpu

D_MODEL = 1024
DEPTH = 2
CHUNK = 64

MLA_HEADS = 8
MLA_Q_LORA = 384
MLA_KV_LORA = 256
MLA_NOPE = 64
MLA_ROPE = 32
MLA_V = 64
SB_HEADS = 8
SB_HEAD_DIM = 64
SB_W = SB_HEADS * SB_HEAD_DIM
DIFF_HEADS = 8
DIFF_HEAD_DIM = 64
DIFF_W = DIFF_HEADS * 2 * DIFF_HEAD_DIM

REL_BUCKETS = 32
REL_MAX_DIST = 128
ROPE_THETA = 10000.0
NORM_EPS = 1e-6
SUBLN_EPS = 1e-5
NEG_INF = -1e30
SB_UNDERFLOW_LOG2 = 152.0
LOG2E = math.log2(math.e)

LANES = 128
V7X_VMEM_BYTES = 64 * 1024 * 1024
VMEM_LIMIT = V7X_VMEM_BYTES * 7 // 8

TM = 512
TM_OUT = 1024
TQ = 256
TK = 256
SKEW = 0
ONES_ROWS = 16

F32 = jnp.float32
BF16 = jnp.bfloat16


def _silu(z):
    return z * (1.0 / (1.0 + jnp.exp(-z)))


def _dot(a, b):
    return jnp.dot(a, b, preferred_element_type=F32)


def _dot_nt(a, b):
    return lax.dot_general(a, b, (((1,), (1,)), ((), ())), preferred_element_type=F32)


def _rep(x, n):
    return x if n == 1 else jnp.concatenate([x] * n, axis=1)


def _params(n_axes=1):
    return pltpu.CompilerParams(dimension_semantics=("arbitrary",) * n_axes, vmem_limit_bytes=VMEM_LIMIT)


def _store_kv_tiles(ref, xt):
    for j in range(xt.shape[1] // TK):
        ref[j] = xt[:, j * TK:(j + 1) * TK].astype(ref.dtype)


def _mod_kernel(c_ref, w_ref, b_ref, o_ref):
    ca = _silu(c_ref[...]).astype(BF16)
    o_ref[0] = _dot(ca, w_ref[0].astype(BF16)) + b_ref[0]


def _modulation(c, ada_w, ada_b):
    B, D = c.shape
    out = pl.pallas_call(
        _mod_kernel,
        name="ada_mod",
        grid=(DEPTH, 3),
        in_specs=[
            pl.BlockSpec((B, D), lambda i, j: (0, 0)),
            pl.BlockSpec((1, D, D), lambda i, j: (i, 0, j)),
            pl.BlockSpec((1, 1, D), lambda i, j: (i * 3 + j, 0, 0)),
        ],
        out_specs=pl.BlockSpec((1, B, D), lambda i, j: (i * 3 + j, 0, 0)),
        out_shape=jax.ShapeDtypeStruct((DEPTH * 3, B, D), F32),
        compiler_params=_params(2),
    )(c, ada_w, ada_b.reshape(DEPTH * 3, 1, D))
    return out.reshape(DEPTH * 3 * B, 1, D)


def _rope_static_tables(S):
    half = MLA_ROPE // 2
    inv_freq = np.float32(ROPE_THETA) ** (-(np.arange(half, dtype=np.float32) / np.float32(half)))
    inv_freq = inv_freq.astype(np.float32)
    ang = np.arange(S, dtype=np.float64)[:, None] * inv_freq.astype(np.float64)[None, :]
    cos, sin = np.cos(ang), np.sin(ang)
    one = np.ones((S, MLA_NOPE))
    zn = np.zeros((S, MLA_NOPE))
    zp = np.zeros((S, LANES - MLA_NOPE - MLA_ROPE))
    tabs = [np.concatenate(parts, axis=1).astype(np.float32) for parts in (
        (one, cos, cos, zp), (zn, sin, sin, zp), (zn, -cos, cos, zp), (zn, -sin, sin, zp))]
    f = np.concatenate([np.zeros(MLA_NOPE, np.float32), inv_freq, inv_freq,
                        np.zeros(LANES - MLA_NOPE - MLA_ROPE, np.float32)])[None, :]
    return [jnp.asarray(t) for t in tabs], jnp.asarray(f)


def _rope_tiles(pos0, f_ref, cs_ref, ss_ref, cs2_ref, ss2_ref):
    ang0 = pos0.astype(F32) * f_ref[...]
    a, b = jnp.cos(ang0), jnp.sin(ang0)
    ctab = a * cs_ref[...] - b * ss_ref[...]
    stab = b * cs2_ref[...] + a * ss2_ref[...]
    return ctab, stab


AB_SEG = (MLA_Q_LORA, MLA_KV_LORA, SB_W, SB_W, MLA_HEADS * MLA_V + SB_W, LANES)
AB_OFF = tuple(int(v) for v in np.cumsum((0,) + AB_SEG))


def _rms(x, eps):
    return x * lax.rsqrt(jnp.mean(x * x, axis=-1, keepdims=True) + eps)


def _swap_rope_halves(v):
    half = MLA_ROPE // 2
    n = v.shape[1]
    lane = lax.broadcasted_iota(jnp.int32, v.shape, 1) % LANES
    return jnp.where(lane < MLA_NOPE + half, pltpu.roll(v, n - half, 1), pltpu.roll(v, half, 1))


def _in0_kernel(per_b, pos_ref, x_ref, shift_ref, scale_ref, g_ref, w_ref, wqa_ref, wk_ref, wvt_ref,
                wvst_ref, qg_ref, kvg_ref, f_ref, cs_ref, ss_ref, cs2_ref, ss2_ref,
                qm_ref, km_ref, vmt_ref, qs_ref, ks_ref, vst_ref, z_ref):
    x = x_ref[...]
    h = _rms(x, NORM_EPS) * g_ref[...] * (1.0 + scale_ref[0]) + shift_ref[0]
    h = h.astype(BF16)

    def seg(i):
        return _dot(h, w_ref[:, AB_OFF[i]:AB_OFF[i + 1]])

    ctab, stab = _rope_tiles(pos_ref[pl.program_id(0) // per_b], f_ref, cs_ref, ss_ref, cs2_ref, ss2_ref)
    nh = MLA_HEADS

    cq = (_rms(seg(0), NORM_EPS) * qg_ref[...]).astype(BF16)
    q = _dot(cq, wqa_ref[...])
    q = q * _rep(ctab, nh) + _swap_rope_halves(q) * _rep(stab, nh)
    qm_ref[...] = (q * ((MLA_NOPE + MLA_ROPE) ** -0.5 * LOG2E)).astype(BF16)

    ckv = (_rms(seg(1), NORM_EPS) * kvg_ref[...]).astype(BF16)
    kr = seg(5)
    krope = kr * ctab + _swap_rope_halves(kr) * stab
    km_ref[...] = (_dot(ckv, wk_ref[...]) + _rep(krope, nh)).astype(BF16)
    _store_kv_tiles(vmt_ref, _dot(ckv, wvt_ref[...]).T)

    qs_ref[...] = (seg(2) * (SB_HEAD_DIM ** -0.5 * LOG2E)).astype(BF16)
    ks_ref[...] = seg(3).astype(BF16)
    _store_kv_tiles(vst_ref, _dot(h, wvst_ref[...]).T)
    z_ref[...] = seg(4).astype(BF16)


def _prep_ab_weights(w_in, w_uq, w_ukv):
    D = w_in.shape[0]
    c = np.cumsum([MLA_Q_LORA, MLA_KV_LORA, MLA_ROPE, SB_W, SB_W, SB_W]).tolist()
    w_kr = w_in[:, c[1]:c[2]]
    zl = jnp.zeros((D, MLA_NOPE), w_in.dtype)
    zr = jnp.zeros((D, LANES - MLA_NOPE - MLA_ROPE), w_in.dtype)
    kr_a = jnp.concatenate([zl, w_kr, zr], axis=1)
    w0 = jnp.concatenate([
        w_in[:, :c[1]],
        w_in[:, c[2]:c[3]],
        w_in[:, c[3]:c[4]],
        w_in[:, c[5]:],
        kr_a], axis=1).astype(BF16)
    wvst = w_in[:, c[4]:c[5]].astype(BF16)

    hq = MLA_NOPE + MLA_ROPE
    uq = w_uq.reshape(MLA_Q_LORA, MLA_HEADS, hq)
    pad = jnp.zeros((MLA_Q_LORA, MLA_HEADS, LANES - hq), w_uq.dtype)
    wqa = jnp.concatenate([uq, pad], axis=2).reshape(MLA_Q_LORA, MLA_HEADS * LANES).astype(BF16)
    ukv = w_ukv.reshape(MLA_KV_LORA, MLA_HEADS, MLA_NOPE + MLA_V)
    zk = jnp.zeros((MLA_KV_LORA, MLA_HEADS, LANES - MLA_NOPE), w_ukv.dtype)
    wk = jnp.concatenate([ukv[:, :, :MLA_NOPE], zk], axis=2).reshape(MLA_KV_LORA, MLA_HEADS * LANES).astype(BF16)
    wvt = ukv[:, :, MLA_NOPE:].reshape(MLA_KV_LORA, MLA_HEADS * MLA_V).astype(BF16)
    return w0, wqa, wk, wvt, wvst


def _layer0_in(x2, pos_offset, mod3, g, w0, wqa, wk, wvt, wvst, qg, kvg, B):
    T, D = x2.shape
    nt = T // TM
    per_b = nt // B
    rope_tabs, rope_f = _rope_static_tables(T // B)
    full = lambda a: pl.BlockSpec(a.shape, lambda i: (0,) * a.ndim)
    tok = lambda w: pl.BlockSpec((TM, w), lambda i: (i, 0))
    seq = pl.BlockSpec((TM, LANES), lambda i: (i % per_b, 0))
    kvt = lambda w: pl.BlockSpec((TM // TK, w, TK), lambda i: (i, 0, 0))
    tok_out = lambda w: jax.ShapeDtypeStruct((T, w), BF16)
    kvt_out = lambda w: jax.ShapeDtypeStruct((T // TK, w, TK), BF16)
    hv = MLA_HEADS * MLA_V
    return pl.pallas_call(
        functools.partial(_in0_kernel, per_b),
        name="l0_in",
        grid=(nt,),
        in_specs=[
            pl.BlockSpec(memory_space=pltpu.SMEM),
            tok(D),
            pl.BlockSpec((1, 1, D), lambda i: (0 * B + i // per_b, 0, 0)),
            pl.BlockSpec((1, 1, D), lambda i: (1 * B + i // per_b, 0, 0)),
            full(g), full(w0), full(wqa), full(wk), full(wvt), full(wvst), full(qg), full(kvg),
            full(rope_f), seq, seq, seq, seq,
        ],
        out_specs=[tok(MLA_HEADS * LANES), tok(MLA_HEADS * LANES), kvt(hv), tok(SB_W), tok(SB_W), kvt(SB_W),
                   tok(hv + SB_W)],
        out_shape=[tok_out(MLA_HEADS * LANES), tok_out(MLA_HEADS * LANES), kvt_out(hv), tok_out(SB_W),
                   tok_out(SB_W), kvt_out(SB_W), tok_out(hv + SB_W)],
        compiler_params=_params(1),
    )(pos_offset, x2, mod3, mod3, g, w0, wqa, wk, wvt, wvst, qg, kvg, rope_f, *rope_tabs)


def _chunk_mask_t():
    k = lax.broadcasted_iota(jnp.int32, (TK, TQ), 0)
    q = lax.broadcasted_iota(jnp.int32, (TK, TQ), 1)
    shift = CHUNK.bit_length() - 1
    return (k >> shift) <= (q >> shift)


def _tile_start(i, size):
    return i * size if isinstance(i, int) else pl.multiple_of(i * size, size)


FAR, NEAR, DIAG = 0, 1, 2
PROGRAM_ROW = 5


def _tile_kind(qi, kj):
    return DIAG if kj == qi else NEAR if kj == qi - 1 else FAR


def _tile_program(n_tiles, kind_of):
    items = [(qi, kj) for qi in range(n_tiles) for kj in range(qi + 1)]
    assert len(items) % 2 == 0 and n_tiles >= 2
    items.append((n_tiles - 1, 0))
    kinds = [kind_of(qi, kj) for qi, kj in items]
    keys, rows = [], []
    for t in range(len(items) - 1):
        key = (kinds[t], kinds[t + 1])
        if key not in keys:
            keys.append(key)
        rows.append([keys.index(key), *items[t], *items[t + 1]])
    return np.asarray(rows, np.int32).reshape(-1), keys


def _run_tile_program(tab_ref, keys, step, finish, prepare=None):
    def variant(key, slot):
        def run(qi, kj, nqi, nkj):
            if key[0] == DIAG and prepare is not None:
                prepare(nqi)
            step(qi, kj, nqi, nkj, slot, key[1])
            if key[0] == DIAG:
                finish(qi)
        return run

    variants = [[variant(key, slot) for key in keys] for slot in range(2)]

    def trip(t, carry):
        for slot in range(2):
            base = (2 * t + slot) * PROGRAM_ROW
            lax.switch(tab_ref[base], variants[slot], *[tab_ref[base + 1 + j] for j in range(PROGRAM_ROW - 1)])
        return carry

    lax.fori_loop(0, tab_ref.shape[0] // (2 * PROGRAM_ROW), trip, 0)


def _softmax_step(s_ref, mc_ref, m_ref, alpha_ref, p_ref, slot, h, rows):
    m_prev = m_ref[h]
    m_new = jnp.maximum(m_prev, mc_ref[slot, h])
    p_ref[rows, :] = jnp.exp2(s_ref[slot, rows, :] - m_new).astype(BF16)
    alpha_ref[h] = jnp.exp2(m_prev - m_new)
    m_ref[h] = m_new


def _values_lhs(vt):
    return jnp.concatenate([vt, jnp.ones((ONES_ROWS, vt.shape[1]), vt.dtype)], axis=0)


def _softmax_init(m_ref, acc_ref):
    m_ref[...] = jnp.full(m_ref.shape, -jnp.inf, F32)
    acc_ref[...] = jnp.zeros(acc_ref.shape, F32)


def _mla_attn_kernel(keys, tab_ref, q_ref, k_ref, vt_ref, o_ref, m_ref, alpha_ref, acc_ref, mc_ref, s_ref, p_ref):
    S = q_ref.shape[0]
    mask = _chunk_mask_t()
    krows = [pl.ds(h * TK, TK) for h in range(MLA_HEADS)]
    vrows = [pl.ds(h * MLA_V, MLA_V) for h in range(MLA_HEADS)]
    arows = [pl.ds(h * (MLA_V + ONES_ROWS), MLA_V + ONES_ROWS) for h in range(MLA_HEADS)]

    def out_t(h):
        base = h * (MLA_V + ONES_ROWS)
        return acc_ref[pl.ds(base, MLA_V), :] * (1.0 / acc_ref[pl.ds(base + MLA_V, 1), :])

    def scores_head(h, qi, kj, slot, masked):
        q0 = _tile_start(qi, TQ)
        k0 = _tile_start(kj, TK)
        hl = slice(h * LANES, (h + 1) * LANES)
        s = _dot_nt(k_ref[pl.ds(k0, TK), hl], q_ref[pl.ds(q0, TQ), hl])
        if masked:
            s = jnp.where(mask, s, NEG_INF)
        s_ref[slot, krows[h], :] = s
        mc_ref[slot, h] = jnp.max(s, axis=0, keepdims=True)

    def scores(qi, kj, slot, masked):
        for h in range(MLA_HEADS):
            scores_head(h, qi, kj, slot, masked)

    def values_head(h, kj):
        acc_ref[arows[h], :] = (alpha_ref[h] * acc_ref[arows[h], :]
                                + _dot(_values_lhs(vt_ref[kj, vrows[h], :]), p_ref[krows[h], :]))

    def step(qi, kj, nqi, nkj, slot, next_masked):
        for h in range(SKEW):
            scores_head(h, nqi, nkj, 1 - slot, next_masked)
        for h in range(MLA_HEADS):
            if h + SKEW < MLA_HEADS:
                scores_head(h + SKEW, nqi, nkj, 1 - slot, next_masked)
            _softmax_step(s_ref, mc_ref, m_ref, alpha_ref, p_ref, slot, h, krows[h])
            if h > 0:
                values_head(h - 1, kj)
        values_head(MLA_HEADS - 1, kj)

    def finish(qi):
        q0 = pl.multiple_of(qi * TQ, TQ)
        for j in range(MLA_HEADS // 2):
            ot = jnp.concatenate([out_t(2 * j), out_t(2 * j + 1)], axis=0)
            o_ref[pl.ds(q0, TQ), j * LANES:(j + 1) * LANES] = ot.T.astype(BF16)
        _softmax_init(m_ref, acc_ref)

    _softmax_init(m_ref, acc_ref)
    scores(0, 0, 0, True)
    _run_tile_program(tab_ref, keys, lambda qi, kj, nqi, nkj, slot, kind: step(qi, kj, nqi, nkj, slot, kind == DIAG),
                      finish)


def _sb_attn_kernel(q_ref, k_ref, vt_ref, o_ref, r_ref, rs_ref, acc_ref, z_ref, ls_ref, hi_ref, lo_ref):
    S = q_ref.shape[0]
    k_i = lax.broadcasted_iota(jnp.int32, (TK, TQ), 0)
    q_i = lax.broadcasted_iota(jnp.int32, (TK, TQ), 1)
    causal = k_i < q_i
    tri = jnp.where(lax.broadcasted_iota(jnp.int32, (TK, TK), 1) > lax.broadcasted_iota(jnp.int32, (TK, TK), 0),
                    1.0, 0.0).astype(BF16)
    lane = lax.broadcasted_iota(jnp.int32, (TQ, LANES), 1)
    first = lane < SB_HEAD_DIM
    crows = [pl.ds(c * TK, TK) for c in range(2 * SB_HEADS)]
    vrows = [pl.ds(h * SB_HEAD_DIM, SB_HEAD_DIM) for h in range(SB_HEADS)]
    pairs = [slice((h // 2) * LANES, (h // 2 + 1) * LANES) for h in range(SB_HEADS)]

    def scores_chain(c, h, q0, kj):
        k0 = pl.multiple_of(kj * TK, TK)
        q = q_ref[pl.ds(q0, TQ), pairs[h]]
        q = jnp.where(first, q, jnp.zeros_like(q)) if h % 2 == 0 else jnp.where(first, jnp.zeros_like(q), q)
        z_ref[crows[c], :] = _dot_nt(k_ref[pl.ds(k0, TK), pairs[h]], q)

    def logs_chain(c, masked):
        z = z_ref[crows[c], :]
        sp = jnp.log(1.0 + jnp.exp2(-jnp.abs(z))) * LOG2E
        ls = jnp.minimum(z, 0.0) - sp
        l1m = ls - z
        if masked:
            l1m = jnp.where(causal, l1m, 0.0)
        hi = l1m.astype(BF16)
        ls_ref[crows[c], :] = ls
        hi_ref[crows[c], :] = hi
        lo_ref[crows[c], :] = (l1m - hi.astype(F32)).astype(BF16)
        rs_ref[c] = l1m[0:1, :]

    def suffix_chain(c):
        z_ref[crows[c], :] = _dot(tri, hi_ref[crows[c], :]) + _dot(tri, lo_ref[crows[c], :])

    def weights_chain(c, h, masked):
        r_prev = r_ref[h]
        suffix = z_ref[crows[c], :]
        w = jnp.exp2(ls_ref[crows[c], :] + suffix + r_prev)
        if masked:
            w = jnp.where(causal, w, 0.0)
        hi_ref[crows[c], :] = w.astype(BF16)
        r_ref[h] = r_prev + (rs_ref[c] + suffix[0:1, :])

    def values_chain(c, h, kj):
        acc_ref[vrows[h], :] += _dot(vt_ref[kj, vrows[h], :], hi_ref[crows[c], :])

    def step(q0, tiles):
        chains = [(h, kj, masked) for h in range(SB_HEADS) for kj, masked in tiles]
        n = len(chains)
        lead, lag = 2, 2
        for c in range(lead):
            scores_chain(c, chains[c][0], q0, chains[c][1])
        for t in range(n + lag):
            if t + lead < n:
                scores_chain(t + lead, chains[t + lead][0], q0, chains[t + lead][1])
            if t < n:
                logs_chain(t, chains[t][2])
                suffix_chain(t)
            if t >= lag:
                h, kj, masked = chains[t - lag]
                weights_chain(t - lag, h, masked)
                values_chain(t - lag, h, kj)

    def start():
        r_ref[...] = jnp.zeros(r_ref.shape, F32)
        acc_ref[...] = jnp.zeros(acc_ref.shape, F32)

    def finish(q0):
        for j in range(SB_HEADS // 2):
            ot = acc_ref[pl.ds(j * LANES, LANES), :]
            o_ref[pl.ds(q0, TQ), j * LANES:(j + 1) * LANES] = ot.T.astype(BF16)

    def still_live():
        return (jnp.max(r_ref[...]) > -SB_UNDERFLOW_LOG2).astype(jnp.int32)

    start()
    step(0, [(0, True)])
    finish(0)

    def q_body(qi, carry):
        q0 = pl.multiple_of(qi * TQ, TQ)
        start()
        step(q0, [(qi, True), (qi - 1, False)])

        def cond(c):
            t, live = c
            return jnp.logical_and(t < qi, live > 0)

        def body(c):
            t, _ = c
            step(q0, [(qi - 1 - t, False)])
            return t + 1, still_live()

        lax.while_loop(cond, body, (jnp.int32(1), still_live()))
        finish(q0)
        return carry

    lax.fori_loop(1, S // TQ, q_body, 0)


def _diff_attn_kernel(lam_init, keys, tab_ref, q_ref, k_ref, vt_ref, bias_ref, lamp_ref, g_ref, o_ref,
                      m_ref, alpha_ref, acc_ref, mc_ref, s_ref, p_ref, q2_ref):
    S = q_ref.shape[0]
    mask = _chunk_mask_t()
    mask2 = jnp.concatenate([mask, mask], axis=1)
    lane = lax.broadcasted_iota(jnp.int32, (TQ, LANES), 1)
    first = lane < DIFF_HEAD_DIM
    lp = lamp_ref[...]
    lam = (jnp.exp(jnp.sum(lp[0:1] * lp[1:2], axis=-1, keepdims=True))
           - jnp.exp(jnp.sum(lp[2:3] * lp[3:4], axis=-1, keepdims=True)) + lam_init)
    krows = [pl.ds(h * TK, TK) for h in range(DIFF_HEADS)]
    vrows = [pl.ds(h * LANES, LANES) for h in range(DIFF_HEADS)]
    arows = [pl.ds(h * (LANES + ONES_ROWS), LANES + ONES_ROWS) for h in range(DIFF_HEADS)]
    gain = g_ref[...] * (1.0 - lam_init)

    def scores_head(h, qi, kj, slot, kind):
        q0 = _tile_start(qi, TQ)
        k0 = _tile_start(kj, TK)
        hl = slice(h * LANES, (h + 1) * LANES)
        s = _dot_nt(k_ref[pl.ds(k0, TK), hl], q2_ref[pl.ds(h * 2 * TQ, 2 * TQ), :])
        if kind != FAR:
            s = s + _rep(bias_ref[h, kind - 1], 2)
        if kind == DIAG:
            s = jnp.where(mask2, s, NEG_INF)
        s_ref[slot, krows[h], :] = s
        mc_ref[slot, h] = jnp.max(s, axis=0, keepdims=True)

    def scores(qi, kj, slot, kind):
        for h in range(DIFF_HEADS):
            scores_head(h, qi, kj, slot, kind)

    def values_head(h, kj):
        acc_ref[arows[h], :] = (alpha_ref[h] * acc_ref[arows[h], :]
                                + _dot(_values_lhs(vt_ref[kj, vrows[h], :]), p_ref[krows[h], :]))

    def step(qi, kj, nqi, nkj, slot, next_kind):
        for h in range(SKEW):
            scores_head(h, nqi, nkj, 1 - slot, next_kind)
        for h in range(DIFF_HEADS):
            if h + SKEW < DIFF_HEADS:
                scores_head(h + SKEW, nqi, nkj, 1 - slot, next_kind)
            _softmax_step(s_ref, mc_ref, m_ref, alpha_ref, p_ref, slot, h, krows[h])
            if h > 0:
                values_head(h - 1, kj)
        values_head(DIFF_HEADS - 1, kj)

    def finish(qi):
        q0 = pl.multiple_of(qi * TQ, TQ)
        for h in range(DIFF_HEADS):
            base = h * (LANES + ONES_ROWS)
            inv_l = 1.0 / acc_ref[pl.ds(base + LANES, 1), :]
            ot = (acc_ref[pl.ds(base, LANES), pl.ds(0, TQ)] * inv_l[:, :TQ]
                  - acc_ref[pl.ds(base, LANES), pl.ds(TQ, TQ)] * (lam * inv_l[:, TQ:]))
            ot = ot * lax.rsqrt(jnp.mean(ot * ot, axis=0, keepdims=True) + SUBLN_EPS)
            o_ref[pl.ds(q0, TQ), h * LANES:(h + 1) * LANES] = (ot.T * gain).astype(BF16)
        _softmax_init(m_ref, acc_ref)

    def split_queries(qi):
        q0 = _tile_start(qi, TQ)
        for h in range(DIFF_HEADS):
            q = q_ref[pl.ds(q0, TQ), h * LANES:(h + 1) * LANES]
            zero = jnp.zeros_like(q)
            q2_ref[pl.ds(h * 2 * TQ, TQ), :] = jnp.where(first, q, zero)
            q2_ref[pl.ds(h * 2 * TQ + TQ, TQ), :] = jnp.where(first, zero, q)

    _softmax_init(m_ref, acc_ref)
    split_queries(0)
    scores(0, 0, 0, DIAG)
    _run_tile_program(tab_ref, keys, step, finish, prepare=split_queries)


def _bias_kernel(idx_ref, tab_ref, o_ref):
    h = pl.program_id(0)
    far = tab_ref[REL_BUCKETS // 2 - 1, h]
    for t in range(2):
        idx = idx_ref[t]
        acc = jnp.zeros(idx.shape, F32)
        for b in range(REL_BUCKETS):
            acc = jnp.where(idx == b, (tab_ref[b, h] - far) * LOG2E, acc)
        o_ref[0, t] = acc


def _t5_bucket_np(rel):
    nb = REL_BUCKETS // 2
    max_exact = nb // 2
    ret = np.where(rel > 0, nb, 0)
    n = np.abs(rel)
    nf = np.maximum(n, 1).astype(np.float32)
    large = max_exact + (np.log(nf / np.float32(max_exact)) / np.float32(math.log(REL_MAX_DIST / max_exact))
                         * np.float32(nb - max_exact)).astype(np.int32)
    large = np.minimum(large, nb - 1)
    return (ret + np.where(n < max_exact, n, large)).astype(np.int32)


def _bias_tiles(rel_table):
    assert TK >= REL_MAX_DIST
    rel = np.stack([(np.arange(TK)[:, None] + (t - 1) * TK) - np.arange(TQ)[None, :] for t in range(2)])
    idx = jnp.asarray(_t5_bucket_np(rel))
    return pl.pallas_call(
        _bias_kernel,
        name="rel_bias",
        grid=(DIFF_HEADS,),
        in_specs=[pl.BlockSpec((2, TK, TQ), lambda h: (0, 0, 0)),
                  pl.BlockSpec(memory_space=pltpu.SMEM)],
        out_specs=pl.BlockSpec((1, 2, TK, TQ), lambda h: (h, 0, 0, 0)),
        out_shape=jax.ShapeDtypeStruct((DIFF_HEADS, 2, TK, TQ), F32),
        compiler_params=_params(1),
    )(idx, rel_table.astype(F32))


def _attention_call(kernel_fn, name, q, k, vt, extra, out_width, scratch, B, S, program=None):
    T = q.shape[0]
    smem = [] if program is None else [jnp.asarray(program)]
    seq = lambda a: pl.BlockSpec((S, a.shape[1]), lambda b: (b, 0))
    const = lambda a: pl.BlockSpec(a.shape, lambda b: (0,) * a.ndim, pipeline_mode=pl.Buffered(1))
    return pl.pallas_call(
        kernel_fn,
        name=name,
        grid=(B,),
        in_specs=[pl.BlockSpec(memory_space=pltpu.SMEM) for _ in smem]
        + [seq(q), seq(k), pl.BlockSpec((S // TK,) + vt.shape[1:], lambda b: (b, 0, 0))]
        + [const(a) for a in extra],
        out_specs=pl.BlockSpec((S, out_width), lambda b: (b, 0)),
        out_shape=jax.ShapeDtypeStruct((T, out_width), BF16),
        scratch_shapes=scratch,
        compiler_params=_params(1),
    )(*smem, q, k, vt, *extra)


def _softmax_scratch(heads, n, dv):
    return ([pltpu.VMEM((heads, 1, n), F32)] * 2 + [pltpu.VMEM((heads * (dv + ONES_ROWS), n), F32)]
            + [pltpu.VMEM((2, heads, 1, n), F32), pltpu.VMEM((2, heads * TK, n), F32)]
            + [pltpu.VMEM((heads * TK, n), BF16)])


def _mla_attention(qm, km, vmt, B, S):
    table, keys = _tile_program(S // TQ, lambda qi, kj: DIAG if kj == qi else FAR)
    return _attention_call(functools.partial(_mla_attn_kernel, keys), "mla_attn", qm, km, vmt, (),
                           MLA_HEADS * MLA_V, _softmax_scratch(MLA_HEADS, TQ, MLA_V), B, S, program=table)


def _sb_attention(qs, ks, vst, B, S):
    n = 2 * SB_HEADS * TK
    scratch = ([pltpu.VMEM((SB_HEADS, 1, TQ), F32), pltpu.VMEM((2 * SB_HEADS, 1, TQ), F32), pltpu.VMEM((SB_W, TQ), F32)]
               + [pltpu.VMEM((n, TQ), F32)] * 2 + [pltpu.VMEM((n, TQ), BF16)] * 2)
    return _attention_call(_sb_attn_kernel, "sb_attn", qs, ks, vst, (), SB_W, scratch, B, S)


def _diff_attention(qd, kd, vdt, bias, lamp, g, lam_init, B, S):
    table, keys = _tile_program(S // TQ, _tile_kind)
    return _attention_call(functools.partial(_diff_attn_kernel, lam_init, keys), "diff_attn", qd, kd, vdt,
                           (bias, lamp, g), DIFF_W,
                           _softmax_scratch(DIFF_HEADS, 2 * TQ, LANES) + [pltpu.VMEM((DIFF_HEADS * 2 * TQ, LANES), BF16)],
                           B, S,
                           program=table)


def _out0_in1_kernel(om_ref, os_ref, z_ref, x_ref, gate_ref, wo_ref, shift_ref, scale_ref, g_ref, w1_ref,
                     x1_ref, qd_ref, kd_ref, vdt_ref, zd_ref):
    z = z_ref[...].astype(F32)
    y = jnp.concatenate([om_ref[...].astype(F32), os_ref[...].astype(F32)], axis=1) * _silu(z)
    out = _dot(y.astype(BF16), wo_ref[...])
    x1 = x_ref[...] + gate_ref[0] * out
    x1_ref[...] = x1
    h = (_rms(x1, NORM_EPS) * g_ref[...] * (1.0 + scale_ref[0]) + shift_ref[0]).astype(BF16)
    qd_ref[...] = (_dot(h, w1_ref[:, 0:DIFF_W]) * (DIFF_HEAD_DIM ** -0.5 * LOG2E)).astype(BF16)
    kd_ref[...] = _dot(h, w1_ref[:, DIFF_W:2 * DIFF_W]).astype(BF16)
    _store_kv_tiles(vdt_ref, _dot(h, w1_ref[:, 2 * DIFF_W:3 * DIFF_W]).T)
    zd_ref[...] = _dot(h, w1_ref[:, 3 * DIFF_W:4 * DIFF_W]).astype(BF16)


def _layer0_out_layer1_in(om, osb, z, x2, mod3, wo, g1, w1, B):
    T, D = x2.shape
    nt = T // TM
    per_b = nt // B
    full = lambda a: pl.BlockSpec(a.shape, lambda i: (0,) * a.ndim)
    tok = lambda w: pl.BlockSpec((TM, w), lambda i: (i, 0))
    mod = lambda row: pl.BlockSpec((1, 1, D), lambda i: (row * B + i // per_b, 0, 0))
    tok_out = jax.ShapeDtypeStruct((T, DIFF_W), BF16)
    return pl.pallas_call(
        _out0_in1_kernel,
        name="l0_out_l1_in",
        grid=(nt,),
        in_specs=[tok(om.shape[1]), tok(osb.shape[1]), tok(D), tok(D), mod(2), full(wo),
                  mod(3), mod(4), full(g1), full(w1)],
        out_specs=[tok(D), tok(DIFF_W), tok(DIFF_W),
                   pl.BlockSpec((TM // TK, DIFF_W, TK), lambda i: (i, 0, 0)), tok(DIFF_W)],
        out_shape=[jax.ShapeDtypeStruct((T, D), F32), tok_out, tok_out,
                   jax.ShapeDtypeStruct((T // TK, DIFF_W, TK), BF16), tok_out],
        compiler_params=_params(1),
    )(om, osb, z, x2, mod3, wo, mod3, mod3, g1, w1)


def _out1_kernel(o_ref, z_ref, x_ref, gate_ref, wo_ref, g_ref, y_ref):
    y = o_ref[...].astype(F32) * _silu(z_ref[...].astype(F32))
    out = _dot(y.astype(BF16), wo_ref[...])
    x2 = x_ref[...] + gate_ref[0] * out
    y_ref[...] = _rms(x2, NORM_EPS) * g_ref[...]


def _layer1_out(od, zd, x1, mod3, wo, gf, B):
    T, D = x1.shape
    nt = T // TM_OUT
    per_b = nt // B
    full = lambda a: pl.BlockSpec(a.shape, lambda i: (0,) * a.ndim)
    tok = lambda w: pl.BlockSpec((TM_OUT, w), lambda i: (i, 0))
    return pl.pallas_call(
        _out1_kernel,
        name="l1_out",
        grid=(nt,),
        in_specs=[tok(D), tok(D), tok(D), pl.BlockSpec((1, 1, D), lambda i: (5 * B + i // per_b, 0, 0)),
                  full(wo), full(gf)],
        out_specs=tok(D),
        out_shape=jax.ShapeDtypeStruct((T, D), F32),
        compiler_params=_params(1),
    )(od, zd, x1, mod3, wo, gf)


def kernel(x, c, pos_offset, rel_bias_table, ada_w, ada_b, norm_g, final_g, ab_w_in, ab_q_norm_g, ab_kv_norm_g,
           ab_w_uq, ab_w_ukv, ab_w_out, dif_w_in, dif_lam_q1, dif_lam_k1, dif_lam_q2, dif_lam_k2, dif_subln_g,
           dif_w_out):
    B, S, D = x.shape
    assert D == D_MODEL and S % TQ == 0 and TQ == TK and S % TM == 0 and S % TM_OUT == 0 and TM % TK == 0
    x2 = x.reshape(B * S, D)

    mod3 = _modulation(c, ada_w, ada_b)

    w0, wqa, wk, wvt, wvst = _prep_ab_weights(ab_w_in[0], ab_w_uq[0], ab_w_ukv[0])
    qm, km, vmt, qs, ks, vst, z = _layer0_in(
        x2, pos_offset, mod3, norm_g[0:1], w0, wqa, wk, wvt, wvst, ab_q_norm_g[0:1], ab_kv_norm_g[0:1], B)
    om = _mla_attention(qm, km, vmt, B, S)
    osb = _sb_attention(qs, ks, vst, B, S)

    x1, qd, kd, vdt, zd = _layer0_out_layer1_in(
        om, osb, z, x2, mod3, ab_w_out[0].astype(BF16), norm_g[1:2], dif_w_in[0].astype(BF16), B)
    lam_init = 0.8 - 0.6 * math.exp(-0.3 * 1)
    lamp = jnp.stack([dif_lam_q1[0], dif_lam_k1[0], dif_lam_q2[0], dif_lam_k2[0]]).astype(F32)
    bias = _bias_tiles(rel_bias_table)
    od = _diff_attention(qd, kd, vdt, bias, lamp, dif_subln_g[0:1], lam_init, B, S)
    y = _layer1_out(od, zd, x1, mod3, dif_w_out[0].astype(BF16), final_g[None, :], B)
    return y.reshape(B, S, D)
```

```python
import functools
import math

import numpy as np
import jax
import jax.numpy as jnp
from jax import lax
from jax.experimental import pallas as pl
from jax.experimental.pallas import tpu as pltpu

D_MODEL = 1024
DEPTH = 2
CHUNK = 64

MLA_HEADS = 8
MLA_Q_LORA = 384
MLA_KV_LORA = 256
MLA_NOPE = 64
MLA_ROPE = 32
MLA_V = 64
SB_HEADS = 8
SB_HEAD_DIM = 64
SB_W = SB_HEADS * SB_HEAD_DIM
DIFF_HEADS = 8
DIFF_HEAD_DIM = 64
DIFF_W = DIFF_HEADS * 2 * DIFF_HEAD_DIM

REL_BUCKETS = 32
REL_MAX_DIST = 128
ROPE_THETA = 10000.0
NORM_EPS = 1e-6
SUBLN_EPS = 1e-5
NEG_INF = -1e30
SB_UNDERFLOW_LOG2 = 152.0
LOG2E = math.log2(math.e)

LANES = 128
V7X_VMEM_BYTES = 64 * 1024 * 1024
VMEM_LIMIT = V7X_VMEM_BYTES * 7 // 8

TM = 512
TM_OUT = 1024
TQ = 256
TK = 256
MLA_TQ = TK
SKEW = 0
ONES_ROWS = 16

F32 = jnp.float32
BF16 = jnp.bfloat16


def _silu(z):
    return z * (1.0 / (1.0 + jnp.exp(-z)))


def _dot(a, b):
    return jnp.dot(a, b, preferred_element_type=F32)


def _dot_nt(a, b):
    return lax.dot_general(a, b, (((1,), (1,)), ((), ())), preferred_element_type=F32)


def _rep(x, n):
    return x if n == 1 else jnp.concatenate([x] * n, axis=1)


def _params(n_axes=1):
    return pltpu.CompilerParams(dimension_semantics=("arbitrary",) * n_axes, vmem_limit_bytes=VMEM_LIMIT)


def _store_kv_tiles(ref, xt):
    for j in range(xt.shape[1] // TK):
        ref[j] = xt[:, j * TK:(j + 1) * TK].astype(ref.dtype)


def _mod_kernel(c_ref, w_ref, b_ref, o_ref):
    ca = _silu(c_ref[...]).astype(BF16)
    o_ref[0] = _dot(ca, w_ref[0].astype(BF16)) + b_ref[0]


def _modulation(c, ada_w, ada_b):
    B, D = c.shape
    out = pl.pallas_call(
        _mod_kernel,
        name="ada_mod",
        grid=(DEPTH, 3),
        in_specs=[
            pl.BlockSpec((B, D), lambda i, j: (0, 0)),
            pl.BlockSpec((1, D, D), lambda i, j: (i, 0, j)),
            pl.BlockSpec((1, 1, D), lambda i, j: (i * 3 + j, 0, 0)),
        ],
        out_specs=pl.BlockSpec((1, B, D), lambda i, j: (i * 3 + j, 0, 0)),
        out_shape=jax.ShapeDtypeStruct((DEPTH * 3, B, D), F32),
        compiler_params=_params(2),
    )(c, ada_w, ada_b.reshape(DEPTH * 3, 1, D))
    return out.reshape(DEPTH * 3 * B, 1, D)


def _rope_static_tables(S):
    half = MLA_ROPE // 2
    inv_freq = np.float32(ROPE_THETA) ** (-(np.arange(half, dtype=np.float32) / np.float32(half)))
    inv_freq = inv_freq.astype(np.float32)
    ang = np.arange(S, dtype=np.float64)[:, None] * inv_freq.astype(np.float64)[None, :]
    cos, sin = np.cos(ang), np.sin(ang)
    one = np.ones((S, MLA_NOPE))
    zn = np.zeros((S, MLA_NOPE))
    zp = np.zeros((S, LANES - MLA_NOPE - MLA_ROPE))
    tabs = [np.concatenate(parts, axis=1).astype(np.float32) for parts in (
        (one, cos, cos, zp), (zn, sin, sin, zp), (zn, -cos, cos, zp), (zn, -sin, sin, zp))]
    f = np.concatenate([np.zeros(MLA_NOPE, np.float32), inv_freq, inv_freq,
                        np.zeros(LANES - MLA_NOPE - MLA_ROPE, np.float32)])[None, :]
    return [jnp.asarray(t) for t in tabs], jnp.asarray(f)


def _rope_tiles(pos0, f_ref, cs_ref, ss_ref, cs2_ref, ss2_ref):
    ang0 = pos0.astype(F32) * f_ref[...]
    a, b = jnp.cos(ang0), jnp.sin(ang0)
    ctab = a * cs_ref[...] - b * ss_ref[...]
    stab = b * cs2_ref[...] + a * ss2_ref[...]
    return ctab, stab


AB_SEG = (MLA_Q_LORA, MLA_KV_LORA, SB_W, SB_W, MLA_HEADS * MLA_V + SB_W, LANES)
AB_OFF = tuple(int(v) for v in np.cumsum((0,) + AB_SEG))


def _rms(x, eps):
    return x * lax.rsqrt(jnp.mean(x * x, axis=-1, keepdims=True) + eps)


def _swap_rope_halves(v):
    half = MLA_ROPE // 2
    n = v.shape[1]
    lane = lax.broadcasted_iota(jnp.int32, v.shape, 1) % LANES
    return jnp.where(lane < MLA_NOPE + half, pltpu.roll(v, n - half, 1), pltpu.roll(v, half, 1))


def _in0_kernel(per_b, pos_ref, x_ref, shift_ref, scale_ref, g_ref, w_ref, wqa_ref, wk_ref, wvt_ref,
                wvst_ref, qg_ref, kvg_ref, f_ref, cs_ref, ss_ref, cs2_ref, ss2_ref,
                qm_ref, km_ref, vmt_ref, qs_ref, ks_ref, vst_ref, z_ref):
    x = x_ref[...]
    h = _rms(x, NORM_EPS) * (g_ref[...] * (1.0 + scale_ref[0])) + shift_ref[0]
    h = h.astype(BF16)

    def seg(i):
        return _dot(h, w_ref[:, AB_OFF[i]:AB_OFF[i + 1]])

    ctab, stab = _rope_tiles(pos_ref[pl.program_id(0) // per_b], f_ref, cs_ref, ss_ref, cs2_ref, ss2_ref)
    nh = MLA_HEADS

    cq = (_rms(seg(0), NORM_EPS) * qg_ref[...]).astype(BF16)
    ckv = (_rms(seg(1), NORM_EPS) * kvg_ref[...]).astype(BF16)
    q = _dot(cq, wqa_ref[...])
    z_ref[...] = seg(4).astype(BF16)
    q = q * _rep(ctab, nh) + _swap_rope_halves(q) * _rep(stab, nh)
    qm_ref[...] = (q * ((MLA_NOPE + MLA_ROPE) ** -0.5 * LOG2E)).astype(BF16)

    kr = seg(5)
    kn = _dot(ckv, wk_ref[...])
    qs_ref[...] = (seg(2) * (SB_HEAD_DIM ** -0.5 * LOG2E)).astype(BF16)
    krope = kr * ctab + _swap_rope_halves(kr) * stab
    km_ref[...] = (kn + _rep(krope, nh)).astype(BF16)
    vmt = _dot(ckv, wvt_ref[...])
    ks_ref[...] = seg(3).astype(BF16)
    _store_kv_tiles(vmt_ref, vmt.T)
    _store_kv_tiles(vst_ref, _dot(h, wvst_ref[...]).T)


def _prep_ab_weights(w_in, w_uq, w_ukv):
    D = w_in.shape[0]
    c = np.cumsum([MLA_Q_LORA, MLA_KV_LORA, MLA_ROPE, SB_W, SB_W, SB_W]).tolist()
    w_kr = w_in[:, c[1]:c[2]]
    zl = jnp.zeros((D, MLA_NOPE), w_in.dtype)
    zr = jnp.zeros((D, LANES - MLA_NOPE - MLA_ROPE), w_in.dtype)
    kr_a = jnp.concatenate([zl, w_kr, zr], axis=1)
    w0 = jnp.concatenate([
        w_in[:, :c[1]],
        w_in[:, c[2]:c[3]],
        w_in[:, c[3]:c[4]],
        w_in[:, c[5]:],
        kr_a], axis=1).astype(BF16)
    wvst = w_in[:, c[4]:c[5]].astype(BF16)

    hq = MLA_NOPE + MLA_ROPE
    uq = w_uq.reshape(MLA_Q_LORA, MLA_HEADS, hq)
    pad = jnp.zeros((MLA_Q_LORA, MLA_HEADS, LANES - hq), w_uq.dtype)
    wqa = jnp.concatenate([uq, pad], axis=2).reshape(MLA_Q_LORA, MLA_HEADS * LANES).astype(BF16)
    ukv = w_ukv.reshape(MLA_KV_LORA, MLA_HEADS, MLA_NOPE + MLA_V)
    zk = jnp.zeros((MLA_KV_LORA, MLA_HEADS, LANES - MLA_NOPE), w_ukv.dtype)
    wk = jnp.concatenate([ukv[:, :, :MLA_NOPE], zk], axis=2).reshape(MLA_KV_LORA, MLA_HEADS * LANES).astype(BF16)
    wvt = ukv[:, :, MLA_NOPE:].reshape(MLA_KV_LORA, MLA_HEADS * MLA_V).astype(BF16)
    return w0, wqa, wk, wvt, wvst


def _layer0_in(x2, pos_offset, mod3, g, w0, wqa, wk, wvt, wvst, qg, kvg, B):
    T, D = x2.shape
    nt = T // TM
    per_b = nt // B
    rope_tabs, rope_f = _rope_static_tables(T // B)
    full = lambda a: pl.BlockSpec(a.shape, lambda i: (0,) * a.ndim)
    tok = lambda w: pl.BlockSpec((TM, w), lambda i: (i, 0))
    seq = pl.BlockSpec((TM, LANES), lambda i: (i % per_b, 0))
    kvt = lambda w: pl.BlockSpec((TM // TK, w, TK), lambda i: (i, 0, 0))
    tok_out = lambda w: jax.ShapeDtypeStruct((T, w), BF16)
    kvt_out = lambda w: jax.ShapeDtypeStruct((T // TK, w, TK), BF16)
    hv = MLA_HEADS * MLA_V
    return pl.pallas_call(
        functools.partial(_in0_kernel, per_b),
        name="l0_in",
        grid=(nt,),
        in_specs=[
            pl.BlockSpec(memory_space=pltpu.SMEM),
            tok(D),
            pl.BlockSpec((1, 1, D), lambda i: (0 * B + i // per_b, 0, 0)),
            pl.BlockSpec((1, 1, D), lambda i: (1 * B + i // per_b, 0, 0)),
            full(g), full(w0), full(wqa), full(wk), full(wvt), full(wvst), full(qg), full(kvg),
            full(rope_f), seq, seq, seq, seq,
        ],
        out_specs=[tok(MLA_HEADS * LANES), tok(MLA_HEADS * LANES), kvt(hv), tok(SB_W), tok(SB_W), kvt(SB_W),
                   tok(hv + SB_W)],
        out_shape=[tok_out(MLA_HEADS * LANES), tok_out(MLA_HEADS * LANES), kvt_out(hv), tok_out(SB_W),
                   tok_out(SB_W), kvt_out(SB_W), tok_out(hv + SB_W)],
        compiler_params=_params(1),
    )(pos_offset, x2, mod3, mod3, g, w0, wqa, wk, wvt, wvst, qg, kvg, rope_f, *rope_tabs)


def _chunk_mask_t():
    k = lax.broadcasted_iota(jnp.int32, (TK, TQ), 0)
    q = lax.broadcasted_iota(jnp.int32, (TK, TQ), 1)
    shift = CHUNK.bit_length() - 1
    return (k >> shift) <= (q >> shift)


def _tile_start(i, size):
    return i * size if isinstance(i, int) else pl.multiple_of(i * size, size)


FAR, NEAR, DIAG = 0, 1, 2
PROGRAM_ROW = 5


def _tile_kind(qi, kj):
    return DIAG if kj == qi else NEAR if kj == qi - 1 else FAR


def _tile_program(n_tiles, kind_of, kv_per_q=1):
    items = [(qi, kj) for qi in range(n_tiles) for kj in range(kv_per_q * (qi + 1))]
    assert len(items) % 2 == 0 and n_tiles >= 2
    items.append((n_tiles - 1, 0))
    kinds = [kind_of(qi, kj) for qi, kj in items]
    keys, rows = [], []
    for t in range(len(items) - 1):
        key = (kinds[t], kinds[t + 1])
        if key not in keys:
            keys.append(key)
        rows.append([keys.index(key), *items[t], *items[t + 1]])
    return np.asarray(rows, np.int32).reshape(-1), keys


def _run_tile_program(tab_ref, keys, step, finish, prepare=None):
    def variant(key, slot):
        def run(qi, kj, nqi, nkj):
            if key[0] == DIAG and prepare is not None:
                prepare(nqi)
            step(qi, kj, nqi, nkj, slot, key[1])
            if key[0] == DIAG:
                finish(qi)
        return run

    variants = [[variant(key, slot) for key in keys] for slot in range(2)]

    def trip(t, carry):
        for slot in range(2):
            base = (2 * t + slot) * PROGRAM_ROW
            lax.switch(tab_ref[base], variants[slot], *[tab_ref[base + 1 + j] for j in range(PROGRAM_ROW - 1)])
        return carry

    lax.fori_loop(0, tab_ref.shape[0] // (2 * PROGRAM_ROW), trip, 0)


def _softmax_step(s_ref, mc_ref, m_ref, alpha_ref, p_ref, slot, h, rows):
    m_prev = m_ref[h]
    m_new = jnp.maximum(m_prev, mc_ref[slot, h])
    p_ref[rows, :] = jnp.exp2(s_ref[slot, rows, :] - m_new).astype(BF16)
    alpha_ref[h] = jnp.exp2(m_prev - m_new)
    m_ref[h] = m_new


def _values_lhs(vt):
    return jnp.concatenate([vt, jnp.ones((ONES_ROWS, vt.shape[1]), vt.dtype)], axis=0)


def _softmax_init(m_ref, acc_ref):
    m_ref[...] = jnp.full(m_ref.shape, -jnp.inf, F32)
    acc_ref[...] = jnp.zeros(acc_ref.shape, F32)


def _mla_attn_kernel(keys, tab_ref, q_ref, k_ref, vt_ref, o_ref, m_ref, alpha_ref, acc_ref, mc_ref, s_ref, p_ref):
    S = q_ref.shape[0]
    shift = CHUNK.bit_length() - 1
    kc = lax.broadcasted_iota(jnp.int32, (TK, MLA_TQ), 0) >> shift
    qc = lax.broadcasted_iota(jnp.int32, (TK, MLA_TQ), 1) >> shift
    per_q = MLA_TQ // TK
    masks = {DIAG: kc + (per_q - 1) * (TK >> shift) <= qc, NEAR: kc <= qc}
    krows = [pl.ds(h * TK, TK) for h in range(MLA_HEADS)]
    vrows = [pl.ds(h * MLA_V, MLA_V) for h in range(MLA_HEADS)]
    arows = [pl.ds(h * (MLA_V + ONES_ROWS), MLA_V + ONES_ROWS) for h in range(MLA_HEADS)]

    def out_t(h):
        base = h * (MLA_V + ONES_ROWS)
        return acc_ref[pl.ds(base, MLA_V), :] * (1.0 / acc_ref[pl.ds(base + MLA_V, 1), :])

    def scores_head(h, qi, kj, slot, kind):
        q0 = _tile_start(qi, MLA_TQ)
        k0 = _tile_start(kj, TK)
        hl = slice(h * LANES, (h + 1) * LANES)
        s = _dot_nt(k_ref[pl.ds(k0, TK), hl], q_ref[pl.ds(q0, MLA_TQ), hl])
        if kind != FAR:
            s = jnp.where(masks[kind], s, NEG_INF)
        s_ref[slot, krows[h], :] = s
        mc_ref[slot, h] = jnp.max(s, axis=0, keepdims=True)

    def scores(qi, kj, slot, kind):
        for h in range(MLA_HEADS):
            scores_head(h, qi, kj, slot, kind)

    def values_head(h, kj):
        acc_ref[arows[h], :] = (alpha_ref[h] * acc_ref[arows[h], :]
                                + _dot(_values_lhs(vt_ref[kj, vrows[h], :]), p_ref[krows[h], :]))

    def step(qi, kj, nqi, nkj, slot, next_kind):
        for h in range(SKEW):
            scores_head(h, nqi, nkj, 1 - slot, next_kind)
        for h in range(MLA_HEADS):
            if h + SKEW < MLA_HEADS:
                scores_head(h + SKEW, nqi, nkj, 1 - slot, next_kind)
            _softmax_step(s_ref, mc_ref, m_ref, alpha_ref, p_ref, slot, h, krows[h])
            if h > 0:
                values_head(h - 1, kj)
        values_head(MLA_HEADS - 1, kj)

    def finish(qi):
        q0 = pl.multiple_of(qi * MLA_TQ, MLA_TQ)
        for j in range(MLA_HEADS // 2):
            ot = jnp.concatenate([out_t(2 * j), out_t(2 * j + 1)], axis=0)
            o_ref[pl.ds(q0, MLA_TQ), j * LANES:(j + 1) * LANES] = ot.T.astype(BF16)
        _softmax_init(m_ref, acc_ref)

    _softmax_init(m_ref, acc_ref)
    scores(0, 0, 0, NEAR if per_q > 1 else DIAG)
    _run_tile_program(tab_ref, keys, step, finish)


def _sb_attn_kernel(q_ref, k_ref, vt_ref, o_ref, r_ref, rs_ref, acc_ref, z_ref, ls_ref, hi_ref, lo_ref):
    S = q_ref.shape[0]
    k_i = lax.broadcasted_iota(jnp.int32, (TK, TQ), 0)
    q_i = lax.broadcasted_iota(jnp.int32, (TK, TQ), 1)
    causal = k_i < q_i
    tri = jnp.where(lax.broadcasted_iota(jnp.int32, (TK, TK), 1) > lax.broadcasted_iota(jnp.int32, (TK, TK), 0),
                    1.0, 0.0).astype(BF16)
    lane = lax.broadcasted_iota(jnp.int32, (TQ, LANES), 1)
    first = lane < SB_HEAD_DIM
    crows = [pl.ds(c * TK, TK) for c in range(2 * SB_HEADS)]
    vrows = [pl.ds(h * SB_HEAD_DIM, SB_HEAD_DIM) for h in range(SB_HEADS)]
    pairs = [slice((h // 2) * LANES, (h // 2 + 1) * LANES) for h in range(SB_HEADS)]

    def scores_chain(c, h, q0, kj):
        k0 = pl.multiple_of(kj * TK, TK)
        q = q_ref[pl.ds(q0, TQ), pairs[h]]
        q = jnp.where(first, q, jnp.zeros_like(q)) if h % 2 == 0 else jnp.where(first, jnp.zeros_like(q), q)
        z_ref[crows[c], :] = _dot_nt(k_ref[pl.ds(k0, TK), pairs[h]], q)

    def logs_chain(c, masked):
        z = z_ref[crows[c], :]
        sp = jnp.log(1.0 + jnp.exp2(-jnp.abs(z))) * LOG2E
        ls = jnp.minimum(z, 0.0) - sp
        l1m = ls - z
        if masked:
            l1m = jnp.where(causal, l1m, 0.0)
        hi = l1m.astype(BF16)
        ls_ref[crows[c], :] = ls
        hi_ref[crows[c], :] = hi
        lo_ref[crows[c], :] = (l1m - hi.astype(F32)).astype(BF16)
        rs_ref[c] = l1m[0:1, :]

    def suffix_chain(c):
        z_ref[crows[c], :] = _dot(tri, hi_ref[crows[c], :]) + _dot(tri, lo_ref[crows[c], :])

    def weights_chain(c, h, masked):
        r_prev = r_ref[h]
        suffix = z_ref[crows[c], :]
        w = jnp.exp2(ls_ref[crows[c], :] + suffix + r_prev)
        if masked:
            w = jnp.where(causal, w, 0.0)
        hi_ref[crows[c], :] = w.astype(BF16)
        r_ref[h] = r_prev + (rs_ref[c] + suffix[0:1, :])

    def values_chain(c, h, kj):
        acc_ref[vrows[h], :] += _dot(vt_ref[kj, vrows[h], :], hi_ref[crows[c], :])

    def step(q0, tiles):
        chains = [(h, kj, masked) for h in range(SB_HEADS) for kj, masked in tiles]
        n = len(chains)
        lead, lag = 2, 2
        for c in range(lead):
            scores_chain(c, chains[c][0], q0, chains[c][1])
        for t in range(n + lag):
            if t + lead < n:
                scores_chain(t + lead, chains[t + lead][0], q0, chains[t + lead][1])
            if t < n:
                logs_chain(t, chains[t][2])
                suffix_chain(t)
            if t >= lag:
                h, kj, masked = chains[t - lag]
                weights_chain(t - lag, h, masked)
                values_chain(t - lag, h, kj)

    def start():
        r_ref[...] = jnp.zeros(r_ref.shape, F32)
        acc_ref[...] = jnp.zeros(acc_ref.shape, F32)

    def finish(q0):
        for j in range(SB_HEADS // 2):
            ot = acc_ref[pl.ds(j * LANES, LANES), :]
            o_ref[pl.ds(q0, TQ), j * LANES:(j + 1) * LANES] = ot.T.astype(BF16)

    def still_live():
        return (jnp.max(r_ref[...]) > -SB_UNDERFLOW_LOG2).astype(jnp.int32)

    start()
    step(0, [(0, True)])
    finish(0)

    def q_body(qi, carry):
        q0 = pl.multiple_of(qi * TQ, TQ)
        start()
        step(q0, [(qi, True), (qi - 1, False)])

        def cond(c):
            t, live = c
            return jnp.logical_and(t < qi, live > 0)

        def body(c):
            t, _ = c
            step(q0, [(qi - 1 - t, False)])
            return t + 1, still_live()

        lax.while_loop(cond, body, (jnp.int32(1), still_live()))
        finish(q0)
        return carry

    lax.fori_loop(1, S // TQ, q_body, 0)


def _diff_attn_kernel(lam_init, keys, tab_ref, q_ref, k_ref, vt_ref, bias_ref, lamp_ref, g_ref, o_ref,
                      m_ref, alpha_ref, acc_ref, mc_ref, s_ref, p_ref, q2_ref):
    S = q_ref.shape[0]
    mask = _chunk_mask_t()
    mask2 = jnp.concatenate([mask, mask], axis=1)
    lane = lax.broadcasted_iota(jnp.int32, (TQ, LANES), 1)
    first = lane < DIFF_HEAD_DIM
    lp = lamp_ref[...]
    lam = (jnp.exp(jnp.sum(lp[0:1] * lp[1:2], axis=-1, keepdims=True))
           - jnp.exp(jnp.sum(lp[2:3] * lp[3:4], axis=-1, keepdims=True)) + lam_init)
    krows = [pl.ds(h * TK, TK) for h in range(DIFF_HEADS)]
    vrows = [pl.ds(h * LANES, LANES) for h in range(DIFF_HEADS)]
    arows = [pl.ds(h * (LANES + ONES_ROWS), LANES + ONES_ROWS) for h in range(DIFF_HEADS)]
    gain = g_ref[...] * (1.0 - lam_init)

    def scores_head(h, qi, kj, slot, kind):
        q0 = _tile_start(qi, TQ)
        k0 = _tile_start(kj, TK)
        hl = slice(h * LANES, (h + 1) * LANES)
        s = _dot_nt(k_ref[pl.ds(k0, TK), hl], q2_ref[pl.ds(h * 2 * TQ, 2 * TQ), :])
        if kind != FAR:
            s = s + _rep(bias_ref[h, kind - 1], 2)
        if kind == DIAG:
            s = jnp.where(mask2, s, NEG_INF)
        s_ref[slot, krows[h], :] = s
        mc_ref[slot, h] = jnp.max(s, axis=0, keepdims=True)

    def scores(qi, kj, slot, kind):
        for h in range(DIFF_HEADS):
            scores_head(h, qi, kj, slot, kind)

    def values_head(h, kj):
        acc_ref[arows[h], :] = (alpha_ref[h] * acc_ref[arows[h], :]
                                + _dot(_values_lhs(vt_ref[kj, vrows[h], :]), p_ref[krows[h], :]))

    def step(qi, kj, nqi, nkj, slot, next_kind):
        for h in range(SKEW):
            scores_head(h, nqi, nkj, 1 - slot, next_kind)
        for h in range(DIFF_HEADS):
            if h + SKEW < DIFF_HEADS:
                scores_head(h + SKEW, nqi, nkj, 1 - slot, next_kind)
            _softmax_step(s_ref, mc_ref, m_ref, alpha_ref, p_ref, slot, h, krows[h])
            if h > 0:
                values_head(h - 1, kj)
        values_head(DIFF_HEADS - 1, kj)

    def finish(qi):
        q0 = pl.multiple_of(qi * TQ, TQ)
        for h in range(DIFF_HEADS):
            base = h * (LANES + ONES_ROWS)
            inv_l = 1.0 / acc_ref[pl.ds(base + LANES, 1), :]
            ot = (acc_ref[pl.ds(base, LANES), pl.ds(0, TQ)] * inv_l[:, :TQ]
                  - acc_ref[pl.ds(base, LANES), pl.ds(TQ, TQ)] * (lam * inv_l[:, TQ:]))
            ot = ot * lax.rsqrt(jnp.mean(ot * ot, axis=0, keepdims=True) + SUBLN_EPS)
            o_ref[pl.ds(q0, TQ), h * LANES:(h + 1) * LANES] = (ot.T * gain).astype(BF16)
        _softmax_init(m_ref, acc_ref)

    def split_queries(qi):
        q0 = _tile_start(qi, TQ)
        for h in range(DIFF_HEADS):
            q = q_ref[pl.ds(q0, TQ), h * LANES:(h + 1) * LANES]
            zero = jnp.zeros_like(q)
            q2_ref[pl.ds(h * 2 * TQ, TQ), :] = jnp.where(first, q, zero)
            q2_ref[pl.ds(h * 2 * TQ + TQ, TQ), :] = jnp.where(first, zero, q)

    _softmax_init(m_ref, acc_ref)
    split_queries(0)
    scores(0, 0, 0, DIAG)
    _run_tile_program(tab_ref, keys, step, finish, prepare=split_queries)


def _bias_kernel(idx_ref, tab_ref, o_ref):
    h = pl.program_id(0)
    far = tab_ref[REL_BUCKETS // 2 - 1, h]
    for t in range(2):
        idx = idx_ref[t]
        acc = jnp.zeros(idx.shape, F32)
        for b in range(REL_BUCKETS):
            acc = jnp.where(idx == b, (tab_ref[b, h] - far) * LOG2E, acc)
        o_ref[0, t] = acc


def _t5_bucket_np(rel):
    nb = REL_BUCKETS // 2
    max_exact = nb // 2
    ret = np.where(rel > 0, nb, 0)
    n = np.abs(rel)
    nf = np.maximum(n, 1).astype(np.float32)
    large = max_exact + (np.log(nf / np.float32(max_exact)) / np.float32(math.log(REL_MAX_DIST / max_exact))
                         * np.float32(nb - max_exact)).astype(np.int32)
    large = np.minimum(large, nb - 1)
    return (ret + np.where(n < max_exact, n, large)).astype(np.int32)


def _bias_tiles(rel_table):
    assert TK >= REL_MAX_DIST
    rel = np.stack([(np.arange(TK)[:, None] + (t - 1) * TK) - np.arange(TQ)[None, :] for t in range(2)])
    idx = jnp.asarray(_t5_bucket_np(rel))
    return pl.pallas_call(
        _bias_kernel,
        name="rel_bias",
        grid=(DIFF_HEADS,),
        in_specs=[pl.BlockSpec((2, TK, TQ), lambda h: (0, 0, 0)),
                  pl.BlockSpec(memory_space=pltpu.SMEM)],
        out_specs=pl.BlockSpec((1, 2, TK, TQ), lambda h: (h, 0, 0, 0)),
        out_shape=jax.ShapeDtypeStruct((DIFF_HEADS, 2, TK, TQ), F32),
        compiler_params=_params(1),
    )(idx, rel_table.astype(F32))


def _attention_call(kernel_fn, name, q, k, vt, extra, out_width, scratch, B, S, program=None):
    T = q.shape[0]
    smem = [] if program is None else [jnp.asarray(program)]
    seq = lambda a: pl.BlockSpec((S, a.shape[1]), lambda b: (b, 0))
    const = lambda a: pl.BlockSpec(a.shape, lambda b: (0,) * a.ndim, pipeline_mode=pl.Buffered(1))
    return pl.pallas_call(
        kernel_fn,
        name=name,
        grid=(B,),
        in_specs=[pl.BlockSpec(memory_space=pltpu.SMEM) for _ in smem]
        + [seq(q), seq(k), pl.BlockSpec((S // TK,) + vt.shape[1:], lambda b: (b, 0, 0))]
        + [const(a) for a in extra],
        out_specs=pl.BlockSpec((S, out_width), lambda b: (b, 0)),
        out_shape=jax.ShapeDtypeStruct((T, out_width), BF16),
        scratch_shapes=scratch,
        compiler_params=_params(1),
    )(*smem, q, k, vt, *extra)


def _softmax_scratch(heads, n, dv):
    return ([pltpu.VMEM((heads, 1, n), F32)] * 2 + [pltpu.VMEM((heads * (dv + ONES_ROWS), n), F32)]
            + [pltpu.VMEM((2, heads, 1, n), F32), pltpu.VMEM((2, heads * TK, n), F32)]
            + [pltpu.VMEM((heads * TK, n), BF16)])


def _mla_attention(qm, km, vmt, B, S):
    per_q = MLA_TQ // TK
    kind = lambda qi, kj: DIAG if kj == per_q * (qi + 1) - 1 else NEAR if kj == per_q * qi else FAR
    table, keys = _tile_program(S // MLA_TQ, kind, kv_per_q=per_q)
    return _attention_call(functools.partial(_mla_attn_kernel, keys), "mla_attn", qm, km, vmt, (),
                           MLA_HEADS * MLA_V, _softmax_scratch(MLA_HEADS, MLA_TQ, MLA_V), B, S, program=table)


def _sb_attention(qs, ks, vst, B, S):
    n = 2 * SB_HEADS * TK
    scratch = ([pltpu.VMEM((SB_HEADS, 1, TQ), F32), pltpu.VMEM((2 * SB_HEADS, 1, TQ), F32), pltpu.VMEM((SB_W, TQ), F32)]
               + [pltpu.VMEM((n, TQ), F32)] * 2 + [pltpu.VMEM((n, TQ), BF16)] * 2)
    return _attention_call(_sb_attn_kernel, "sb_attn", qs, ks, vst, (), SB_W, scratch, B, S)


def _diff_attention(qd, kd, vdt, bias, lamp, g, lam_init, B, S):
    table, keys = _tile_program(S // TQ, _tile_kind)
    return _attention_call(functools.partial(_diff_attn_kernel, lam_init, keys), "diff_attn", qd, kd, vdt,
                           (bias, lamp, g), DIFF_W,
                           _softmax_scratch(DIFF_HEADS, 2 * TQ, LANES) + [pltpu.VMEM((DIFF_HEADS * 2 * TQ, LANES), BF16)],
                           B, S,
                           program=table)


def _out0_in1_kernel(om_ref, os_ref, z_ref, x_ref, gate_ref, wo_ref, shift_ref, scale_ref, g_ref, w1_ref,
                     x1_ref, qd_ref, kd_ref, vdt_ref, zd_ref):
    z = z_ref[...].astype(F32)
    y = jnp.concatenate([om_ref[...].astype(F32), os_ref[...].astype(F32)], axis=1) * _silu(z)
    out = _dot(y.astype(BF16), wo_ref[...])
    x1 = x_ref[...] + gate_ref[0] * out
    x1_ref[...] = x1
    h = (_rms(x1, NORM_EPS) * (g_ref[...] * (1.0 + scale_ref[0])) + shift_ref[0]).astype(BF16)
    qd_ref[...] = (_dot(h, w1_ref[:, 0:DIFF_W]) * (DIFF_HEAD_DIM ** -0.5 * LOG2E)).astype(BF16)
    kd_ref[...] = _dot(h, w1_ref[:, DIFF_W:2 * DIFF_W]).astype(BF16)
    _store_kv_tiles(vdt_ref, _dot(h, w1_ref[:, 2 * DIFF_W:3 * DIFF_W]).T)
    zd_ref[...] = _dot(h, w1_ref[:, 3 * DIFF_W:4 * DIFF_W]).astype(BF16)


def _layer0_out_layer1_in(om, osb, z, x2, mod3, wo, g1, w1, B):
    T, D = x2.shape
    nt = T // TM
    per_b = nt // B
    full = lambda a: pl.BlockSpec(a.shape, lambda i: (0,) * a.ndim)
    tok = lambda w: pl.BlockSpec((TM, w), lambda i: (i, 0))
    mod = lambda row: pl.BlockSpec((1, 1, D), lambda i: (row * B + i // per_b, 0, 0))
    tok_out = jax.ShapeDtypeStruct((T, DIFF_W), BF16)
    return pl.pallas_call(
        _out0_in1_kernel,
        name="l0_out_l1_in",
        grid=(nt,),
        in_specs=[tok(om.shape[1]), tok(osb.shape[1]), tok(D), tok(D), mod(2), full(wo),
                  mod(3), mod(4), full(g1), full(w1)],
        out_specs=[tok(D), tok(DIFF_W), tok(DIFF_W),
                   pl.BlockSpec((TM // TK, DIFF_W, TK), lambda i: (i, 0, 0)), tok(DIFF_W)],
        out_shape=[jax.ShapeDtypeStruct((T, D), F32), tok_out, tok_out,
                   jax.ShapeDtypeStruct((T // TK, DIFF_W, TK), BF16), tok_out],
        compiler_params=_params(1),
    )(om, osb, z, x2, mod3, wo, mod3, mod3, g1, w1)


def _out1_kernel(o_ref, z_ref, x_ref, gate_ref, wo_ref, g_ref, y_ref):
    y = o_ref[...].astype(F32) * _silu(z_ref[...].astype(F32))
    out = _dot(y.astype(BF16), wo_ref[...])
    x2 = x_ref[...] + gate_ref[0] * out
    y_ref[...] = _rms(x2, NORM_EPS) * g_ref[...]


def _layer1_out(od, zd, x1, mod3, wo, gf, B):
    T, D = x1.shape
    nt = T // TM_OUT
    per_b = nt // B
    full = lambda a: pl.BlockSpec(a.shape, lambda i: (0,) * a.ndim)
    tok = lambda w: pl.BlockSpec((TM_OUT, w), lambda i: (i, 0))
    return pl.pallas_call(
        _out1_kernel,
        name="l1_out",
        grid=(nt,),
        in_specs=[tok(D), tok(D), tok(D), pl.BlockSpec((1, 1, D), lambda i: (5 * B + i // per_b, 0, 0)),
                  full(wo), full(gf)],
        out_specs=tok(D),
        out_shape=jax.ShapeDtypeStruct((T, D), F32),
        compiler_params=_params(1),
    )(od, zd, x1, mod3, wo, gf)


def kernel(x, c, pos_offset, rel_bias_table, ada_w, ada_b, norm_g, final_g, ab_w_in, ab_q_norm_g, ab_kv_norm_g,
           ab_w_uq, ab_w_ukv, ab_w_out, dif_w_in, dif_lam_q1, dif_lam_k1, dif_lam_q2, dif_lam_k2, dif_subln_g,
           dif_w_out):
    B, S, D = x.shape
    assert D == D_MODEL and S % TQ == 0 and TQ == TK and S % TM == 0 and S % TM_OUT == 0 and TM % TK == 0
    x2 = x.reshape(B * S, D)

    mod3 = _modulation(c, ada_w, ada_b)

    w0, wqa, wk, wvt, wvst = _prep_ab_weights(ab_w_in[0], ab_w_uq[0], ab_w_ukv[0])
    qm, km, vmt, qs, ks, vst, z = _layer0_in(
        x2, pos_offset, mod3, norm_g[0:1], w0, wqa, wk, wvt, wvst, ab_q_norm_g[0:1], ab_kv_norm_g[0:1], B)
    om = _mla_attention(qm, km, vmt, B, S)
    osb = _sb_attention(qs, ks, vst, B, S)

    x1, qd, kd, vdt, zd = _layer0_out_layer1_in(
        om, osb, z, x2, mod3, ab_w_out[0].astype(BF16), norm_g[1:2], dif_w_in[0].astype(BF16), B)
    lam_init = 0.8 - 0.6 * math.exp(-0.3 * 1)
    lamp = jnp.stack([dif_lam_q1[0], dif_lam_k1[0], dif_lam_q2[0], dif_lam_k2[0]]).astype(F32)
    bias = _bias_tiles(rel_bias_table)
    od = _diff_attention(qd, kd, vdt, bias, lamp, dif_subln_g[0:1], lam_init, B, S)
    y = _layer1_out(od, zd, x1, mod3, dif_w_out[0].astype(BF16), final_g[None, :], B)
    return y.reshape(B, S, D)
```

```python
import functools
import math

import numpy as np
import jax
import jax.numpy as jnp
from jax import lax
from jax.experimental import pallas as pl
from jax.experimental.pallas import tpu as pltpu

D_MODEL = 1024
DEPTH = 2
CHUNK = 64

MLA_HEADS = 8
MLA_Q_LORA = 384
MLA_KV_LORA = 256
MLA_NOPE = 64
MLA_ROPE = 32
MLA_V = 64
SB_HEADS = 8
SB_HEAD_DIM = 64
SB_W = SB_HEADS * SB_HEAD_DIM
DIFF_HEADS = 8
DIFF_HEAD_DIM = 64
DIFF_W = DIFF_HEADS * 2 * DIFF_HEAD_DIM

REL_BUCKETS = 32
REL_MAX_DIST = 128
ROPE_THETA = 10000.0
NORM_EPS = 1e-6
SUBLN_EPS = 1e-5
NEG_INF = -1e30
SB_UNDERFLOW_LOG2 = 152.0
LOG2E = math.log2(math.e)

LANES = 128
V7X_VMEM_BYTES = 64 * 1024 * 1024
VMEM_LIMIT = V7X_VMEM_BYTES * 7 // 8

TM = 512
TM_OUT = 1024
TQ = 256
TK = 256
MLA_TQ = TK
SKEW = 0
ONES_ROWS = 16

F32 = jnp.float32
BF16 = jnp.bfloat16


def _silu(z):
    return z * (1.0 / (1.0 + jnp.exp(-z)))


def _dot(a, b):
    return jnp.dot(a, b, preferred_element_type=F32)


def _dot_nt(a, b):
    return lax.dot_general(a, b, (((1,), (1,)), ((), ())), preferred_element_type=F32)


def _rep(x, n):
    return x if n == 1 else jnp.concatenate([x] * n, axis=1)


def _params(n_axes=1):
    return pltpu.CompilerParams(dimension_semantics=("arbitrary",) * n_axes, vmem_limit_bytes=VMEM_LIMIT)


def _store_kv_tiles(ref, xt):
    for j in range(xt.shape[1] // TK):
        ref[j] = xt[:, j * TK:(j + 1) * TK].astype(ref.dtype)


def _mod_kernel(c_ref, w_ref, b_ref, o_ref):
    ca = _silu(c_ref[...]).astype(BF16)
    o_ref[0] = _dot(ca, w_ref[0].astype(BF16)) + b_ref[0]


def _modulation(c, ada_w, ada_b):
    B, D = c.shape
    out = pl.pallas_call(
        _mod_kernel,
        name="ada_mod",
        grid=(DEPTH, 3),
        in_specs=[
            pl.BlockSpec((B, D), lambda i, j: (0, 0)),
            pl.BlockSpec((1, D, D), lambda i, j: (i, 0, j)),
            pl.BlockSpec((1, 1, D), lambda i, j: (i * 3 + j, 0, 0)),
        ],
        out_specs=pl.BlockSpec((1, B, D), lambda i, j: (i * 3 + j, 0, 0)),
        out_shape=jax.ShapeDtypeStruct((DEPTH * 3, B, D), F32),
        compiler_params=_params(2),
    )(c, ada_w, ada_b.reshape(DEPTH * 3, 1, D))
    return out.reshape(DEPTH * 3 * B, 1, D)


def _rope_static_tables(S):
    half = MLA_ROPE // 2
    inv_freq = np.float32(ROPE_THETA) ** (-(np.arange(half, dtype=np.float32) / np.float32(half)))
    inv_freq = inv_freq.astype(np.float32)
    ang = np.arange(S, dtype=np.float64)[:, None] * inv_freq.astype(np.float64)[None, :]
    cos, sin = np.cos(ang), np.sin(ang)
    one = np.ones((S, MLA_NOPE))
    zn = np.zeros((S, MLA_NOPE))
    zp = np.zeros((S, LANES - MLA_NOPE - MLA_ROPE))
    tabs = [np.concatenate(parts, axis=1).astype(np.float32) for parts in (
        (one, cos, cos, zp), (zn, sin, sin, zp), (zn, -cos, cos, zp), (zn, -sin, sin, zp))]
    f = np.concatenate([np.zeros(MLA_NOPE, np.float32), inv_freq, inv_freq,
                        np.zeros(LANES - MLA_NOPE - MLA_ROPE, np.float32)])[None, :]
    return [jnp.asarray(t) for t in tabs], jnp.asarray(f)


def _rope_tiles(pos0, f_ref, cs_ref, ss_ref, cs2_ref, ss2_ref):
    ang0 = pos0.astype(F32) * f_ref[...]
    a, b = jnp.cos(ang0), jnp.sin(ang0)
    ctab = a * cs_ref[...] - b * ss_ref[...]
    stab = b * cs2_ref[...] + a * ss2_ref[...]
    return ctab, stab


AB_SEG = (MLA_Q_LORA, MLA_KV_LORA, SB_W, SB_W, MLA_HEADS * MLA_V + SB_W, LANES)
AB_OFF = tuple(int(v) for v in np.cumsum((0,) + AB_SEG))


def _rms(x, eps):
    return x * lax.rsqrt(jnp.mean(x * x, axis=-1, keepdims=True) + eps)


def _swap_rope_halves(v):
    half = MLA_ROPE // 2
    n = v.shape[1]
    lane = lax.broadcasted_iota(jnp.int32, v.shape, 1) % LANES
    return jnp.where(lane < MLA_NOPE + half, pltpu.roll(v, n - half, 1), pltpu.roll(v, half, 1))


def _in0_kernel(per_b, pos_ref, x_ref, shift_ref, scale_ref, g_ref, w_ref, wqa_ref, wk_ref, wvt_ref,
                wvst_ref, qg_ref, kvg_ref, f_ref, cs_ref, ss_ref, cs2_ref, ss2_ref,
                qm_ref, km_ref, vmt_ref, qs_ref, ks_ref, vst_ref, z_ref):
    x = x_ref[...]
    h = _rms(x, NORM_EPS) * (g_ref[...] * (1.0 + scale_ref[0])) + shift_ref[0]
    h = h.astype(BF16)

    def seg(i):
        return _dot(h, w_ref[:, AB_OFF[i]:AB_OFF[i + 1]])

    ctab, stab = _rope_tiles(pos_ref[pl.program_id(0) // per_b], f_ref, cs_ref, ss_ref, cs2_ref, ss2_ref)
    nh = MLA_HEADS

    cq = (_rms(seg(0), NORM_EPS) * qg_ref[...]).astype(BF16)
    ckv = (_rms(seg(1), NORM_EPS) * kvg_ref[...]).astype(BF16)
    q = _dot(cq, wqa_ref[...])
    z_ref[...] = seg(4).astype(BF16)
    q = q * _rep(ctab, nh) + _swap_rope_halves(q) * _rep(stab, nh)
    qm_ref[...] = (q * ((MLA_NOPE + MLA_ROPE) ** -0.5 * LOG2E)).astype(BF16)

    kr = seg(5)
    kn = _dot(ckv, wk_ref[...])
    qs_ref[...] = (seg(2) * (SB_HEAD_DIM ** -0.5 * LOG2E)).astype(BF16)
    krope = kr * ctab + _swap_rope_halves(kr) * stab
    km_ref[...] = (kn + _rep(krope, nh)).astype(BF16)
    vmt = _dot(ckv, wvt_ref[...])
    ks_ref[...] = seg(3).astype(BF16)
    _store_kv_tiles(vmt_ref, vmt.T)
    _store_kv_tiles(vst_ref, _dot(h, wvst_ref[...]).T)


def _prep_ab_weights(w_in, w_uq, w_ukv):
    D = w_in.shape[0]
    c = np.cumsum([MLA_Q_LORA, MLA_KV_LORA, MLA_ROPE, SB_W, SB_W, SB_W]).tolist()
    w_kr = w_in[:, c[1]:c[2]]
    zl = jnp.zeros((D, MLA_NOPE), w_in.dtype)
    zr = jnp.zeros((D, LANES - MLA_NOPE - MLA_ROPE), w_in.dtype)
    kr_a = jnp.concatenate([zl, w_kr, zr], axis=1)
    w0 = jnp.concatenate([
        w_in[:, :c[1]],
        w_in[:, c[2]:c[3]],
        w_in[:, c[3]:c[4]],
        w_in[:, c[5]:],
        kr_a], axis=1).astype(BF16)
    wvst = w_in[:, c[4]:c[5]].astype(BF16)

    hq = MLA_NOPE + MLA_ROPE
    uq = w_uq.reshape(MLA_Q_LORA, MLA_HEADS, hq)
    pad = jnp.zeros((MLA_Q_LORA, MLA_HEADS, LANES - hq), w_uq.dtype)
    wqa = jnp.concatenate([uq, pad], axis=2).reshape(MLA_Q_LORA, MLA_HEADS * LANES).astype(BF16)
    ukv = w_ukv.reshape(MLA_KV_LORA, MLA_HEADS, MLA_NOPE + MLA_V)
    zk = jnp.zeros((MLA_KV_LORA, MLA_HEADS, LANES - MLA_NOPE), w_ukv.dtype)
    wk = jnp.concatenate([ukv[:, :, :MLA_NOPE], zk], axis=2).reshape(MLA_KV_LORA, MLA_HEADS * LANES).astype(BF16)
    wvt = ukv[:, :, MLA_NOPE:].reshape(MLA_KV_LORA, MLA_HEADS * MLA_V).astype(BF16)
    return w0, wqa, wk, wvt, wvst


def _layer0_in(x2, pos_offset, mod3, g, w0, wqa, wk, wvt, wvst, qg, kvg, B):
    T, D = x2.shape
    nt = T // TM
    per_b = nt // B
    rope_tabs, rope_f = _rope_static_tables(T // B)
    full = lambda a: pl.BlockSpec(a.shape, lambda i: (0,) * a.ndim)
    tok = lambda w: pl.BlockSpec((TM, w), lambda i: (i, 0))
    seq = pl.BlockSpec((TM, LANES), lambda i: (i % per_b, 0))
    kvt = lambda w: pl.BlockSpec((TM // TK, w, TK), lambda i: (i, 0, 0))
    tok_out = lambda w: jax.ShapeDtypeStruct((T, w), BF16)
    kvt_out = lambda w: jax.ShapeDtypeStruct((T // TK, w, TK), BF16)
    hv = MLA_HEADS * MLA_V
    return pl.pallas_call(
        functools.partial(_in0_kernel, per_b),
        name="l0_in",
        grid=(nt,),
        in_specs=[
            pl.BlockSpec(memory_space=pltpu.SMEM),
            tok(D),
            pl.BlockSpec((1, 1, D), lambda i: (0 * B + i // per_b, 0, 0)),
            pl.BlockSpec((1, 1, D), lambda i: (1 * B + i // per_b, 0, 0)),
            full(g), full(w0), full(wqa), full(wk), full(wvt), full(wvst), full(qg), full(kvg),
            full(rope_f), seq, seq, seq, seq,
        ],
        out_specs=[tok(MLA_HEADS * LANES), tok(MLA_HEADS * LANES), kvt(hv), tok(SB_W), tok(SB_W), kvt(SB_W),
                   tok(hv + SB_W)],
        out_shape=[tok_out(MLA_HEADS * LANES), tok_out(MLA_HEADS * LANES), kvt_out(hv), tok_out(SB_W),
                   tok_out(SB_W), kvt_out(SB_W), tok_out(hv + SB_W)],
        compiler_params=_params(1),
    )(pos_offset, x2, mod3, mod3, g, w0, wqa, wk, wvt, wvst, qg, kvg, rope_f, *rope_tabs)


def _chunk_mask_t():
    k = lax.broadcasted_iota(jnp.int32, (TK, TQ), 0)
    q = lax.broadcasted_iota(jnp.int32, (TK, TQ), 1)
    shift = CHUNK.bit_length() - 1
    return (k >> shift) <= (q >> shift)


def _tile_start(i, size):
    return i * size if isinstance(i, int) else pl.multiple_of(i * size, size)


FAR, NEAR, DIAG = 0, 1, 2
PROGRAM_ROW = 5


def _tile_kind(qi, kj):
    return DIAG if kj == qi else NEAR if kj == qi - 1 else FAR


def _tile_program(n_tiles, kind_of, kv_per_q=1):
    items = [(qi, kj) for qi in range(n_tiles) for kj in range(kv_per_q * (qi + 1))]
    assert len(items) % 2 == 0 and n_tiles >= 2
    items.append((n_tiles - 1, 0))
    kinds = [kind_of(qi, kj) for qi, kj in items]
    keys, rows = [], []
    for t in range(len(items) - 1):
        key = (kinds[t], kinds[t + 1])
        if key not in keys:
            keys.append(key)
        rows.append([keys.index(key), *items[t], *items[t + 1]])
    return np.asarray(rows, np.int32).reshape(-1), keys


def _run_tile_program(tab_ref, keys, step, finish, prepare=None):
    def variant(key, slot):
        def run(qi, kj, nqi, nkj):
            if key[0] == DIAG and prepare is not None:
                prepare(nqi)
            step(qi, kj, nqi, nkj, slot, key[1])
            if key[0] == DIAG:
                finish(qi)
        return run

    variants = [[variant(key, slot) for key in keys] for slot in range(2)]

    def trip(t, carry):
        for slot in range(2):
            base = (2 * t + slot) * PROGRAM_ROW
            lax.switch(tab_ref[base], variants[slot], *[tab_ref[base + 1 + j] for j in range(PROGRAM_ROW - 1)])
        return carry

    lax.fori_loop(0, tab_ref.shape[0] // (2 * PROGRAM_ROW), trip, 0)


def _softmax_step(s_ref, mc_ref, m_ref, alpha_ref, p_ref, slot, h, rows):
    m_prev = m_ref[h]
    m_new = jnp.maximum(m_prev, mc_ref[slot, h])
    p_ref[rows, :] = jnp.exp2(s_ref[slot, rows, :] - m_new).astype(BF16)
    alpha_ref[h] = jnp.exp2(m_prev - m_new)
    m_ref[h] = m_new


def _values_lhs(vt):
    return jnp.concatenate([vt, jnp.ones((ONES_ROWS, vt.shape[1]), vt.dtype)], axis=0)


def _softmax_init(m_ref, acc_ref):
    m_ref[...] = jnp.full(m_ref.shape, -jnp.inf, F32)
    acc_ref[...] = jnp.zeros(acc_ref.shape, F32)


def _mla_attn_kernel(keys, tab_ref, q_ref, k_ref, vt_ref, o_ref, m_ref, alpha_ref, acc_ref, mc_ref, s_ref, p_ref):
    S = q_ref.shape[0]
    shift = CHUNK.bit_length() - 1
    kc = lax.broadcasted_iota(jnp.int32, (TK, MLA_TQ), 0) >> shift
    qc = lax.broadcasted_iota(jnp.int32, (TK, MLA_TQ), 1) >> shift
    per_q = MLA_TQ // TK
    masks = {DIAG: kc + (per_q - 1) * (TK >> shift) <= qc, NEAR: kc <= qc}
    krows = [pl.ds(h * TK, TK) for h in range(MLA_HEADS)]
    vrows = [pl.ds(h * MLA_V, MLA_V) for h in range(MLA_HEADS)]
    arows = [pl.ds(h * (MLA_V + ONES_ROWS), MLA_V + ONES_ROWS) for h in range(MLA_HEADS)]

    def out_t(h):
        base = h * (MLA_V + ONES_ROWS)
        return acc_ref[pl.ds(base, MLA_V), :] * (1.0 / acc_ref[pl.ds(base + MLA_V, 1), :])

    def scores_head(h, qi, kj, slot, kind):
        q0 = _tile_start(qi, MLA_TQ)
        k0 = _tile_start(kj, TK)
        hl = slice(h * LANES, (h + 1) * LANES)
        s = _dot_nt(k_ref[pl.ds(k0, TK), hl], q_ref[pl.ds(q0, MLA_TQ), hl])
        if kind != FAR:
            s = jnp.where(masks[kind], s, NEG_INF)
        s_ref[slot, krows[h], :] = s
        mc_ref[slot, h] = jnp.max(s, axis=0, keepdims=True)

    def scores(qi, kj, slot, kind):
        for h in range(MLA_HEADS):
            scores_head(h, qi, kj, slot, kind)

    def values_head(h, kj):
        acc_ref[arows[h], :] = (alpha_ref[h] * acc_ref[arows[h], :]
                                + _dot(_values_lhs(vt_ref[kj, vrows[h], :]), p_ref[krows[h], :]))

    def step(qi, kj, nqi, nkj, slot, next_kind):
        for h in range(SKEW):
            scores_head(h, nqi, nkj, 1 - slot, next_kind)
        for h in range(MLA_HEADS):
            if h + SKEW < MLA_HEADS:
                scores_head(h + SKEW, nqi, nkj, 1 - slot, next_kind)
            _softmax_step(s_ref, mc_ref, m_ref, alpha_ref, p_ref, slot, h, krows[h])
            if h > 0:
                values_head(h - 1, kj)
        values_head(MLA_HEADS - 1, kj)

    def finish(qi):
        q0 = pl.multiple_of(qi * MLA_TQ, MLA_TQ)
        for j in range(MLA_HEADS // 2):
            ot = jnp.concatenate([out_t(2 * j), out_t(2 * j + 1)], axis=0)
            o_ref[pl.ds(q0, MLA_TQ), j * LANES:(j + 1) * LANES] = ot.T.astype(BF16)
        _softmax_init(m_ref, acc_ref)

    _softmax_init(m_ref, acc_ref)
    scores(0, 0, 0, NEAR if per_q > 1 else DIAG)
    _run_tile_program(tab_ref, keys, step, finish)


def _sb_attn_kernel(q_ref, k_ref, vt_ref, o_ref, r_ref, rs_ref, acc_ref, z_ref, ls_ref, hi_ref, lo_ref):
    S = q_ref.shape[0]
    k_i = lax.broadcasted_iota(jnp.int32, (TK, TQ), 0)
    q_i = lax.broadcasted_iota(jnp.int32, (TK, TQ), 1)
    causal = k_i < q_i
    tri = jnp.where(lax.broadcasted_iota(jnp.int32, (TK, TK), 1) > lax.broadcasted_iota(jnp.int32, (TK, TK), 0),
                    1.0, 0.0).astype(BF16)
    lane = lax.broadcasted_iota(jnp.int32, (TQ, LANES), 1)
    first = lane < SB_HEAD_DIM
    crows = [pl.ds(c * TK, TK) for c in range(2 * SB_HEADS)]
    vrows = [pl.ds(h * SB_HEAD_DIM, SB_HEAD_DIM) for h in range(SB_HEADS)]
    pairs = [slice((h // 2) * LANES, (h // 2 + 1) * LANES) for h in range(SB_HEADS)]

    def scores_chain(c, h, q0, kj):
        k0 = pl.multiple_of(kj * TK, TK)
        q = q_ref[pl.ds(q0, TQ), pairs[h]]
        q = jnp.where(first, q, jnp.zeros_like(q)) if h % 2 == 0 else jnp.where(first, jnp.zeros_like(q), q)
        z_ref[crows[c], :] = _dot_nt(k_ref[pl.ds(k0, TK), pairs[h]], q)

    def logs_chain(c, masked):
        z = z_ref[crows[c], :]
        sp = jnp.log(1.0 + jnp.exp2(-jnp.abs(z))) * LOG2E
        ls = jnp.minimum(z, 0.0) - sp
        l1m = ls - z
        if masked:
            l1m = jnp.where(causal, l1m, 0.0)
        hi = l1m.astype(BF16)
        ls_ref[crows[c], :] = ls
        hi_ref[crows[c], :] = hi
        lo_ref[crows[c], :] = (l1m - hi.astype(F32)).astype(BF16)
        rs_ref[c] = l1m[0:1, :]

    def suffix_chain(c):
        z_ref[crows[c], :] = _dot(tri, hi_ref[crows[c], :]) + _dot(tri, lo_ref[crows[c], :])

    def weights_chain(c, h, masked):
        r_prev = r_ref[h]
        suffix = z_ref[crows[c], :]
        w = jnp.exp2(ls_ref[crows[c], :] + suffix + r_prev)
        if masked:
            w = jnp.where(causal, w, 0.0)
        hi_ref[crows[c], :] = w.astype(BF16)
        r_ref[h] = r_prev + (rs_ref[c] + suffix[0:1, :])

    def values_chain(c, h, kj):
        acc_ref[vrows[h], :] += _dot(vt_ref[kj, vrows[h], :], hi_ref[crows[c], :])

    def step(q0, tiles):
        chains = [(h, kj, masked) for h in range(SB_HEADS) for kj, masked in tiles]
        n = len(chains)
        lead, lag = 2, 2
        for c in range(lead):
            scores_chain(c, chains[c][0], q0, chains[c][1])
        for t in range(n + lag):
            if t + lead < n:
                scores_chain(t + lead, chains[t + lead][0], q0, chains[t + lead][1])
            if t < n:
                logs_chain(t, chains[t][2])
                suffix_chain(t)
            if t >= lag:
                h, kj, masked = chains[t - lag]
                weights_chain(t - lag, h, masked)
                values_chain(t - lag, h, kj)

    def start():
        r_ref[...] = jnp.zeros(r_ref.shape, F32)
        acc_ref[...] = jnp.zeros(acc_ref.shape, F32)

    def finish(q0):
        for j in range(SB_HEADS // 2):
            ot = acc_ref[pl.ds(j * LANES, LANES), :]
            o_ref[pl.ds(q0, TQ), j * LANES:(j + 1) * LANES] = ot.T.astype(BF16)

    def still_live():
        return (jnp.max(r_ref[...]) > -SB_UNDERFLOW_LOG2).astype(jnp.int32)

    start()
    step(0, [(0, True)])
    finish(0)

    def q_body(qi, carry):
        q0 = pl.multiple_of(qi * TQ, TQ)
        start()
        step(q0, [(qi, True), (qi - 1, False)])

        def cond(c):
            t, live = c
            return jnp.logical_and(t < qi, live > 0)

        def body(c):
            t, _ = c
            step(q0, [(qi - 1 - t, False)])
            return t + 1, still_live()

        lax.while_loop(cond, body, (jnp.int32(1), still_live()))
        finish(q0)
        return carry

    lax.fori_loop(1, S // TQ, q_body, 0)


def _diff_attn_kernel(lam_init, keys, tab_ref, q_ref, k_ref, vt_ref, bias_ref, lamp_ref, g_ref, o_ref,
                      m_ref, alpha_ref, acc_ref, mc_ref, s_ref, p_ref, q2_ref):
    S = q_ref.shape[0]
    mask = _chunk_mask_t()
    mask2 = jnp.concatenate([mask, mask], axis=1)
    lane = lax.broadcasted_iota(jnp.int32, (TQ, LANES), 1)
    first = lane < DIFF_HEAD_DIM
    lp = lamp_ref[...]
    lam = (jnp.exp(jnp.sum(lp[0:1] * lp[1:2], axis=-1, keepdims=True))
           - jnp.exp(jnp.sum(lp[2:3] * lp[3:4], axis=-1, keepdims=True)) + lam_init)
    krows = [pl.ds(h * TK, TK) for h in range(DIFF_HEADS)]
    vrows = [pl.ds(h * LANES, LANES) for h in range(DIFF_HEADS)]
    arows = [pl.ds(h * (LANES + ONES_ROWS), LANES + ONES_ROWS) for h in range(DIFF_HEADS)]
    gain = g_ref[...] * (1.0 - lam_init)

    def scores_head(h, qi, kj, slot, kind):
        q0 = _tile_start(qi, TQ)
        k0 = _tile_start(kj, TK)
        hl = slice(h * LANES, (h + 1) * LANES)
        s = _dot_nt(k_ref[pl.ds(k0, TK), hl], q2_ref[pl.ds(h * 2 * TQ, 2 * TQ), :])
        if kind != FAR:
            s = s + _rep(bias_ref[h, kind - 1], 2)
        if kind == DIAG:
            s = jnp.where(mask2, s, NEG_INF)
        s_ref[slot, krows[h], :] = s
        mc_ref[slot, h] = jnp.max(s, axis=0, keepdims=True)

    def scores(qi, kj, slot, kind):
        for h in range(DIFF_HEADS):
            scores_head(h, qi, kj, slot, kind)

    def values_head(h, kj):
        acc_ref[arows[h], :] = (alpha_ref[h] * acc_ref[arows[h], :]
                                + _dot(_values_lhs(vt_ref[kj, vrows[h], :]), p_ref[krows[h], :]))

    def step(qi, kj, nqi, nkj, slot, next_kind):
        for h in range(SKEW):
            scores_head(h, nqi, nkj, 1 - slot, next_kind)
        for h in range(DIFF_HEADS):
            _softmax_step(s_ref, mc_ref, m_ref, alpha_ref, p_ref, slot, h, krows[h])
            if h + SKEW < DIFF_HEADS:
                scores_head(h + SKEW, nqi, nkj, 1 - slot, next_kind)
            values_head(h, kj)

    def finish(qi):
        q0 = pl.multiple_of(qi * TQ, TQ)
        for h in range(DIFF_HEADS):
            base = h * (LANES + ONES_ROWS)
            inv_l = 1.0 / acc_ref[pl.ds(base + LANES, 1), :]
            ot = (acc_ref[pl.ds(base, LANES), pl.ds(0, TQ)] * inv_l[:, :TQ]
                  - acc_ref[pl.ds(base, LANES), pl.ds(TQ, TQ)] * (lam * inv_l[:, TQ:]))
            ot = ot * lax.rsqrt(jnp.mean(ot * ot, axis=0, keepdims=True) + SUBLN_EPS)
            o_ref[pl.ds(q0, TQ), h * LANES:(h + 1) * LANES] = (ot.T * gain).astype(BF16)
        _softmax_init(m_ref, acc_ref)

    def split_queries(qi):
        q0 = _tile_start(qi, TQ)
        for h in range(DIFF_HEADS):
            q = q_ref[pl.ds(q0, TQ), h * LANES:(h + 1) * LANES]
            zero = jnp.zeros_like(q)
            q2_ref[pl.ds(h * 2 * TQ, TQ), :] = jnp.where(first, q, zero)
            q2_ref[pl.ds(h * 2 * TQ + TQ, TQ), :] = jnp.where(first, zero, q)

    _softmax_init(m_ref, acc_ref)
    split_queries(0)
    scores(0, 0, 0, DIAG)
    _run_tile_program(tab_ref, keys, step, finish, prepare=split_queries)


def _bias_kernel(buckets, idx_ref, tab_ref, o_ref):
    h = pl.program_id(0)
    far = tab_ref[REL_BUCKETS // 2 - 1, h]
    for t in range(2):
        idx = idx_ref[t]
        acc = jnp.zeros(idx.shape, F32)
        for b in buckets[t]:
            acc = jnp.where(idx == b, (tab_ref[b, h] - far) * LOG2E, acc)
        o_ref[0, t] = acc


def _t5_bucket_np(rel):
    nb = REL_BUCKETS // 2
    max_exact = nb // 2
    ret = np.where(rel > 0, nb, 0)
    n = np.abs(rel)
    nf = np.maximum(n, 1).astype(np.float32)
    large = max_exact + (np.log(nf / np.float32(max_exact)) / np.float32(math.log(REL_MAX_DIST / max_exact))
                         * np.float32(nb - max_exact)).astype(np.int32)
    large = np.minimum(large, nb - 1)
    return (ret + np.where(n < max_exact, n, large)).astype(np.int32)


def _bias_tiles(rel_table):
    assert TK >= REL_MAX_DIST
    rel = np.stack([(np.arange(TK)[:, None] + (t - 1) * TK) - np.arange(TQ)[None, :] for t in range(2)])
    idx_np = _t5_bucket_np(rel)
    buckets = [[int(b) for b in np.unique(idx_np[t])] for t in range(2)]
    idx = jnp.asarray(idx_np)
    return pl.pallas_call(
        functools.partial(_bias_kernel, buckets),
        name="rel_bias",
        grid=(DIFF_HEADS,),
        in_specs=[pl.BlockSpec((2, TK, TQ), lambda h: (0, 0, 0)),
                  pl.BlockSpec(memory_space=pltpu.SMEM)],
        out_specs=pl.BlockSpec((1, 2, TK, TQ), lambda h: (h, 0, 0, 0)),
        out_shape=jax.ShapeDtypeStruct((DIFF_HEADS, 2, TK, TQ), F32),
        compiler_params=_params(1),
    )(idx, rel_table.astype(F32))


def _attention_call(kernel_fn, name, q, k, vt, extra, out_width, scratch, B, S, program=None):
    T = q.shape[0]
    smem = [] if program is None else [jnp.asarray(program)]
    seq = lambda a: pl.BlockSpec((S, a.shape[1]), lambda b: (b, 0))
    const = lambda a: pl.BlockSpec(a.shape, lambda b: (0,) * a.ndim, pipeline_mode=pl.Buffered(1))
    return pl.pallas_call(
        kernel_fn,
        name=name,
        grid=(B,),
        in_specs=[pl.BlockSpec(memory_space=pltpu.SMEM) for _ in smem]
        + [seq(q), seq(k), pl.BlockSpec((S // TK,) + vt.shape[1:], lambda b: (b, 0, 0))]
        + [const(a) for a in extra],
        out_specs=pl.BlockSpec((S, out_width), lambda b: (b, 0)),
        out_shape=jax.ShapeDtypeStruct((T, out_width), BF16),
        scratch_shapes=scratch,
        compiler_params=_params(1),
    )(*smem, q, k, vt, *extra)


def _softmax_scratch(heads, n, dv):
    return ([pltpu.VMEM((heads, 1, n), F32)] * 2 + [pltpu.VMEM((heads * (dv + ONES_ROWS), n), F32)]
            + [pltpu.VMEM((2, heads, 1, n), F32), pltpu.VMEM((2, heads * TK, n), F32)]
            + [pltpu.VMEM((heads * TK, n), BF16)])


def _mla_attention(qm, km, vmt, B, S):
    per_q = MLA_TQ // TK
    kind = lambda qi, kj: DIAG if kj == per_q * (qi + 1) - 1 else NEAR if kj == per_q * qi else FAR
    table, keys = _tile_program(S // MLA_TQ, kind, kv_per_q=per_q)
    return _attention_call(functools.partial(_mla_attn_kernel, keys), "mla_attn", qm, km, vmt, (),
                           MLA_HEADS * MLA_V, _softmax_scratch(MLA_HEADS, MLA_TQ, MLA_V), B, S, program=table)


def _sb_attention(qs, ks, vst, B, S):
    n = 2 * SB_HEADS * TK
    scratch = ([pltpu.VMEM((SB_HEADS, 1, TQ), F32), pltpu.VMEM((2 * SB_HEADS, 1, TQ), F32), pltpu.VMEM((SB_W, TQ), F32)]
               + [pltpu.VMEM((n, TQ), F32)] * 2 + [pltpu.VMEM((n, TQ), BF16)] * 2)
    return _attention_call(_sb_attn_kernel, "sb_attn", qs, ks, vst, (), SB_W, scratch, B, S)


def _diff_attention(qd, kd, vdt, bias, lamp, g, lam_init, B, S):
    table, keys = _tile_program(S // TQ, _tile_kind)
    return _attention_call(functools.partial(_diff_attn_kernel, lam_init, keys), "diff_attn", qd, kd, vdt,
                           (bias, lamp, g), DIFF_W,
                           _softmax_scratch(DIFF_HEADS, 2 * TQ, LANES) + [pltpu.VMEM((DIFF_HEADS * 2 * TQ, LANES), BF16)],
                           B, S,
                           program=table)


def _out0_in1_kernel(om_ref, os_ref, z_ref, x_ref, gate_ref, wo_ref, shift_ref, scale_ref, g_ref, w1_ref,
                     x1_ref, qd_ref, kd_ref, vdt_ref, zd_ref):
    z = z_ref[...].astype(F32)
    y = jnp.concatenate([om_ref[...].astype(F32), os_ref[...].astype(F32)], axis=1) * _silu(z)
    out = _dot(y.astype(BF16), wo_ref[...])
    x1 = x_ref[...] + gate_ref[0] * out
    x1_ref[...] = x1
    h = (_rms(x1, NORM_EPS) * (g_ref[...] * (1.0 + scale_ref[0])) + shift_ref[0]).astype(BF16)
    qd_ref[...] = (_dot(h, w1_ref[:, 0:DIFF_W]) * (DIFF_HEAD_DIM ** -0.5 * LOG2E)).astype(BF16)
    kd_ref[...] = _dot(h, w1_ref[:, DIFF_W:2 * DIFF_W]).astype(BF16)
    _store_kv_tiles(vdt_ref, _dot(h, w1_ref[:, 2 * DIFF_W:3 * DIFF_W]).T)
    zd_ref[...] = _dot(h, w1_ref[:, 3 * DIFF_W:4 * DIFF_W]).astype(BF16)


def _layer0_out_layer1_in(om, osb, z, x2, mod3, wo, g1, w1, B):
    T, D = x2.shape
    nt = T // TM
    per_b = nt // B
    full = lambda a: pl.BlockSpec(a.shape, lambda i: (0,) * a.ndim)
    tok = lambda w: pl.BlockSpec((TM, w), lambda i: (i, 0))
    mod = lambda row: pl.BlockSpec((1, 1, D), lambda i: (row * B + i // per_b, 0, 0))
    tok_out = jax.ShapeDtypeStruct((T, DIFF_W), BF16)
    return pl.pallas_call(
        _out0_in1_kernel,
        name="l0_out_l1_in",
        grid=(nt,),
        in_specs=[tok(om.shape[1]), tok(osb.shape[1]), tok(D), tok(D), mod(2), full(wo),
                  mod(3), mod(4), full(g1), full(w1)],
        out_specs=[tok(D), tok(DIFF_W), tok(DIFF_W),
                   pl.BlockSpec((TM // TK, DIFF_W, TK), lambda i: (i, 0, 0)), tok(DIFF_W)],
        out_shape=[jax.ShapeDtypeStruct((T, D), F32), tok_out, tok_out,
                   jax.ShapeDtypeStruct((T // TK, DIFF_W, TK), BF16), tok_out],
        compiler_params=_params(1),
    )(om, osb, z, x2, mod3, wo, mod3, mod3, g1, w1)


def _out1_kernel(o_ref, z_ref, x_ref, gate_ref, wo_ref, g_ref, y_ref):
    y = o_ref[...].astype(F32) * _silu(z_ref[...].astype(F32))
    out = _dot(y.astype(BF16), wo_ref[...])
    x2 = x_ref[...] + gate_ref[0] * out
    y_ref[...] = _rms(x2, NORM_EPS) * g_ref[...]


def _layer1_out(od, zd, x1, mod3, wo, gf, B):
    T, D = x1.shape
    nt = T // TM_OUT
    per_b = nt // B
    full = lambda a: pl.BlockSpec(a.shape, lambda i: (0,) * a.ndim)
    tok = lambda w: pl.BlockSpec((TM_OUT, w), lambda i: (i, 0))
    return pl.pallas_call(
        _out1_kernel,
        name="l1_out",
        grid=(nt,),
        in_specs=[tok(D), tok(D), tok(D), pl.BlockSpec((1, 1, D), lambda i: (5 * B + i // per_b, 0, 0)),
                  full(wo), full(gf)],
        out_specs=tok(D),
        out_shape=jax.ShapeDtypeStruct((T, D), F32),
        compiler_params=_params(1),
    )(od, zd, x1, mod3, wo, gf)


def kernel(x, c, pos_offset, rel_bias_table, ada_w, ada_b, norm_g, final_g, ab_w_in, ab_q_norm_g, ab_kv_norm_g,
           ab_w_uq, ab_w_ukv, ab_w_out, dif_w_in, dif_lam_q1, dif_lam_k1, dif_lam_q2, dif_lam_k2, dif_subln_g,
           dif_w_out):
    B, S, D = x.shape
    assert D == D_MODEL and S % TQ == 0 and TQ == TK and S % TM == 0 and S % TM_OUT == 0 and TM % TK == 0
    x2 = x.reshape(B * S, D)

    mod3 = _modulation(c, ada_w, ada_b)

    w0, wqa, wk, wvt, wvst = _prep_ab_weights(ab_w_in[0], ab_w_uq[0], ab_w_ukv[0])
    qm, km, vmt, qs, ks, vst, z = _layer0_in(
        x2, pos_offset, mod3, norm_g[0:1], w0, wqa, wk, wvt, wvst, ab_q_norm_g[0:1], ab_kv_norm_g[0:1], B)
    om = _mla_attention(qm, km, vmt, B, S)
    osb = _sb_attention(qs, ks, vst, B, S)

    x1, qd, kd, vdt, zd = _layer0_out_layer1_in(
        om, osb, z, x2, mod3, ab_w_out[0].astype(BF16), norm_g[1:2], dif_w_in[0].astype(BF16), B)
    lam_init = 0.8 - 0.6 * math.exp(-0.3 * 1)
    lamp = jnp.stack([dif_lam_q1[0], dif_lam_k1[0], dif_lam_q2[0], dif_lam_k2[0]]).astype(F32)
    bias = _bias_tiles(rel_bias_table)
    od = _diff_attention(qd, kd, vdt, bias, lamp, dif_subln_g[0:1], lam_init, B, S)
    y = _layer1_out(od, zd, x1, mod3, dif_w_out[0].astype(BF16), final_g[None, :], B)
    return y.reshape(B, S, D)
```

```python
import functools
import math

import numpy as np
import jax
import jax.numpy as jnp
from jax import lax
from jax.experimental import pallas as pl
from jax.experimental.pallas import tpu as pltpu

D_MODEL = 1024
DEPTH = 2
CHUNK = 64

MLA_HEADS = 8
MLA_Q_LORA = 384
MLA_KV_LORA = 256
MLA_NOPE = 64
MLA_ROPE = 32
MLA_V = 64
SB_HEADS = 8
SB_HEAD_DIM = 64
SB_W = SB_HEADS * SB_HEAD_DIM
DIFF_HEADS = 8
DIFF_HEAD_DIM = 64
DIFF_W = DIFF_HEADS * 2 * DIFF_HEAD_DIM

REL_BUCKETS = 32
REL_MAX_DIST = 128
ROPE_THETA = 10000.0
NORM_EPS = 1e-6
SUBLN_EPS = 1e-5
NEG_INF = -1e30
SB_UNDERFLOW_LOG2 = 152.0
LOG2E = math.log2(math.e)

LANES = 128
V7X_VMEM_BYTES = 64 * 1024 * 1024
VMEM_LIMIT = V7X_VMEM_BYTES * 7 // 8

TM = 512
TM_OUT = 1024
TQ = 256
TK = 256
MLA_TQ = TK
SKEW = 0
ONES_ROWS = 16

F32 = jnp.float32
BF16 = jnp.bfloat16


def _silu(z):
    return z * (1.0 / (1.0 + jnp.exp(-z)))


def _dot(a, b):
    return jnp.dot(a, b, preferred_element_type=F32)


def _dot_nt(a, b):
    return lax.dot_general(a, b, (((1,), (1,)), ((), ())), preferred_element_type=F32)


def _rep(x, n):
    return x if n == 1 else jnp.concatenate([x] * n, axis=1)


def _params(n_axes=1):
    return pltpu.CompilerParams(dimension_semantics=("arbitrary",) * n_axes, vmem_limit_bytes=VMEM_LIMIT)


def _store_kv_tiles(ref, xt):
    for j in range(xt.shape[1] // TK):
        ref[j] = xt[:, j * TK:(j + 1) * TK].astype(ref.dtype)


def _mod_kernel(c_ref, w_ref, b_ref, o_ref):
    D = c_ref.shape[1]
    ca = _silu(c_ref[...]).astype(BF16)
    for j in range(3):
        o_ref[j] = _dot(ca, w_ref[0, :, j * D:(j + 1) * D].astype(BF16)) + b_ref[0, :, j * D:(j + 1) * D]


def _modulation(c, ada_w, ada_b):
    B, D = c.shape
    out = pl.pallas_call(
        _mod_kernel,
        name="ada_mod",
        grid=(DEPTH,),
        in_specs=[
            pl.BlockSpec((B, D), lambda i: (0, 0)),
            pl.BlockSpec((1, D, 3 * D), lambda i: (i, 0, 0)),
            pl.BlockSpec((1, 1, 3 * D), lambda i: (i, 0, 0)),
        ],
        out_specs=pl.BlockSpec((3, B, D), lambda i: (i, 0, 0)),
        out_shape=jax.ShapeDtypeStruct((DEPTH * 3, B, D), F32),
        compiler_params=_params(1),
    )(c, ada_w, ada_b.reshape(DEPTH, 1, 3 * D))
    return out.reshape(DEPTH * 3 * B, 1, D)


def _rope_static_tables(S):
    half = MLA_ROPE // 2
    inv_freq = np.float32(ROPE_THETA) ** (-(np.arange(half, dtype=np.float32) / np.float32(half)))
    inv_freq = inv_freq.astype(np.float32)
    ang = np.arange(S, dtype=np.float64)[:, None] * inv_freq.astype(np.float64)[None, :]
    cos, sin = np.cos(ang), np.sin(ang)
    one = np.ones((S, MLA_NOPE))
    zn = np.zeros((S, MLA_NOPE))
    zp = np.zeros((S, LANES - MLA_NOPE - MLA_ROPE))
    tabs = [np.concatenate(parts, axis=1).astype(np.float32) for parts in (
        (one, cos, cos, zp), (zn, sin, sin, zp), (zn, -cos, cos, zp), (zn, -sin, sin, zp))]
    f = np.concatenate([np.zeros(MLA_NOPE, np.float32), inv_freq, inv_freq,
                        np.zeros(LANES - MLA_NOPE - MLA_ROPE, np.float32)])[None, :]
    return [jnp.asarray(t) for t in tabs], jnp.asarray(f)


def _rope_tiles(pos0, f_ref, cs_ref, ss_ref, cs2_ref, ss2_ref):
    ang0 = pos0.astype(F32) * f_ref[...]
    a, b = jnp.cos(ang0), jnp.sin(ang0)
    ctab = a * cs_ref[...] - b * ss_ref[...]
    stab = b * cs2_ref[...] + a * ss2_ref[...]
    return ctab, stab


AB_SEG = (MLA_Q_LORA, MLA_KV_LORA, SB_W, SB_W, MLA_HEADS * MLA_V + SB_W, LANES)
AB_OFF = tuple(int(v) for v in np.cumsum((0,) + AB_SEG))


def _rms(x, eps):
    return x * lax.rsqrt(jnp.mean(x * x, axis=-1, keepdims=True) + eps)


def _swap_rope_halves(v):
    half = MLA_ROPE // 2
    n = v.shape[1]
    lane = lax.broadcasted_iota(jnp.int32, v.shape, 1) % LANES
    return jnp.where(lane < MLA_NOPE + half, pltpu.roll(v, n - half, 1), pltpu.roll(v, half, 1))


def _in0_kernel(per_b, pos_ref, x_ref, shift_ref, scale_ref, g_ref, w_ref, wqa_ref, wk_ref, wvt_ref,
                wvst_ref, qg_ref, kvg_ref, f_ref, cs_ref, ss_ref, cs2_ref, ss2_ref,
                qm_ref, km_ref, vmt_ref, qs_ref, ks_ref, vst_ref, z_ref):
    x = x_ref[...]
    h = _rms(x, NORM_EPS) * (g_ref[...] * (1.0 + scale_ref[0])) + shift_ref[0]
    h = h.astype(BF16)

    def seg(i):
        return _dot(h, w_ref[:, AB_OFF[i]:AB_OFF[i + 1]])

    ctab, stab = _rope_tiles(pos_ref[pl.program_id(0) // per_b], f_ref, cs_ref, ss_ref, cs2_ref, ss2_ref)
    nh = MLA_HEADS

    cq = (_rms(seg(0), NORM_EPS) * qg_ref[...]).astype(BF16)
    ckv = (_rms(seg(1), NORM_EPS) * kvg_ref[...]).astype(BF16)
    q = _dot(cq, wqa_ref[...])
    z_ref[...] = seg(4).astype(BF16)
    q = q * _rep(ctab, nh) + _swap_rope_halves(q) * _rep(stab, nh)
    qm_ref[...] = (q * ((MLA_NOPE + MLA_ROPE) ** -0.5 * LOG2E)).astype(BF16)

    kr = seg(5)
    kn = _dot(ckv, wk_ref[...])
    qs_ref[...] = (seg(2) * (SB_HEAD_DIM ** -0.5 * LOG2E)).astype(BF16)
    krope = kr * ctab + _swap_rope_halves(kr) * stab
    km_ref[...] = (kn + _rep(krope, nh)).astype(BF16)
    vmt = _dot(ckv, wvt_ref[...])
    ks_ref[...] = seg(3).astype(BF16)
    _store_kv_tiles(vmt_ref, vmt.T)
    _store_kv_tiles(vst_ref, _dot(h, wvst_ref[...]).T)


def _prep_ab_weights(w_in, w_uq, w_ukv):
    D = w_in.shape[0]
    c = np.cumsum([MLA_Q_LORA, MLA_KV_LORA, MLA_ROPE, SB_W, SB_W, SB_W]).tolist()
    w_kr = w_in[:, c[1]:c[2]]
    zl = jnp.zeros((D, MLA_NOPE), w_in.dtype)
    zr = jnp.zeros((D, LANES - MLA_NOPE - MLA_ROPE), w_in.dtype)
    kr_a = jnp.concatenate([zl, w_kr, zr], axis=1)
    w0 = jnp.concatenate([
        w_in[:, :c[1]],
        w_in[:, c[2]:c[3]],
        w_in[:, c[3]:c[4]],
        w_in[:, c[5]:],
        kr_a], axis=1).astype(BF16)
    wvst = w_in[:, c[4]:c[5]].astype(BF16)

    hq = MLA_NOPE + MLA_ROPE
    uq = w_uq.reshape(MLA_Q_LORA, MLA_HEADS, hq)
    pad = jnp.zeros((MLA_Q_LORA, MLA_HEADS, LANES - hq), w_uq.dtype)
    wqa = jnp.concatenate([uq, pad], axis=2).reshape(MLA_Q_LORA, MLA_HEADS * LANES).astype(BF16)
    ukv = w_ukv.reshape(MLA_KV_LORA, MLA_HEADS, MLA_NOPE + MLA_V)
    zk = jnp.zeros((MLA_KV_LORA, MLA_HEADS, LANES - MLA_NOPE), w_ukv.dtype)
    wk = jnp.concatenate([ukv[:, :, :MLA_NOPE], zk], axis=2).reshape(MLA_KV_LORA, MLA_HEADS * LANES).astype(BF16)
    wvt = ukv[:, :, MLA_NOPE:].reshape(MLA_KV_LORA, MLA_HEADS * MLA_V).astype(BF16)
    return w0, wqa, wk, wvt, wvst


def _layer0_in(x2, pos_offset, mod3, g, w0, wqa, wk, wvt, wvst, qg, kvg, B):
    T, D = x2.shape
    nt = T // TM
    per_b = nt // B
    rope_tabs, rope_f = _rope_static_tables(T // B)
    full = lambda a: pl.BlockSpec(a.shape, lambda i: (0,) * a.ndim)
    tok = lambda w: pl.BlockSpec((TM, w), lambda i: (i, 0))
    seq = pl.BlockSpec((TM, LANES), lambda i: (i % per_b, 0))
    kvt = lambda w: pl.BlockSpec((TM // TK, w, TK), lambda i: (i, 0, 0))
    tok_out = lambda w: jax.ShapeDtypeStruct((T, w), BF16)
    kvt_out = lambda w: jax.ShapeDtypeStruct((T // TK, w, TK), BF16)
    hv = MLA_HEADS * MLA_V
    return pl.pallas_call(
        functools.partial(_in0_kernel, per_b),
        name="l0_in",
        grid=(nt,),
        in_specs=[
            pl.BlockSpec(memory_space=pltpu.SMEM),
            tok(D),
            pl.BlockSpec((1, 1, D), lambda i: (0 * B + i // per_b, 0, 0)),
            pl.BlockSpec((1, 1, D), lambda i: (1 * B + i // per_b, 0, 0)),
            full(g), full(w0), full(wqa), full(wk), full(wvt), full(wvst), full(qg), full(kvg),
            full(rope_f), seq, seq, seq, seq,
        ],
        out_specs=[tok(MLA_HEADS * LANES), tok(MLA_HEADS * LANES), kvt(hv), tok(SB_W), tok(SB_W), kvt(SB_W),
                   tok(hv + SB_W)],
        out_shape=[tok_out(MLA_HEADS * LANES), tok_out(MLA_HEADS * LANES), kvt_out(hv), tok_out(SB_W),
                   tok_out(SB_W), kvt_out(SB_W), tok_out(hv + SB_W)],
        compiler_params=_params(1),
    )(pos_offset, x2, mod3, mod3, g, w0, wqa, wk, wvt, wvst, qg, kvg, rope_f, *rope_tabs)


def _chunk_mask_t():
    k = lax.broadcasted_iota(jnp.int32, (TK, TQ), 0)
    q = lax.broadcasted_iota(jnp.int32, (TK, TQ), 1)
    shift = CHUNK.bit_length() - 1
    return (k >> shift) <= (q >> shift)


def _tile_start(i, size):
    return i * size if isinstance(i, int) else pl.multiple_of(i * size, size)


FAR, NEAR, DIAG = 0, 1, 2
PROGRAM_ROW = 5


def _tile_kind(qi, kj):
    return DIAG if kj == qi else NEAR if kj == qi - 1 else FAR


def _tile_program(n_tiles, kind_of, kv_per_q=1):
    items = [(qi, kj) for qi in range(n_tiles) for kj in range(kv_per_q * (qi + 1))]
    assert len(items) % 2 == 0 and n_tiles >= 2
    items.append((n_tiles - 1, 0))
    kinds = [kind_of(qi, kj) for qi, kj in items]
    keys, rows = [], []
    for t in range(len(items) - 1):
        key = (kinds[t], kinds[t + 1])
        if key not in keys:
            keys.append(key)
        rows.append([keys.index(key), *items[t], *items[t + 1]])
    return np.asarray(rows, np.int32).reshape(-1), keys


def _run_tile_program(tab_ref, keys, step, finish, prepare=None):
    def variant(key, slot):
        def run(qi, kj, nqi, nkj):
            if key[0] == DIAG and prepare is not None:
                prepare(nqi)
            step(qi, kj, nqi, nkj, slot, key[1])
            if key[0] == DIAG:
                finish(qi)
        return run

    variants = [[variant(key, slot) for key in keys] for slot in range(2)]

    def trip(t, carry):
        for slot in range(2):
            base = (2 * t + slot) * PROGRAM_ROW
            lax.switch(tab_ref[base], variants[slot], *[tab_ref[base + 1 + j] for j in range(PROGRAM_ROW - 1)])
        return carry

    lax.fori_loop(0, tab_ref.shape[0] // (2 * PROGRAM_ROW), trip, 0)


def _softmax_step(s_ref, mc_ref, m_ref, alpha_ref, p_ref, slot, h, rows):
    m_prev = m_ref[h]
    m_new = jnp.maximum(m_prev, mc_ref[slot, h])
    p_ref[rows, :] = jnp.exp2(s_ref[slot, rows, :] - m_new).astype(BF16)
    alpha_ref[h] = jnp.exp2(m_prev - m_new)
    m_ref[h] = m_new


def _values_lhs(vt):
    return jnp.concatenate([vt, jnp.ones((ONES_ROWS, vt.shape[1]), vt.dtype)], axis=0)


def _softmax_init(m_ref, acc_ref):
    m_ref[...] = jnp.full(m_ref.shape, -jnp.inf, F32)
    acc_ref[...] = jnp.zeros(acc_ref.shape, F32)


def _mla_attn_kernel(keys, tab_ref, q_ref, k_ref, vt_ref, o_ref, m_ref, alpha_ref, acc_ref, mc_ref, s_ref, p_ref):
    S = q_ref.shape[0]
    shift = CHUNK.bit_length() - 1
    kc = lax.broadcasted_iota(jnp.int32, (TK, MLA_TQ), 0) >> shift
    qc = lax.broadcasted_iota(jnp.int32, (TK, MLA_TQ), 1) >> shift
    per_q = MLA_TQ // TK
    masks = {DIAG: kc + (per_q - 1) * (TK >> shift) <= qc, NEAR: kc <= qc}
    krows = [pl.ds(h * TK, TK) for h in range(MLA_HEADS)]
    vrows = [pl.ds(h * MLA_V, MLA_V) for h in range(MLA_HEADS)]
    arows = [pl.ds(h * (MLA_V + ONES_ROWS), MLA_V + ONES_ROWS) for h in range(MLA_HEADS)]

    def out_t(h):
        base = h * (MLA_V + ONES_ROWS)
        return acc_ref[pl.ds(base, MLA_V), :] * (1.0 / acc_ref[pl.ds(base + MLA_V, 1), :])

    def scores_head(h, qi, kj, slot, kind):
        q0 = _tile_start(qi, MLA_TQ)
        k0 = _tile_start(kj, TK)
        hl = slice(h * LANES, (h + 1) * LANES)
        s = _dot_nt(k_ref[pl.ds(k0, TK), hl], q_ref[pl.ds(q0, MLA_TQ), hl])
        if kind != FAR:
            s = jnp.where(masks[kind], s, NEG_INF)
        s_ref[slot, krows[h], :] = s
        mc_ref[slot, h] = jnp.max(s, axis=0, keepdims=True)

    def scores(qi, kj, slot, kind):
        for h in range(MLA_HEADS):
            scores_head(h, qi, kj, slot, kind)

    def values_head(h, kj):
        acc_ref[arows[h], :] = (alpha_ref[h] * acc_ref[arows[h], :]
                                + _dot(_values_lhs(vt_ref[kj, vrows[h], :]), p_ref[krows[h], :]))

    def step(qi, kj, nqi, nkj, slot, next_kind):
        for h in range(SKEW):
            scores_head(h, nqi, nkj, 1 - slot, next_kind)
        for h in range(MLA_HEADS):
            if h + SKEW < MLA_HEADS:
                scores_head(h + SKEW, nqi, nkj, 1 - slot, next_kind)
            _softmax_step(s_ref, mc_ref, m_ref, alpha_ref, p_ref, slot, h, krows[h])
            if h > 0:
                values_head(h - 1, kj)
        values_head(MLA_HEADS - 1, kj)

    def finish(qi):
        q0 = pl.multiple_of(qi * MLA_TQ, MLA_TQ)
        for j in range(MLA_HEADS // 2):
            ot = jnp.concatenate([out_t(2 * j), out_t(2 * j + 1)], axis=0)
            o_ref[pl.ds(q0, MLA_TQ), j * LANES:(j + 1) * LANES] = ot.T.astype(BF16)
        _softmax_init(m_ref, acc_ref)

    _softmax_init(m_ref, acc_ref)
    scores(0, 0, 0, NEAR if per_q > 1 else DIAG)
    _run_tile_program(tab_ref, keys, step, finish)


def _sb_attn_kernel(q_ref, k_ref, vt_ref, o_ref, r_ref, rs_ref, acc_ref, z_ref, ls_ref, hi_ref, lo_ref):
    S = q_ref.shape[0]
    k_i = lax.broadcasted_iota(jnp.int32, (TK, TQ), 0)
    q_i = lax.broadcasted_iota(jnp.int32, (TK, TQ), 1)
    causal = k_i < q_i
    tri = jnp.where(lax.broadcasted_iota(jnp.int32, (TK, TK), 1) > lax.broadcasted_iota(jnp.int32, (TK, TK), 0),
                    1.0, 0.0).astype(BF16)
    lane = lax.broadcasted_iota(jnp.int32, (TQ, LANES), 1)
    first = lane < SB_HEAD_DIM
    crows = [pl.ds(c * TK, TK) for c in range(2 * SB_HEADS)]
    vrows = [pl.ds(h * SB_HEAD_DIM, SB_HEAD_DIM) for h in range(SB_HEADS)]
    pairs = [slice((h // 2) * LANES, (h // 2 + 1) * LANES) for h in range(SB_HEADS)]

    def scores_chain(c, h, q0, kj):
        k0 = pl.multiple_of(kj * TK, TK)
        q = q_ref[pl.ds(q0, TQ), pairs[h]]
        q = jnp.where(first, q, jnp.zeros_like(q)) if h % 2 == 0 else jnp.where(first, jnp.zeros_like(q), q)
        z_ref[crows[c], :] = _dot_nt(k_ref[pl.ds(k0, TK), pairs[h]], q)

    def logs_chain(c, masked):
        z = z_ref[crows[c], :]
        sp = jnp.log(1.0 + jnp.exp2(-jnp.abs(z))) * LOG2E
        ls = jnp.minimum(z, 0.0) - sp
        l1m = ls - z
        if masked:
            l1m = jnp.where(causal, l1m, 0.0)
        hi = l1m.astype(BF16)
        ls_ref[crows[c], :] = ls
        hi_ref[crows[c], :] = hi
        lo_ref[crows[c], :] = (l1m - hi.astype(F32)).astype(BF16)
        rs_ref[c] = l1m[0:1, :]

    def suffix_chain(c):
        z_ref[crows[c], :] = _dot(tri, hi_ref[crows[c], :]) + _dot(tri, lo_ref[crows[c], :])

    def weights_chain(c, h, masked):
        r_prev = r_ref[h]
        suffix = z_ref[crows[c], :]
        w = jnp.exp2(ls_ref[crows[c], :] + suffix + r_prev)
        if masked:
            w = jnp.where(causal, w, 0.0)
        hi_ref[crows[c], :] = w.astype(BF16)
        r_ref[h] = r_prev + (rs_ref[c] + suffix[0:1, :])

    def values_chain(c, h, kj):
        acc_ref[vrows[h], :] += _dot(vt_ref[kj, vrows[h], :], hi_ref[crows[c], :])

    def step(q0, tiles):
        chains = [(h, kj, masked) for h in range(SB_HEADS) for kj, masked in tiles]
        n = len(chains)
        lead, lag = 2, 2
        for c in range(lead):
            scores_chain(c, chains[c][0], q0, chains[c][1])
        for t in range(n + lag):
            if t + lead < n:
                scores_chain(t + lead, chains[t + lead][0], q0, chains[t + lead][1])
            if t < n:
                logs_chain(t, chains[t][2])
                suffix_chain(t)
            if t >= lag:
                h, kj, masked = chains[t - lag]
                weights_chain(t - lag, h, masked)
                values_chain(t - lag, h, kj)

    def start():
        r_ref[...] = jnp.zeros(r_ref.shape, F32)
        acc_ref[...] = jnp.zeros(acc_ref.shape, F32)

    def finish(q0):
        for j in range(SB_HEADS // 2):
            ot = acc_ref[pl.ds(j * LANES, LANES), :]
            o_ref[pl.ds(q0, TQ), j * LANES:(j + 1) * LANES] = ot.T.astype(BF16)

    def still_live():
        return (jnp.max(r_ref[...]) > -SB_UNDERFLOW_LOG2).astype(jnp.int32)

    start()
    step(0, [(0, True)])
    finish(0)

    def q_body(qi, carry):
        q0 = pl.multiple_of(qi * TQ, TQ)
        start()
        step(q0, [(qi, True), (qi - 1, False)])

        def cond(c):
            t, live = c
            return jnp.logical_and(t < qi, live > 0)

        def body(c):
            t, _ = c
            step(q0, [(qi - 1 - t, False)])
            return t + 1, still_live()

        lax.while_loop(cond, body, (jnp.int32(1), still_live()))
        finish(q0)
        return carry

    lax.fori_loop(1, S // TQ, q_body, 0)


def _diff_attn_kernel(lam_init, keys, tab_ref, q_ref, k_ref, vt_ref, bias_ref, lamp_ref, g_ref, o_ref,
                      m_ref, alpha_ref, acc_ref, mc_ref, s_ref, p_ref, q2_ref):
    S = q_ref.shape[0]
    mask = _chunk_mask_t()
    mask2 = jnp.concatenate([mask, mask], axis=1)
    lane = lax.broadcasted_iota(jnp.int32, (TQ, LANES), 1)
    first = lane < DIFF_HEAD_DIM
    lp = lamp_ref[...]
    lam = (jnp.exp(jnp.sum(lp[0:1] * lp[1:2], axis=-1, keepdims=True))
           - jnp.exp(jnp.sum(lp[2:3] * lp[3:4], axis=-1, keepdims=True)) + lam_init)
    krows = [pl.ds(h * TK, TK) for h in range(DIFF_HEADS)]
    vrows = [pl.ds(h * LANES, LANES) for h in range(DIFF_HEADS)]
    arows = [pl.ds(h * (LANES + ONES_ROWS), LANES + ONES_ROWS) for h in range(DIFF_HEADS)]
    gain = g_ref[...] * (1.0 - lam_init)

    def scores_head(h, qi, kj, slot, kind):
        q0 = _tile_start(qi, TQ)
        k0 = _tile_start(kj, TK)
        hl = slice(h * LANES, (h + 1) * LANES)
        s = _dot_nt(k_ref[pl.ds(k0, TK), hl], q2_ref[pl.ds(h * 2 * TQ, 2 * TQ), :])
        if kind != FAR:
            s = s + _rep(bias_ref[h, kind - 1], 2)
        if kind == DIAG:
            s = jnp.where(mask2, s, NEG_INF)
        s_ref[slot, krows[h], :] = s
        mc_ref[slot, h] = jnp.max(s, axis=0, keepdims=True)

    def scores(qi, kj, slot, kind):
        for h in range(DIFF_HEADS):
            scores_head(h, qi, kj, slot, kind)

    def values_head(h, kj):
        acc_ref[arows[h], :] = (alpha_ref[h] * acc_ref[arows[h], :]
                                + _dot(_values_lhs(vt_ref[kj, vrows[h], :]), p_ref[krows[h], :]))

    def step(qi, kj, nqi, nkj, slot, next_kind):
        for h in range(SKEW):
            scores_head(h, nqi, nkj, 1 - slot, next_kind)
        for h in range(DIFF_HEADS):
            _softmax_step(s_ref, mc_ref, m_ref, alpha_ref, p_ref, slot, h, krows[h])
            if h + SKEW < DIFF_HEADS:
                scores_head(h + SKEW, nqi, nkj, 1 - slot, next_kind)
            values_head(h, kj)

    def finish(qi):
        q0 = pl.multiple_of(qi * TQ, TQ)
        for h in range(DIFF_HEADS):
            base = h * (LANES + ONES_ROWS)
            inv_l = 1.0 / acc_ref[pl.ds(base + LANES, 1), :]
            ot = (acc_ref[pl.ds(base, LANES), pl.ds(0, TQ)] * inv_l[:, :TQ]
                  - acc_ref[pl.ds(base, LANES), pl.ds(TQ, TQ)] * (lam * inv_l[:, TQ:]))
            ot = ot * lax.rsqrt(jnp.mean(ot * ot, axis=0, keepdims=True) + SUBLN_EPS)
            o_ref[pl.ds(q0, TQ), h * LANES:(h + 1) * LANES] = (ot.T * gain).astype(BF16)
        _softmax_init(m_ref, acc_ref)

    def split_queries(qi):
        q0 = _tile_start(qi, TQ)
        for h in range(DIFF_HEADS):
            q = q_ref[pl.ds(q0, TQ), h * LANES:(h + 1) * LANES]
            zero = jnp.zeros_like(q)
            q2_ref[pl.ds(h * 2 * TQ, TQ), :] = jnp.where(first, q, zero)
            q2_ref[pl.ds(h * 2 * TQ + TQ, TQ), :] = jnp.where(first, zero, q)

    _softmax_init(m_ref, acc_ref)
    split_queries(0)
    scores(0, 0, 0, DIAG)
    _run_tile_program(tab_ref, keys, step, finish, prepare=split_queries)


def _bias_kernel(buckets, idx_ref, tab_ref, o_ref):
    h = pl.program_id(0)
    far = tab_ref[REL_BUCKETS // 2 - 1, h]
    for t in range(2):
        idx = idx_ref[t]
        acc = jnp.zeros(idx.shape, F32)
        for b in buckets[t]:
            acc = jnp.where(idx == b, (tab_ref[b, h] - far) * LOG2E, acc)
        o_ref[0, t] = acc


def _t5_bucket_np(rel):
    nb = REL_BUCKETS // 2
    max_exact = nb // 2
    ret = np.where(rel > 0, nb, 0)
    n = np.abs(rel)
    nf = np.maximum(n, 1).astype(np.float32)
    large = max_exact + (np.log(nf / np.float32(max_exact)) / np.float32(math.log(REL_MAX_DIST / max_exact))
                         * np.float32(nb - max_exact)).astype(np.int32)
    large = np.minimum(large, nb - 1)
    return (ret + np.where(n < max_exact, n, large)).astype(np.int32)


def _bias_tiles(rel_table):
    assert TK >= REL_MAX_DIST
    rel = np.stack([(np.arange(TK)[:, None] + (t - 1) * TK) - np.arange(TQ)[None, :] for t in range(2)])
    idx_np = _t5_bucket_np(rel)
    buckets = [[int(b) for b in np.unique(idx_np[t])] for t in range(2)]
    idx = jnp.asarray(idx_np)
    return pl.pallas_call(
        functools.partial(_bias_kernel, buckets),
        name="rel_bias",
        grid=(DIFF_HEADS,),
        in_specs=[pl.BlockSpec((2, TK, TQ), lambda h: (0, 0, 0)),
                  pl.BlockSpec(memory_space=pltpu.SMEM)],
        out_specs=pl.BlockSpec((1, 2, TK, TQ), lambda h: (h, 0, 0, 0)),
        out_shape=jax.ShapeDtypeStruct((DIFF_HEADS, 2, TK, TQ), F32),
        compiler_params=_params(1),
    )(idx, rel_table.astype(F32))


def _attention_call(kernel_fn, name, q, k, vt, extra, out_width, scratch, B, S, program=None):
    T = q.shape[0]
    smem = [] if program is None else [jnp.asarray(program)]
    seq = lambda a: pl.BlockSpec((S, a.shape[1]), lambda b: (b, 0))
    const = lambda a: pl.BlockSpec(a.shape, lambda b: (0,) * a.ndim, pipeline_mode=pl.Buffered(1))
    return pl.pallas_call(
        kernel_fn,
        name=name,
        grid=(B,),
        in_specs=[pl.BlockSpec(memory_space=pltpu.SMEM) for _ in smem]
        + [seq(q), seq(k), pl.BlockSpec((S // TK,) + vt.shape[1:], lambda b: (b, 0, 0))]
        + [const(a) for a in extra],
        out_specs=pl.BlockSpec((S, out_width), lambda b: (b, 0)),
        out_shape=jax.ShapeDtypeStruct((T, out_width), BF16),
        scratch_shapes=scratch,
        compiler_params=_params(1),
    )(*smem, q, k, vt, *extra)


def _softmax_scratch(heads, n, dv):
    return ([pltpu.VMEM((heads, 1, n), F32)] * 2 + [pltpu.VMEM((heads * (dv + ONES_ROWS), n), F32)]
            + [pltpu.VMEM((2, heads, 1, n), F32), pltpu.VMEM((2, heads * TK, n), F32)]
            + [pltpu.VMEM((heads * TK, n), BF16)])


def _mla_attention(qm, km, vmt, B, S):
    per_q = MLA_TQ // TK
    kind = lambda qi, kj: DIAG if kj == per_q * (qi + 1) - 1 else NEAR if kj == per_q * qi else FAR
    table, keys = _tile_program(S // MLA_TQ, kind, kv_per_q=per_q)
    return _attention_call(functools.partial(_mla_attn_kernel, keys), "mla_attn", qm, km, vmt, (),
                           MLA_HEADS * MLA_V, _softmax_scratch(MLA_HEADS, MLA_TQ, MLA_V), B, S, program=table)


def _sb_attention(qs, ks, vst, B, S):
    n = 2 * SB_HEADS * TK
    scratch = ([pltpu.VMEM((SB_HEADS, 1, TQ), F32), pltpu.VMEM((2 * SB_HEADS, 1, TQ), F32), pltpu.VMEM((SB_W, TQ), F32)]
               + [pltpu.VMEM((n, TQ), F32)] * 2 + [pltpu.VMEM((n, TQ), BF16)] * 2)
    return _attention_call(_sb_attn_kernel, "sb_attn", qs, ks, vst, (), SB_W, scratch, B, S)


def _diff_attention(qd, kd, vdt, bias, lamp, g, lam_init, B, S):
    table, keys = _tile_program(S // TQ, _tile_kind)
    return _attention_call(functools.partial(_diff_attn_kernel, lam_init, keys), "diff_attn", qd, kd, vdt,
                           (bias, lamp, g), DIFF_W,
                           _softmax_scratch(DIFF_HEADS, 2 * TQ, LANES) + [pltpu.VMEM((DIFF_HEADS * 2 * TQ, LANES), BF16)],
                           B, S,
                           program=table)


def _out0_in1_kernel(om_ref, os_ref, z_ref, x_ref, gate_ref, wo_ref, shift_ref, scale_ref, g_ref, w1_ref,
                     x1_ref, qd_ref, kd_ref, vdt_ref, zd_ref):
    z = z_ref[...].astype(F32)
    y = jnp.concatenate([om_ref[...].astype(F32), os_ref[...].astype(F32)], axis=1) * _silu(z)
    out = _dot(y.astype(BF16), wo_ref[...])
    x1 = x_ref[...] + gate_ref[0] * out
    x1_ref[...] = x1
    h = (_rms(x1, NORM_EPS) * (g_ref[...] * (1.0 + scale_ref[0])) + shift_ref[0]).astype(BF16)
    qd_ref[...] = (_dot(h, w1_ref[:, 0:DIFF_W]) * (DIFF_HEAD_DIM ** -0.5 * LOG2E)).astype(BF16)
    kd_ref[...] = _dot(h, w1_ref[:, DIFF_W:2 * DIFF_W]).astype(BF16)
    _store_kv_tiles(vdt_ref, _dot(h, w1_ref[:, 2 * DIFF_W:3 * DIFF_W]).T)
    zd_ref[...] = _dot(h, w1_ref[:, 3 * DIFF_W:4 * DIFF_W]).astype(BF16)


def _layer0_out_layer1_in(om, osb, z, x2, mod3, wo, g1, w1, B):
    T, D = x2.shape
    nt = T // TM
    per_b = nt // B
    full = lambda a: pl.BlockSpec(a.shape, lambda i: (0,) * a.ndim)
    tok = lambda w: pl.BlockSpec((TM, w), lambda i: (i, 0))
    mod = lambda row: pl.BlockSpec((1, 1, D), lambda i: (row * B + i // per_b, 0, 0))
    tok_out = jax.ShapeDtypeStruct((T, DIFF_W), BF16)
    return pl.pallas_call(
        _out0_in1_kernel,
        name="l0_out_l1_in",
        grid=(nt,),
        in_specs=[tok(om.shape[1]), tok(osb.shape[1]), tok(D), tok(D), mod(2), full(wo),
                  mod(3), mod(4), full(g1), full(w1)],
        out_specs=[tok(D), tok(DIFF_W), tok(DIFF_W),
                   pl.BlockSpec((TM // TK, DIFF_W, TK), lambda i: (i, 0, 0)), tok(DIFF_W)],
        out_shape=[jax.ShapeDtypeStruct((T, D), F32), tok_out, tok_out,
                   jax.ShapeDtypeStruct((T // TK, DIFF_W, TK), BF16), tok_out],
        compiler_params=_params(1),
    )(om, osb, z, x2, mod3, wo, mod3, mod3, g1, w1)


def _out1_kernel(o_ref, z_ref, x_ref, gate_ref, wo_ref, g_ref, y_ref):
    y = o_ref[...].astype(F32) * _silu(z_ref[...].astype(F32))
    out = _dot(y.astype(BF16), wo_ref[...])
    x2 = x_ref[...] + gate_ref[0] * out
    y_ref[...] = _rms(x2, NORM_EPS) * g_ref[...]


def _layer1_out(od, zd, x1, mod3, wo, gf, B):
    T, D = x1.shape
    nt = T // TM_OUT
    per_b = nt // B
    full = lambda a: pl.BlockSpec(a.shape, lambda i: (0,) * a.ndim)
    tok = lambda w: pl.BlockSpec((TM_OUT, w), lambda i: (i, 0))
    return pl.pallas_call(
        _out1_kernel,
        name="l1_out",
        grid=(nt,),
        in_specs=[tok(D), tok(D), tok(D), pl.BlockSpec((1, 1, D), lambda i: (5 * B + i // per_b, 0, 0)),
                  full(wo), full(gf)],
        out_specs=tok(D),
        out_shape=jax.ShapeDtypeStruct((T, D), F32),
        compiler_params=_params(1),
    )(od, zd, x1, mod3, wo, gf)


def kernel(x, c, pos_offset, rel_bias_table, ada_w, ada_b, norm_g, final_g, ab_w_in, ab_q_norm_g, ab_kv_norm_g,
           ab_w_uq, ab_w_ukv, ab_w_out, dif_w_in, dif_lam_q1, dif_lam_k1, dif_lam_q2, dif_lam_k2, dif_subln_g,
           dif_w_out):
    B, S, D = x.shape
    assert D == D_MODEL and S % TQ == 0 and TQ == TK and S % TM == 0 and S % TM_OUT == 0 and TM % TK == 0
    x2 = x.reshape(B * S, D)

    mod3 = _modulation(c, ada_w, ada_b)

    w0, wqa, wk, wvt, wvst = _prep_ab_weights(ab_w_in[0], ab_w_uq[0], ab_w_ukv[0])
    qm, km, vmt, qs, ks, vst, z = _layer0_in(
        x2, pos_offset, mod3, norm_g[0:1], w0, wqa, wk, wvt, wvst, ab_q_norm_g[0:1], ab_kv_norm_g[0:1], B)
    om = _mla_attention(qm, km, vmt, B, S)
    osb = _sb_attention(qs, ks, vst, B, S)

    x1, qd, kd, vdt, zd = _layer0_out_layer1_in(
        om, osb, z, x2, mod3, ab_w_out[0].astype(BF16), norm_g[1:2], dif_w_in[0].astype(BF16), B)
    lam_init = 0.8 - 0.6 * math.exp(-0.3 * 1)
    lamp = jnp.stack([dif_lam_q1[0], dif_lam_k1[0], dif_lam_q2[0], dif_lam_k2[0]]).astype(F32)
    bias = _bias_tiles(rel_bias_table)
    od = _diff_attention(qd, kd, vdt, bias, lamp, dif_subln_g[0:1], lam_init, B, S)
    y = _layer1_out(od, zd, x1, mod3, dif_w_out[0].astype(BF16), final_g[None, :], B)
    return y.reshape(B, S, D)
```

```python
import functools
import math

import numpy as np
import jax
import jax.numpy as jnp
from jax import lax
from jax.experimental import pallas as pl
from jax.experimental.pallas import tpu as pltpu

D_MODEL = 1024
DEPTH = 2
CHUNK = 64

MLA_HEADS = 8
MLA_Q_LORA = 384
MLA_KV_LORA = 256
MLA_NOPE = 64
MLA_ROPE = 32
MLA_V = 64
SB_HEADS = 8
SB_HEAD_DIM = 64
SB_W = SB_HEADS * SB_HEAD_DIM
DIFF_HEADS = 8
DIFF_HEAD_DIM = 64
DIFF_W = DIFF_HEADS * 2 * DIFF_HEAD_DIM

REL_BUCKETS = 32
REL_MAX_DIST = 128
ROPE_THETA = 10000.0
NORM_EPS = 1e-6
SUBLN_EPS = 1e-5
NEG_INF = -1e30
SB_UNDERFLOW_LOG2 = 152.0
LOG2E = math.log2(math.e)

LANES = 128
V7X_VMEM_BYTES = 64 * 1024 * 1024
VMEM_LIMIT = V7X_VMEM_BYTES * 7 // 8

TM = 512
TM_OUT = 1024
TQ = 256
TK = 256
MLA_TQ = TK
SKEW = 0
ONES_ROWS = 16

F32 = jnp.float32
BF16 = jnp.bfloat16


def _silu(z):
    return z * (1.0 / (1.0 + jnp.exp(-z)))


def _dot(a, b):
    return jnp.dot(a, b, preferred_element_type=F32)


def _dot_nt(a, b):
    return lax.dot_general(a, b, (((1,), (1,)), ((), ())), preferred_element_type=F32)


def _rep(x, n):
    return x if n == 1 else jnp.concatenate([x] * n, axis=1)


def _params(n_axes=1):
    return pltpu.CompilerParams(dimension_semantics=("arbitrary",) * n_axes, vmem_limit_bytes=VMEM_LIMIT)


def _store_kv_tiles(ref, xt):
    for j in range(xt.shape[1] // TK):
        ref[j] = xt[:, j * TK:(j + 1) * TK].astype(ref.dtype)


def _mod_kernel(c_ref, w_ref, b_ref, o_ref):
    ca = _silu(c_ref[...]).astype(BF16)
    o_ref[0] = _dot(ca, w_ref[0].astype(BF16)) + b_ref[0]


def _modulation(c, ada_w, ada_b):
    B, D = c.shape
    out = pl.pallas_call(
        _mod_kernel,
        name="ada_mod",
        grid=(DEPTH, 3),
        in_specs=[
            pl.BlockSpec((B, D), lambda i, j: (0, 0)),
            pl.BlockSpec((1, D, D), lambda i, j: (i, 0, j)),
            pl.BlockSpec((1, 1, D), lambda i, j: (i * 3 + j, 0, 0)),
        ],
        out_specs=pl.BlockSpec((1, B, D), lambda i, j: (i * 3 + j, 0, 0)),
        out_shape=jax.ShapeDtypeStruct((DEPTH * 3, B, D), F32),
        compiler_params=_params(2),
    )(c, ada_w, ada_b.reshape(DEPTH * 3, 1, D))
    return out.reshape(DEPTH * 3 * B, 1, D)


def _rope_static_tables(S):
    half = MLA_ROPE // 2
    inv_freq = np.float32(ROPE_THETA) ** (-(np.arange(half, dtype=np.float32) / np.float32(half)))
    inv_freq = inv_freq.astype(np.float32)
    ang = np.arange(S, dtype=np.float64)[:, None] * inv_freq.astype(np.float64)[None, :]
    cos, sin = np.cos(ang), np.sin(ang)
    one = np.ones((S, MLA_NOPE))
    zn = np.zeros((S, MLA_NOPE))
    zp = np.zeros((S, LANES - MLA_NOPE - MLA_ROPE))
    tabs = [np.concatenate(parts, axis=1).astype(np.float32) for parts in (
        (one, cos, cos, zp), (zn, sin, sin, zp), (zn, -cos, cos, zp), (zn, -sin, sin, zp))]
    f = np.concatenate([np.zeros(MLA_NOPE, np.float32), inv_freq, inv_freq,
                        np.zeros(LANES - MLA_NOPE - MLA_ROPE, np.float32)])[None, :]
    return [jnp.asarray(t) for t in tabs], jnp.asarray(f)


def _rope_tiles(pos0, f_ref, cs_ref, ss_ref, cs2_ref, ss2_ref):
    ang0 = pos0.astype(F32) * f_ref[...]
    a, b = jnp.cos(ang0), jnp.sin(ang0)
    ctab = a * cs_ref[...] - b * ss_ref[...]
    stab = b * cs2_ref[...] + a * ss2_ref[...]
    return ctab, stab


AB_SEG = (MLA_Q_LORA, MLA_KV_LORA, SB_W, SB_W, MLA_HEADS * MLA_V + SB_W, LANES)
AB_OFF = tuple(int(v) for v in np.cumsum((0,) + AB_SEG))


def _rms(x, eps):
    return x * lax.rsqrt(jnp.mean(x * x, axis=-1, keepdims=True) + eps)


def _swap_rope_halves(v):
    half = MLA_ROPE // 2
    n = v.shape[1]
    lane = lax.broadcasted_iota(jnp.int32, v.shape, 1) % LANES
    return jnp.where(lane < MLA_NOPE + half, pltpu.roll(v, n - half, 1), pltpu.roll(v, half, 1))


def _in0_kernel(per_b, pos_ref, x_ref, shift_ref, scale_ref, g_ref, w_ref, wqa_ref, wk_ref, wvt_ref,
                wvst_ref, qg_ref, kvg_ref, f_ref, cs_ref, ss_ref, cs2_ref, ss2_ref,
                qm_ref, km_ref, vmt_ref, qs_ref, ks_ref, vst_ref, z_ref):
    x = x_ref[...]
    h = _rms(x, NORM_EPS) * (g_ref[...] * (1.0 + scale_ref[0])) + shift_ref[0]
    h = h.astype(BF16)

    def seg(i):
        return _dot(h, w_ref[:, AB_OFF[i]:AB_OFF[i + 1]])

    ctab, stab = _rope_tiles(pos_ref[pl.program_id(0) // per_b], f_ref, cs_ref, ss_ref, cs2_ref, ss2_ref)
    nh = MLA_HEADS

    cq = (_rms(seg(0), NORM_EPS) * qg_ref[...]).astype(BF16)
    ckv = (_rms(seg(1), NORM_EPS) * kvg_ref[...]).astype(BF16)
    q = _dot(cq, wqa_ref[...])
    z_ref[...] = seg(4).astype(BF16)
    q = q * _rep(ctab, nh) + _swap_rope_halves(q) * _rep(stab, nh)
    qm_ref[...] = (q * ((MLA_NOPE + MLA_ROPE) ** -0.5 * LOG2E)).astype(BF16)

    kr = seg(5)
    kn = _dot(ckv, wk_ref[...])
    qs_ref[...] = (seg(2) * (SB_HEAD_DIM ** -0.5 * LOG2E)).astype(BF16)
    krope = kr * ctab + _swap_rope_halves(kr) * stab
    km_ref[...] = (kn + _rep(krope, nh)).astype(BF16)
    vmt = _dot(ckv, wvt_ref[...])
    ks_ref[...] = seg(3).astype(BF16)
    _store_kv_tiles(vmt_ref, vmt.T)
    _store_kv_tiles(vst_ref, _dot(h, wvst_ref[...]).T)


def _prep_ab_weights(w_in, w_uq, w_ukv):
    D = w_in.shape[0]
    c = np.cumsum([MLA_Q_LORA, MLA_KV_LORA, MLA_ROPE, SB_W, SB_W, SB_W]).tolist()
    w_kr = w_in[:, c[1]:c[2]]
    zl = jnp.zeros((D, MLA_NOPE), w_in.dtype)
    zr = jnp.zeros((D, LANES - MLA_NOPE - MLA_ROPE), w_in.dtype)
    kr_a = jnp.concatenate([zl, w_kr, zr], axis=1)
    w0 = jnp.concatenate([
        w_in[:, :c[1]],
        w_in[:, c[2]:c[3]],
        w_in[:, c[3]:c[4]],
        w_in[:, c[5]:],
        kr_a], axis=1).astype(BF16)
    wvst = w_in[:, c[4]:c[5]].astype(BF16)

    hq = MLA_NOPE + MLA_ROPE
    uq = w_uq.reshape(MLA_Q_LORA, MLA_HEADS, hq)
    pad = jnp.zeros((MLA_Q_LORA, MLA_HEADS, LANES - hq), w_uq.dtype)
    wqa = jnp.concatenate([uq, pad], axis=2).reshape(MLA_Q_LORA, MLA_HEADS * LANES).astype(BF16)
    ukv = w_ukv.reshape(MLA_KV_LORA, MLA_HEADS, MLA_NOPE + MLA_V)
    zk = jnp.zeros((MLA_KV_LORA, MLA_HEADS, LANES - MLA_NOPE), w_ukv.dtype)
    wk = jnp.concatenate([ukv[:, :, :MLA_NOPE], zk], axis=2).reshape(MLA_KV_LORA, MLA_HEADS * LANES).astype(BF16)
    wvt = ukv[:, :, MLA_NOPE:].reshape(MLA_KV_LORA, MLA_HEADS * MLA_V).astype(BF16)
    return w0, wqa, wk, wvt, wvst


def _layer0_in(x2, pos_offset, mod3, g, w0, wqa, wk, wvt, wvst, qg, kvg, B):
    T, D = x2.shape
    nt = T // TM
    per_b = nt // B
    rope_tabs, rope_f = _rope_static_tables(T // B)
    full = lambda a: pl.BlockSpec(a.shape, lambda i: (0,) * a.ndim)
    tok = lambda w: pl.BlockSpec((TM, w), lambda i: (i, 0))
    seq = pl.BlockSpec((TM, LANES), lambda i: (i % per_b, 0))
    kvt = lambda w: pl.BlockSpec((TM // TK, w, TK), lambda i: (i, 0, 0))
    tok_out = lambda w: jax.ShapeDtypeStruct((T, w), BF16)
    kvt_out = lambda w: jax.ShapeDtypeStruct((T // TK, w, TK), BF16)
    hv = MLA_HEADS * MLA_V
    return pl.pallas_call(
        functools.partial(_in0_kernel, per_b),
        name="l0_in",
        grid=(nt,),
        in_specs=[
            pl.BlockSpec(memory_space=pltpu.SMEM),
            tok(D),
            pl.BlockSpec((1, 1, D), lambda i: (0 * B + i // per_b, 0, 0)),
            pl.BlockSpec((1, 1, D), lambda i: (1 * B + i // per_b, 0, 0)),
            full(g), full(w0), full(wqa), full(wk), full(wvt), full(wvst), full(qg), full(kvg),
            full(rope_f), seq, seq, seq, seq,
        ],
        out_specs=[tok(MLA_HEADS * LANES), tok(MLA_HEADS * LANES), kvt(hv), tok(SB_W), tok(SB_W), kvt(SB_W),
                   tok(hv + SB_W)],
        out_shape=[tok_out(MLA_HEADS * LANES), tok_out(MLA_HEADS * LANES), kvt_out(hv), tok_out(SB_W),
                   tok_out(SB_W), kvt_out(SB_W), tok_out(hv + SB_W)],
        compiler_params=_params(1),
    )(pos_offset, x2, mod3, mod3, g, w0, wqa, wk, wvt, wvst, qg, kvg, rope_f, *rope_tabs)


def _chunk_mask_t():
    k = lax.broadcasted_iota(jnp.int32, (TK, TQ), 0)
    q = lax.broadcasted_iota(jnp.int32, (TK, TQ), 1)
    shift = CHUNK.bit_length() - 1
    return (k >> shift) <= (q >> shift)


def _tile_start(i, size):
    return i * size if isinstance(i, int) else pl.multiple_of(i * size, size)


FAR, NEAR, DIAG = 0, 1, 2
PROGRAM_ROW = 5


def _tile_kind(qi, kj):
    return DIAG if kj == qi else NEAR if kj == qi - 1 else FAR


def _tile_program(n_tiles, kind_of, kv_per_q=1):
    items = [(qi, kj) for qi in range(n_tiles) for kj in range(kv_per_q * (qi + 1))]
    assert len(items) % 2 == 0 and n_tiles >= 2
    items.append((n_tiles - 1, 0))
    kinds = [kind_of(qi, kj) for qi, kj in items]
    keys, rows = [], []
    for t in range(len(items) - 1):
        key = (kinds[t], kinds[t + 1])
        if key not in keys:
            keys.append(key)
        rows.append([keys.index(key), *items[t], *items[t + 1]])
    return np.asarray(rows, np.int32).reshape(-1), keys


def _run_tile_program(tab_ref, keys, step, finish, prepare=None):
    def variant(key, slot):
        def run(qi, kj, nqi, nkj):
            if key[0] == DIAG and prepare is not None:
                prepare(nqi)
            step(qi, kj, nqi, nkj, slot, key[1])
            if key[0] == DIAG:
                finish(qi)
        return run

    variants = [[variant(key, slot) for key in keys] for slot in range(2)]

    def trip(t, carry):
        for slot in range(2):
            base = (2 * t + slot) * PROGRAM_ROW
            lax.switch(tab_ref[base], variants[slot], *[tab_ref[base + 1 + j] for j in range(PROGRAM_ROW - 1)])
        return carry

    lax.fori_loop(0, tab_ref.shape[0] // (2 * PROGRAM_ROW), trip, 0)


def _softmax_step(s_ref, mc_ref, m_ref, alpha_ref, p_ref, slot, h, rows):
    m_prev = m_ref[h]
    m_new = jnp.maximum(m_prev, mc_ref[slot, h])
    p_ref[rows, :] = jnp.exp2(s_ref[slot, rows, :] - m_new).astype(BF16)
    alpha_ref[h] = jnp.exp2(m_prev - m_new)
    m_ref[h] = m_new


def _values_lhs(vt):
    return jnp.concatenate([vt, jnp.ones((ONES_ROWS, vt.shape[1]), vt.dtype)], axis=0)


def _softmax_init(m_ref, acc_ref):
    m_ref[...] = jnp.full(m_ref.shape, -jnp.inf, F32)
    acc_ref[...] = jnp.zeros(acc_ref.shape, F32)


def _mla_attn_kernel(keys, tab_ref, q_ref, k_ref, vt_ref, o_ref, m_ref, alpha_ref, acc_ref, mc_ref, s_ref, p_ref):
    S = q_ref.shape[0]
    shift = CHUNK.bit_length() - 1
    kc = lax.broadcasted_iota(jnp.int32, (TK, MLA_TQ), 0) >> shift
    qc = lax.broadcasted_iota(jnp.int32, (TK, MLA_TQ), 1) >> shift
    per_q = MLA_TQ // TK
    masks = {DIAG: kc + (per_q - 1) * (TK >> shift) <= qc, NEAR: kc <= qc}
    krows = [pl.ds(h * TK, TK) for h in range(MLA_HEADS)]
    vrows = [pl.ds(h * MLA_V, MLA_V) for h in range(MLA_HEADS)]
    arows = [pl.ds(h * (MLA_V + ONES_ROWS), MLA_V + ONES_ROWS) for h in range(MLA_HEADS)]

    def out_t(h):
        base = h * (MLA_V + ONES_ROWS)
        return acc_ref[pl.ds(base, MLA_V), :] * (1.0 / acc_ref[pl.ds(base + MLA_V, 1), :])

    def scores_head(h, qi, kj, slot, kind):
        q0 = _tile_start(qi, MLA_TQ)
        k0 = _tile_start(kj, TK)
        hl = slice(h * LANES, (h + 1) * LANES)
        s = _dot_nt(k_ref[pl.ds(k0, TK), hl], q_ref[pl.ds(q0, MLA_TQ), hl])
        if kind != FAR:
            s = jnp.where(masks[kind], s, NEG_INF)
        s_ref[slot, krows[h], :] = s
        mc_ref[slot, h] = jnp.max(s, axis=0, keepdims=True)

    def scores(qi, kj, slot, kind):
        for h in range(MLA_HEADS):
            scores_head(h, qi, kj, slot, kind)

    def values_head(h, kj):
        acc_ref[arows[h], :] = (alpha_ref[h] * acc_ref[arows[h], :]
                                + _dot(_values_lhs(vt_ref[kj, vrows[h], :]), p_ref[krows[h], :]))

    def step(qi, kj, nqi, nkj, slot, next_kind):
        for h in range(SKEW):
            scores_head(h, nqi, nkj, 1 - slot, next_kind)
        for h in range(MLA_HEADS):
            if h + SKEW < MLA_HEADS:
                scores_head(h + SKEW, nqi, nkj, 1 - slot, next_kind)
            _softmax_step(s_ref, mc_ref, m_ref, alpha_ref, p_ref, slot, h, krows[h])
            if h > 0:
                values_head(h - 1, kj)
        values_head(MLA_HEADS - 1, kj)

    def finish(qi):
        q0 = pl.multiple_of(qi * MLA_TQ, MLA_TQ)
        for j in range(MLA_HEADS // 2):
            ot = jnp.concatenate([out_t(2 * j), out_t(2 * j + 1)], axis=0)
            o_ref[pl.ds(q0, MLA_TQ), j * LANES:(j + 1) * LANES] = ot.T.astype(BF16)
        _softmax_init(m_ref, acc_ref)

    _softmax_init(m_ref, acc_ref)
    scores(0, 0, 0, NEAR if per_q > 1 else DIAG)
    _run_tile_program(tab_ref, keys, step, finish)


def _sb_attn_kernel(q_ref, k_ref, vt_ref, o_ref, r_ref, rs_ref, acc_ref, z_ref, ls_ref, hi_ref, lo_ref):
    S = q_ref.shape[0]
    k_i = lax.broadcasted_iota(jnp.int32, (TK, TQ), 0)
    q_i = lax.broadcasted_iota(jnp.int32, (TK, TQ), 1)
    causal = k_i < q_i
    tri = jnp.where(lax.broadcasted_iota(jnp.int32, (TK, TK), 1) > lax.broadcasted_iota(jnp.int32, (TK, TK), 0),
                    1.0, 0.0).astype(BF16)
    lane = lax.broadcasted_iota(jnp.int32, (TQ, LANES), 1)
    first = lane < SB_HEAD_DIM
    crows = [pl.ds(c * TK, TK) for c in range(2 * SB_HEADS)]
    vrows = [pl.ds(h * SB_HEAD_DIM, SB_HEAD_DIM) for h in range(SB_HEADS)]
    pairs = [slice((h // 2) * LANES, (h // 2 + 1) * LANES) for h in range(SB_HEADS)]

    def scores_chain(c, h, q0, kj):
        k0 = pl.multiple_of(kj * TK, TK)
        q = q_ref[pl.ds(q0, TQ), pairs[h]]
        q = jnp.where(first, q, jnp.zeros_like(q)) if h % 2 == 0 else jnp.where(first, jnp.zeros_like(q), q)
        z_ref[crows[c], :] = _dot_nt(k_ref[pl.ds(k0, TK), pairs[h]], q)

    def logs_chain(c, masked):
        z = z_ref[crows[c], :]
        sp = jnp.log(1.0 + jnp.exp2(-jnp.abs(z))) * LOG2E
        ls = jnp.minimum(z, 0.0) - sp
        l1m = ls - z
        if masked:
            l1m = jnp.where(causal, l1m, 0.0)
        hi = l1m.astype(BF16)
        ls_ref[crows[c], :] = ls
        hi_ref[crows[c], :] = hi
        lo_ref[crows[c], :] = (l1m - hi.astype(F32)).astype(BF16)
        rs_ref[c] = l1m[0:1, :]

    def suffix_chain(c):
        z_ref[crows[c], :] = _dot(tri, hi_ref[crows[c], :]) + _dot(tri, lo_ref[crows[c], :])

    def weights_chain(c, h, masked):
        r_prev = r_ref[h]
        suffix = z_ref[crows[c], :]
        w = jnp.exp2(ls_ref[crows[c], :] + suffix + r_prev)
        if masked:
            w = jnp.where(causal, w, 0.0)
        hi_ref[crows[c], :] = w.astype(BF16)
        r_ref[h] = r_prev + (rs_ref[c] + suffix[0:1, :])

    def values_chain(c, h, kj):
        acc_ref[vrows[h], :] += _dot(vt_ref[kj, vrows[h], :], hi_ref[crows[c], :])

    def step(q0, tiles):
        chains = [(h, kj, masked) for h in range(SB_HEADS) for kj, masked in tiles]
        n = len(chains)
        lead, lag = 2, 2
        for c in range(lead):
            scores_chain(c, chains[c][0], q0, chains[c][1])
        for t in range(n + lag):
            if t + lead < n:
                scores_chain(t + lead, chains[t + lead][0], q0, chains[t + lead][1])
            if t < n:
                logs_chain(t, chains[t][2])
                suffix_chain(t)
            if t >= lag:
                h, kj, masked = chains[t - lag]
                weights_chain(t - lag, h, masked)
                values_chain(t - lag, h, kj)

    def start():
        r_ref[...] = jnp.zeros(r_ref.shape, F32)
        acc_ref[...] = jnp.zeros(acc_ref.shape, F32)

    def finish(q0):
        for j in range(SB_HEADS // 2):
            ot = acc_ref[pl.ds(j * LANES, LANES), :]
            o_ref[pl.ds(q0, TQ), j * LANES:(j + 1) * LANES] = ot.T.astype(BF16)

    def still_live():
        return (jnp.max(r_ref[...]) > -SB_UNDERFLOW_LOG2).astype(jnp.int32)

    start()
    step(0, [(0, True)])
    finish(0)

    def q_body(qi, carry):
        q0 = pl.multiple_of(qi * TQ, TQ)
        start()
        step(q0, [(qi, True), (qi - 1, False)])

        def cond(c):
            t, live = c
            return jnp.logical_and(t < qi, live > 0)

        def body(c):
            t, _ = c
            step(q0, [(qi - 1 - t, False)])
            return t + 1, still_live()

        lax.while_loop(cond, body, (jnp.int32(1), still_live()))
        finish(q0)
        return carry

    lax.fori_loop(1, S // TQ, q_body, 0)


def _diff_attn_kernel(lam_init, keys, tab_ref, q_ref, k_ref, vt_ref, bias_ref, lamp_ref, g_ref, o_ref,
                      m_ref, alpha_ref, acc_ref, mc_ref, s_ref, p_ref, q2_ref):
    S = q_ref.shape[0]
    mask = _chunk_mask_t()
    mask2 = jnp.concatenate([mask, mask], axis=1)
    lane = lax.broadcasted_iota(jnp.int32, (TQ, LANES), 1)
    first = lane < DIFF_HEAD_DIM
    lp = lamp_ref[...]
    lam = (jnp.exp(jnp.sum(lp[0:1] * lp[1:2], axis=-1, keepdims=True))
           - jnp.exp(jnp.sum(lp[2:3] * lp[3:4], axis=-1, keepdims=True)) + lam_init)
    krows = [pl.ds(h * TK, TK) for h in range(DIFF_HEADS)]
    vrows = [pl.ds(h * LANES, LANES) for h in range(DIFF_HEADS)]
    arows = [pl.ds(h * (LANES + ONES_ROWS), LANES + ONES_ROWS) for h in range(DIFF_HEADS)]
    gain = g_ref[...] * (1.0 - lam_init)

    def scores_head(h, qi, kj, slot, kind):
        q0 = _tile_start(qi, TQ)
        k0 = _tile_start(kj, TK)
        hl = slice(h * LANES, (h + 1) * LANES)
        s = _dot_nt(k_ref[pl.ds(k0, TK), hl], q2_ref[pl.ds(h * 2 * TQ, 2 * TQ), :])
        if kind != FAR:
            s = s + _rep(bias_ref[h, kind - 1], 2)
        if kind == DIAG:
            s = jnp.where(mask2, s, NEG_INF)
        s_ref[slot, krows[h], :] = s
        mc_ref[slot, h] = jnp.max(s, axis=0, keepdims=True)

    def scores(qi, kj, slot, kind):
        for h in range(DIFF_HEADS):
            scores_head(h, qi, kj, slot, kind)

    def values_head(h, kj):
        acc_ref[arows[h], :] = (alpha_ref[h] * acc_ref[arows[h], :]
                                + _dot(_values_lhs(vt_ref[kj, vrows[h], :]), p_ref[krows[h], :]))

    def step(qi, kj, nqi, nkj, slot, next_kind):
        for h in range(SKEW):
            scores_head(h, nqi, nkj, 1 - slot, next_kind)
        for h in range(DIFF_HEADS):
            _softmax_step(s_ref, mc_ref, m_ref, alpha_ref, p_ref, slot, h, krows[h])
            if h + SKEW < DIFF_HEADS:
                scores_head(h + SKEW, nqi, nkj, 1 - slot, next_kind)
            values_head(h, kj)

    def finish(qi):
        q0 = pl.multiple_of(qi * TQ, TQ)
        for h in range(DIFF_HEADS):
            base = h * (LANES + ONES_ROWS)
            inv_l = 1.0 / acc_ref[pl.ds(base + LANES, 1), :]
            ot = (acc_ref[pl.ds(base, LANES), pl.ds(0, TQ)] * inv_l[:, :TQ]
                  - acc_ref[pl.ds(base, LANES), pl.ds(TQ, TQ)] * (lam * inv_l[:, TQ:]))
            ot = ot * lax.rsqrt(jnp.mean(ot * ot, axis=0, keepdims=True) + SUBLN_EPS)
            o_ref[pl.ds(q0, TQ), h * LANES:(h + 1) * LANES] = (ot.T * gain).astype(BF16)
        _softmax_init(m_ref, acc_ref)

    def split_queries(qi):
        q0 = _tile_start(qi, TQ)
        for h in range(DIFF_HEADS):
            q = q_ref[pl.ds(q0, TQ), h * LANES:(h + 1) * LANES]
            zero = jnp.zeros_like(q)
            q2_ref[pl.ds(h * 2 * TQ, TQ), :] = jnp.where(first, q, zero)
            q2_ref[pl.ds(h * 2 * TQ + TQ, TQ), :] = jnp.where(first, zero, q)

    _softmax_init(m_ref, acc_ref)
    split_queries(0)
    scores(0, 0, 0, DIAG)
    _run_tile_program(tab_ref, keys, step, finish, prepare=split_queries)


def _bias_kernel(buckets, idx_ref, tab_ref, o_ref):
    h = pl.program_id(0)
    far = tab_ref[REL_BUCKETS // 2 - 1, h]
    for t in range(2):
        idx = idx_ref[t]
        acc = jnp.zeros(idx.shape, F32)
        for b in buckets[t]:
            acc = jnp.where(idx == b, (tab_ref[b, h] - far) * LOG2E, acc)
        o_ref[0, t] = acc


def _t5_bucket_np(rel):
    nb = REL_BUCKETS // 2
    max_exact = nb // 2
    ret = np.where(rel > 0, nb, 0)
    n = np.abs(rel)
    nf = np.maximum(n, 1).astype(np.float32)
    large = max_exact + (np.log(nf / np.float32(max_exact)) / np.float32(math.log(REL_MAX_DIST / max_exact))
                         * np.float32(nb - max_exact)).astype(np.int32)
    large = np.minimum(large, nb - 1)
    return (ret + np.where(n < max_exact, n, large)).astype(np.int32)


def _bias_tiles(rel_table):
    assert TK >= REL_MAX_DIST
    rel = np.stack([(np.arange(TK)[:, None] + (t - 1) * TK) - np.arange(TQ)[None, :] for t in range(2)])
    idx_np = _t5_bucket_np(rel)
    buckets = [[int(b) for b in np.unique(idx_np[t])] for t in range(2)]
    idx = jnp.asarray(idx_np)
    return pl.pallas_call(
        functools.partial(_bias_kernel, buckets),
        name="rel_bias",
        grid=(DIFF_HEADS,),
        in_specs=[pl.BlockSpec((2, TK, TQ), lambda h: (0, 0, 0)),
                  pl.BlockSpec(memory_space=pltpu.SMEM)],
        out_specs=pl.BlockSpec((1, 2, TK, TQ), lambda h: (h, 0, 0, 0)),
        out_shape=jax.ShapeDtypeStruct((DIFF_HEADS, 2, TK, TQ), F32),
        compiler_params=_params(1),
    )(idx, rel_table.astype(F32))


def _attention_call(kernel_fn, name, q, k, vt, extra, out_width, scratch, B, S, program=None):
    T = q.shape[0]
    smem = [] if program is None else [jnp.asarray(program)]
    seq = lambda a: pl.BlockSpec((S, a.shape[1]), lambda b: (b, 0))
    const = lambda a: pl.BlockSpec(a.shape, lambda b: (0,) * a.ndim, pipeline_mode=pl.Buffered(1))
    return pl.pallas_call(
        kernel_fn,
        name=name,
        grid=(B,),
        in_specs=[pl.BlockSpec(memory_space=pltpu.SMEM) for _ in smem]
        + [seq(q), seq(k), pl.BlockSpec((S // TK,) + vt.shape[1:], lambda b: (b, 0, 0))]
        + [const(a) for a in extra],
        out_specs=pl.BlockSpec((S, out_width), lambda b: (b, 0)),
        out_shape=jax.ShapeDtypeStruct((T, out_width), BF16),
        scratch_shapes=scratch,
        compiler_params=_params(1),
    )(*smem, q, k, vt, *extra)


def _softmax_scratch(heads, n, dv):
    return ([pltpu.VMEM((heads, 1, n), F32)] * 2 + [pltpu.VMEM((heads * (dv + ONES_ROWS), n), F32)]
            + [pltpu.VMEM((2, heads, 1, n), F32), pltpu.VMEM((2, heads * TK, n), F32)]
            + [pltpu.VMEM((heads * TK, n), BF16)])


def _mla_attention(qm, km, vmt, B, S):
    per_q = MLA_TQ // TK
    kind = lambda qi, kj: DIAG if kj == per_q * (qi + 1) - 1 else NEAR if kj == per_q * qi else FAR
    table, keys = _tile_program(S // MLA_TQ, kind, kv_per_q=per_q)
    return _attention_call(functools.partial(_mla_attn_kernel, keys), "mla_attn", qm, km, vmt, (),
                           MLA_HEADS * MLA_V, _softmax_scratch(MLA_HEADS, MLA_TQ, MLA_V), B, S, program=table)


def _sb_attention(qs, ks, vst, B, S):
    n = 2 * SB_HEADS * TK
    scratch = ([pltpu.VMEM((SB_HEADS, 1, TQ), F32), pltpu.VMEM((2 * SB_HEADS, 1, TQ), F32), pltpu.VMEM((SB_W, TQ), F32)]
               + [pltpu.VMEM((n, TQ), F32)] * 2 + [pltpu.VMEM((n, TQ), BF16)] * 2)
    return _attention_call(_sb_attn_kernel, "sb_attn", qs, ks, vst, (), SB_W, scratch, B, S)


def _diff_attention(qd, kd, vdt, bias, lamp, g, lam_init, B, S):
    table, keys = _tile_program(S // TQ, _tile_kind)
    return _attention_call(functools.partial(_diff_attn_kernel, lam_init, keys), "diff_attn", qd, kd, vdt,
                           (bias, lamp, g), DIFF_W,
                           _softmax_scratch(DIFF_HEADS, 2 * TQ, LANES) + [pltpu.VMEM((DIFF_HEADS * 2 * TQ, LANES), BF16)],
                           B, S,
                           program=table)


def _out0_in1_kernel(om_ref, os_ref, z_ref, x_ref, gate_ref, wo32_ref, shift_ref, scale_ref, g_ref, w132_ref,
                     x1_ref, qd_ref, kd_ref, vdt_ref, zd_ref, wo_ref, w1_ref):
    @pl.when(pl.program_id(0) == 0)
    def _():
        wo_ref[...] = wo32_ref[...].astype(BF16)
        for j in range(w1_ref.shape[1] // DIFF_W):
            w1_ref[:, j * DIFF_W:(j + 1) * DIFF_W] = w132_ref[:, j * DIFF_W:(j + 1) * DIFF_W].astype(BF16)

    z = z_ref[...].astype(F32)
    y = jnp.concatenate([om_ref[...].astype(F32), os_ref[...].astype(F32)], axis=1) * _silu(z)
    out = _dot(y.astype(BF16), wo_ref[...])
    x1 = x_ref[...] + gate_ref[0] * out
    x1_ref[...] = x1
    h = (_rms(x1, NORM_EPS) * (g_ref[...] * (1.0 + scale_ref[0])) + shift_ref[0]).astype(BF16)
    qd_ref[...] = (_dot(h, w1_ref[:, 0:DIFF_W]) * (DIFF_HEAD_DIM ** -0.5 * LOG2E)).astype(BF16)
    kd_ref[...] = _dot(h, w1_ref[:, DIFF_W:2 * DIFF_W]).astype(BF16)
    _store_kv_tiles(vdt_ref, _dot(h, w1_ref[:, 2 * DIFF_W:3 * DIFF_W]).T)
    zd_ref[...] = _dot(h, w1_ref[:, 3 * DIFF_W:4 * DIFF_W]).astype(BF16)


def _layer0_out_layer1_in(om, osb, z, x2, mod3, wo, g1, w1, B):
    T, D = x2.shape
    nt = T // TM
    per_b = nt // B
    full = lambda a: pl.BlockSpec(a.shape, lambda i: (0,) * a.ndim)
    once = lambda a: pl.BlockSpec(a.shape, lambda i: (0,) * a.ndim, pipeline_mode=pl.Buffered(1))
    tok = lambda w: pl.BlockSpec((TM, w), lambda i: (i, 0))
    mod = lambda row: pl.BlockSpec((1, 1, D), lambda i: (row * B + i // per_b, 0, 0))
    tok_out = jax.ShapeDtypeStruct((T, DIFF_W), BF16)
    return pl.pallas_call(
        _out0_in1_kernel,
        name="l0_out_l1_in",
        grid=(nt,),
        in_specs=[tok(om.shape[1]), tok(osb.shape[1]), tok(D), tok(D), mod(2), once(wo),
                  mod(3), mod(4), full(g1), once(w1)],
        scratch_shapes=[pltpu.VMEM(wo.shape, BF16), pltpu.VMEM(w1.shape, BF16)],
        out_specs=[tok(D), tok(DIFF_W), tok(DIFF_W),
                   pl.BlockSpec((TM // TK, DIFF_W, TK), lambda i: (i, 0, 0)), tok(DIFF_W)],
        out_shape=[jax.ShapeDtypeStruct((T, D), F32), tok_out, tok_out,
                   jax.ShapeDtypeStruct((T // TK, DIFF_W, TK), BF16), tok_out],
        compiler_params=_params(1),
    )(om, osb, z, x2, mod3, wo, mod3, mod3, g1, w1)


def _out1_kernel(o_ref, z_ref, x_ref, gate_ref, wo_ref, g_ref, y_ref):
    y = o_ref[...].astype(F32) * _silu(z_ref[...].astype(F32))
    out = _dot(y.astype(BF16), wo_ref[...])
    x2 = x_ref[...] + gate_ref[0] * out
    y_ref[...] = _rms(x2, NORM_EPS) * g_ref[...]


def _layer1_out(od, zd, x1, mod3, wo, gf, B):
    T, D = x1.shape
    nt = T // TM_OUT
    per_b = nt // B
    full = lambda a: pl.BlockSpec(a.shape, lambda i: (0,) * a.ndim)
    tok = lambda w: pl.BlockSpec((TM_OUT, w), lambda i: (i, 0))
    return pl.pallas_call(
        _out1_kernel,
        name="l1_out",
        grid=(nt,),
        in_specs=[tok(D), tok(D), tok(D), pl.BlockSpec((1, 1, D), lambda i: (5 * B + i // per_b, 0, 0)),
                  full(wo), full(gf)],
        out_specs=tok(D),
        out_shape=jax.ShapeDtypeStruct((T, D), F32),
        compiler_params=_params(1),
    )(od, zd, x1, mod3, wo, gf)


def kernel(x, c, pos_offset, rel_bias_table, ada_w, ada_b, norm_g, final_g, ab_w_in, ab_q_norm_g, ab_kv_norm_g,
           ab_w_uq, ab_w_ukv, ab_w_out, dif_w_in, dif_lam_q1, dif_lam_k1, dif_lam_q2, dif_lam_k2, dif_subln_g,
           dif_w_out):
    B, S, D = x.shape
    assert D == D_MODEL and S % TQ == 0 and TQ == TK and S % TM == 0 and S % TM_OUT == 0 and TM % TK == 0
    x2 = x.reshape(B * S, D)

    mod3 = _modulation(c, ada_w, ada_b)

    w0, wqa, wk, wvt, wvst = _prep_ab_weights(ab_w_in[0], ab_w_uq[0], ab_w_ukv[0])
    qm, km, vmt, qs, ks, vst, z = _layer0_in(
        x2, pos_offset, mod3, norm_g[0:1], w0, wqa, wk, wvt, wvst, ab_q_norm_g[0:1], ab_kv_norm_g[0:1], B)
    om = _mla_attention(qm, km, vmt, B, S)
    osb = _sb_attention(qs, ks, vst, B, S)

    x1, qd, kd, vdt, zd = _layer0_out_layer1_in(
        om, osb, z, x2, mod3, ab_w_out[0], norm_g[1:2], dif_w_in[0], B)
    lam_init = 0.8 - 0.6 * math.exp(-0.3 * 1)
    lamp = jnp.stack([dif_lam_q1[0], dif_lam_k1[0], dif_lam_q2[0], dif_lam_k2[0]]).astype(F32)
    bias = _bias_tiles(rel_bias_table)
    od = _diff_attention(qd, kd, vdt, bias, lamp, dif_subln_g[0:1], lam_init, B, S)
    y = _layer1_out(od, zd, x1, mod3, dif_w_out[0].astype(BF16), final_g[None, :], B)
    return y.reshape(B, S, D)
```
